```python
import functools
import jax, jax.numpy as jnp
from jax import lax
import numpy as np

D_MODEL = 1024
BATCH = 32
SEQ = 2048
DEPTH = 2

CTX_LEN = 256
GRID_W = 64
RWKV_HEADS = 8
RWKV_HEAD = 64
RWKV_W = RWKV_HEADS * RWKV_HEAD
DECAY_LORA = 64
AAA_LORA = 64
GATE_LORA = 128
LNX_EPS = 64e-5
ATT_HEADS = 8
KV_HEADS = 2
HEAD_DIM = 64
GROUPS = ATT_HEADS // KV_HEADS
ATT_W = ATT_HEADS * HEAD_DIM
KV_W = KV_HEADS * HEAD_DIM
Q_BLOCK = 128
ROPE_THETA = 10000.0
ROPE_AXIS_DIM = HEAD_DIM // 2
N_BRANCH = 2
R_COLS = 3 * RWKV_W + DECAY_LORA + AAA_LORA + GATE_LORA
A_COLS = ATT_W + 2 * KV_W
G_COLS = N_BRANCH * D_MODEL
N_IN = R_COLS + A_COLS + G_COLS
FFN_DIM = 3584
N_EXPERTS = 8
TOP_K = 2
MOE_BLOCK = 512
N_DENSE = (DEPTH + 1) // 2
N_MOE = DEPTH // 2
NORM_EPS = 1e-6

kernel_name = "hybrid_rwkv7_gqa_moe_dit_block"

F32 = jnp.float32


def rmsnorm(x, gain):
    xf = x.astype(F32)
    y = xf * lax.rsqrt(jnp.mean(xf * xf, axis=-1, keepdims=True) + NORM_EPS)
    return (y * gain.astype(F32)).astype(x.dtype)


def centred_shift(p, mu):
    zero = jnp.zeros_like(p[:, :1])
    prev = jnp.concatenate([zero, p[:, :-1]], axis=1)
    nxt = jnp.concatenate([p[:, 1:], zero], axis=1)
    return p + mu[0] * (prev - p) + mu[1] * (nxt - p)


def rwkv_prepare(pr, mu, w0, w2, a0, a2, g2, k_k, k_a):
    B, T, _ = pr.shape
    heads = lambda t: t.reshape(B, T, RWKV_HEADS, RWKV_HEAD)
    pr = centred_shift(pr, mu)
    cuts = [RWKV_W, 2 * RWKV_W, 3 * RWKV_W, 3 * RWKV_W + DECAY_LORA, 3 * RWKV_W + DECAY_LORA + AAA_LORA]
    r, k, v, wd, ad, gd = jnp.split(pr, cuts, axis=-1)
    kkf = heads(k * k_k).astype(F32)
    kk = kkf / jnp.maximum(jnp.sqrt(jnp.sum(kkf * kkf, axis=-1, keepdims=True)), 1e-12)
    tw = jnp.tanh(wd)
    decays, keys, bs = [], [], []
    for d in range(2):
        z = (w0[d] + tw @ w2[d]).astype(F32)
        decay = jnp.exp(-jnp.exp(-jax.nn.softplus(-z) - 0.5))
        a = jax.nn.sigmoid(a0[d] + ad @ a2[d])
        decays.append(heads(decay))
        keys.append(heads(k * (1 + (a - 1) * k_a)))
        bs.append(kk * heads(a).astype(F32))
    g = jax.nn.sigmoid(gd) @ g2
    return heads(r), heads(v), kk, g, decays, keys, bs


def wkv_scan(state, r, decay, k, v, kk, b, reverse):
    xs = tuple(jnp.moveaxis(t.astype(F32), 1, 0) for t in (r, decay, k, v, -kk, b))

    def step(S, inp):
        r_t, w_t, k_t, v_t, a_t, b_t = inp
        Sa = jnp.einsum('bhvk,bhk->bhv', S, a_t)
        S = S * w_t[:, :, None, :] + Sa[..., None] * b_t[:, :, None, :] + v_t[..., None] * k_t[:, :, None, :]
        return S, jnp.einsum('bhvk,bhk->bhv', S, r_t)

    S, ys = lax.scan(step, state, xs, reverse=reverse)
    return S, jnp.moveaxis(ys, 0, 1)


def rwkv_output(y, r, keys, v, g, r_k, lnx_w, lnx_b):
    B, T = y.shape[:2]
    mean = jnp.mean(y, axis=-1, keepdims=True)
    var = jnp.mean(jnp.square(y - mean), axis=-1, keepdims=True)
    yn = ((y - mean) * lax.rsqrt(var + LNX_EPS)).reshape(B, T, RWKV_W)
    yn = (yn * lnx_w.astype(F32) + lnx_b.astype(F32)).astype(v.dtype)
    kbar = 0.5 * (keys[0] + keys[1])
    bonus = jnp.sum(r * kbar * r_k, axis=-1, keepdims=True) * v
    return (yn + bonus.reshape(B, T, RWKV_W)) * g


def head_rms(t, gain):
    tf = t.astype(F32)
    return (tf * lax.rsqrt(jnp.mean(tf * tf, axis=-1, keepdims=True) + NORM_EPS) * gain.astype(F32)).astype(t.dtype)


def attn_qkv(pa, q_norm, k_norm):
    B, T, _ = pa.shape
    q = pa[..., :ATT_W].reshape(B, T, KV_HEADS, GROUPS, HEAD_DIM)
    k = pa[..., ATT_W:ATT_W + KV_W].reshape(B, T, KV_HEADS, HEAD_DIM)
    v = pa[..., ATT_W + KV_W:].reshape(B, T, KV_HEADS, HEAD_DIM)
    return head_rms(q, q_norm), head_rms(k, k_norm), v


def rope_2d_tables(n_tokens):
    rows = n_tokens // GRID_W
    row, col = jnp.meshgrid(jnp.arange(rows), jnp.arange(GRID_W), indexing='ij')
    inv = ROPE_THETA ** (-jnp.arange(0, ROPE_AXIS_DIM, 2, dtype=F32) / ROPE_AXIS_DIM)
    ang = jnp.concatenate([row.reshape(-1, 1).astype(F32) * inv, col.reshape(-1, 1).astype(F32) * inv], axis=-1)
    return jnp.cos(ang), jnp.sin(ang)


def apply_rope(t, cos, sin):
    shp = t.shape
    tp = t.astype(F32).reshape(shp[:-1] + (HEAD_DIM // 2, 2))
    bshape = (1, shp[1]) + (1,) * (len(shp) - 3) + (HEAD_DIM // 2,)
    c, s = cos.reshape(bshape), sin.reshape(bshape)
    x0, x1 = tp[..., 0], tp[..., 1]
    return jnp.stack([x0 * c - x1 * s, x0 * s + x1 * c], axis=-1).reshape(shp).astype(t.dtype)


def block_attention(q, k, v):
    B, T = q.shape[:2]
    nb = T // Q_BLOCK
    qb = jnp.moveaxis(q.reshape(B, nb, Q_BLOCK, KV_HEADS, GROUPS, HEAD_DIM), 1, 0)
    scale = HEAD_DIM ** -0.5

    def one(qblk):
        s = jnp.einsum('bqhgd,bkhd->bhgqk', qblk, k).astype(F32) * scale
        p = jax.nn.softmax(s, axis=-1).astype(v.dtype)
        return jnp.einsum('bhgqk,bkhd->bqhgd', p, v)

    o = lax.map(one, qb)
    return jnp.moveaxis(o, 0, 1).reshape(B, T, ATT_W)


def merge_branches(pg, y_rwkv, y_att, w_pa, w_pb, w_o):
    gate_a = jax.nn.sigmoid(pg[..., :D_MODEL])
    gate_b = jax.nn.sigmoid(pg[..., D_MODEL:])
    return (gate_a * (y_rwkv @ w_pa) + gate_b * (y_att @ w_pb)) @ w_o


def swiglu(h, wg, wu, wd):
    return (jax.nn.silu(h @ wg) * (h @ wu)) @ wd


def moe_swiglu(h, router, wg, wu, wd):
    shp = h.shape
    hf = h.reshape(-1, D_MODEL)
    T = hf.shape[0]
    logits = (hf @ router).astype(F32)
    top_logit, top_idx = lax.top_k(logits, TOP_K)
    top_w = jax.nn.softmax(top_logit, axis=-1)
    n_assign = T * TOP_K
    e_flat = top_idx.reshape(-1).astype(jnp.int32)
    tok_flat = jnp.repeat(jnp.arange(T, dtype=jnp.int32), TOP_K)
    w_flat = top_w.reshape(-1)
    order = jnp.argsort(e_flat)
    e_sorted = e_flat[order]
    counts = jnp.zeros((N_EXPERTS,), jnp.int32).at[e_flat].add(1)
    starts = jnp.cumsum(counts) - counts
    padded = (counts + MOE_BLOCK - 1) // MOE_BLOCK * MOE_BLOCK
    pends = jnp.cumsum(padded)
    pstarts = pends - padded
    dest = pstarts[e_sorted] + jnp.arange(n_assign, dtype=jnp.int32) - starts[e_sorted]
    n_pad = (-(-n_assign // MOE_BLOCK) + N_EXPERTS) * MOE_BLOCK
    tok_buf = jnp.zeros((n_pad,), jnp.int32).at[dest].set(tok_flat[order])
    w_buf = jnp.zeros((n_pad,), F32).at[dest].set(w_flat[order])
    nblk = n_pad // MOE_BLOCK
    blk_start = jnp.arange(nblk, dtype=jnp.int32) * MOE_BLOCK
    blk_e = jnp.minimum(jnp.searchsorted(pends, blk_start, side='right'), N_EXPERTS - 1).astype(jnp.int32)
    xb = hf[tok_buf].reshape(nblk, MOE_BLOCK, D_MODEL)

    def expert_block(args):
        xblk, e = args
        return swiglu(xblk, wg[e], wu[e], wd[e])

    yb = lax.map(expert_block, (xb, blk_e)).reshape(n_pad, D_MODEL)
    out = jnp.zeros_like(hf).at[tok_buf].add(yb * w_buf[:, None].astype(hf.dtype))
    return out.reshape(shp)


def setup_inputs(seed: int = 0) -> dict:
    key = jax.random.key(seed)
    ks = iter(jax.random.split(key, 48))
    nrm = lambda shape, scale: jax.random.normal(next(ks), shape, F32) * scale
    uni = lambda shape, lo, hi: jax.random.uniform(next(ks), shape, F32, lo, hi)
    D = D_MODEL
    return {
        "x": nrm((BATCH, SEQ, D), 1.0),
        "c": nrm((BATCH, D), 1.0),
        "ctx": nrm((BATCH, CTX_LEN, D), 1.0),
        "c_ctx": nrm((D,), 1.0),
        "ada_w": nrm((DEPTH, D, 6 * D), 0.5 * D ** -0.5),
        "ada_b": nrm((DEPTH, 6 * D), 0.02),
        "norm1": 1.0 + nrm((DEPTH, D), 0.05),
        "norm2": 1.0 + nrm((DEPTH, D), 0.05),
        "w_in": nrm((DEPTH, D, N_IN), D ** -0.5),
        "shift_mu": uni((DEPTH, 2, R_COLS), 0.0, 0.5),
        "rwkv_w0": uni((DEPTH, 2, RWKV_W), -6.0, 1.0),
        "rwkv_w2": nrm((DEPTH, 2, DECAY_LORA, RWKV_W), 0.1),
        "rwkv_a0": nrm((DEPTH, 2, RWKV_W), 0.1),
        "rwkv_a2": nrm((DEPTH, 2, AAA_LORA, RWKV_W), AAA_LORA ** -0.5),
        "rwkv_g2": nrm((DEPTH, GATE_LORA, RWKV_W), GATE_LORA ** -0.5),
        "rwkv_kk": 0.85 + nrm((DEPTH, RWKV_W), 0.05),
        "rwkv_ka": 1.0 + nrm((DEPTH, RWKV_W), 0.05),
        "rwkv_rk": nrm((DEPTH, RWKV_HEADS, RWKV_HEAD), 0.1),
        "lnx_w": 1.0 + nrm((DEPTH, RWKV_W), 0.05),
        "lnx_b": nrm((DEPTH, RWKV_W), 0.02),
        "q_norm": 1.0 + nrm((DEPTH, HEAD_DIM), 0.05),
        "k_norm": 1.0 + nrm((DEPTH, HEAD_DIM), 0.05),
        "w_pa": nrm((DEPTH, RWKV_W, D), RWKV_W ** -0.5),
        "w_pb": nrm((DEPTH, ATT_W, D), ATT_W ** -0.5),
        "w_o": nrm((DEPTH, D, D), D ** -0.5),
        "ffn_wg": nrm((N_DENSE, D, FFN_DIM), D ** -0.5),
        "ffn_wu": nrm((N_DENSE, D, FFN_DIM), D ** -0.5),
        "ffn_wd": nrm((N_DENSE, FFN_DIM, D), FFN_DIM ** -0.5),
        "router": nrm((N_MOE, D, N_EXPERTS), D ** -0.5),
        "moe_wg": nrm((N_MOE, N_EXPERTS, D, FFN_DIM), D ** -0.5),
        "moe_wu": nrm((N_MOE, N_EXPERTS, D, FFN_DIM), D ** -0.5),
        "moe_wd": nrm((N_MOE, N_EXPERTS, FFN_DIM, D), FFN_DIM ** -0.5),
        "final_norm": 1.0 + nrm((D,), 0.05),
    }


def reference(x, c, ctx, c_ctx, ada_w, ada_b, norm1, norm2, w_in, shift_mu, rwkv_w0, rwkv_w2, rwkv_a0, rwkv_a2,
              rwkv_g2, rwkv_kk, rwkv_ka, rwkv_rk, lnx_w, lnx_b, q_norm, k_norm, w_pa, w_pb, w_o,
              ffn_wg, ffn_wu, ffn_wd, router, moe_wg, moe_wu, moe_wd, final_norm):
    B, T, _ = x.shape
    cos, sin = rope_2d_tables(T)
    c_act = jax.nn.silu(c)
    cc_act = jax.nn.silu(c_ctx)
    xc = ctx
    zero_state = jnp.zeros((B, RWKV_HEADS, RWKV_HEAD, RWKV_HEAD), F32)
    for l in range(DEPTH):
        last = l == DEPTH - 1
        mod = (c_act @ ada_w[l] + ada_b[l])[:, None, :]
        modc = cc_act @ ada_w[l] + ada_b[l]
        sh1, sc1, gt1, sh2, sc2, gt2 = jnp.split(mod, 6, axis=-1)
        csh1, csc1, cgt1, csh2, csc2, cgt2 = jnp.split(modc, 6, axis=-1)

        h = rmsnorm(x, norm1[l]) * (1 + sc1) + sh1
        hc = rmsnorm(xc, norm1[l]) * (1 + csc1) + csh1
        p = h @ w_in[l]
        pc = hc @ w_in[l]

        rw = (shift_mu[l], rwkv_w0[l], rwkv_w2[l], rwkv_a0[l], rwkv_a2[l], rwkv_g2[l], rwkv_kk[l], rwkv_ka[l])
        lr, lv, lkk, lg, ldec, lkeys, lbs = rwkv_prepare(p[..., :R_COLS], *rw)
        cr, cv, ckk, cg, cdec, ckeys, cbs = rwkv_prepare(pc[..., :R_COLS], *rw)
        y_lat_dirs, y_ctx_dirs = [], []
        for d in range(2):
            rev = d == 1
            s_ctx, yc_d = wkv_scan(zero_state, cr, cdec[d], ckeys[d], cv, ckk, cbs[d], rev)
            _, yl_d = wkv_scan(s_ctx, lr, ldec[d], lkeys[d], lv, lkk, lbs[d], rev)
            y_lat_dirs.append(yl_d)
            y_ctx_dirs.append(yc_d)
        y_rwkv = rwkv_output(y_lat_dirs[0] + y_lat_dirs[1], lr, lkeys, lv, lg, rwkv_rk[l], lnx_w[l], lnx_b[l])

        q, k, v = attn_qkv(p[..., R_COLS:R_COLS + A_COLS], q_norm[l], k_norm[l])
        qc, kc, vc = attn_qkv(pc[..., R_COLS:R_COLS + A_COLS], q_norm[l], k_norm[l])
        q = apply_rope(q, cos, sin)
        k = apply_rope(k, cos, sin)
        y_att = block_attention(q, jnp.concatenate([k, kc], axis=1), jnp.concatenate([v, vc], axis=1))

        x = x + gt1 * merge_branches(p[..., R_COLS + A_COLS:], y_rwkv, y_att, w_pa[l], w_pb[l], w_o[l])
        if not last:
            yc_rwkv = rwkv_output(y_ctx_dirs[0] + y_ctx_dirs[1], cr, ckeys, cv, cg, rwkv_rk[l], lnx_w[l], lnx_b[l])
            yc_att = block_attention(qc, kc, vc)
            xc = xc + cgt1 * merge_branches(pc[..., R_COLS + A_COLS:], yc_rwkv, yc_att, w_pa[l], w_pb[l], w_o[l])

        if l % 2 == 0:
            ffn = functools.partial(swiglu, wg=ffn_wg[l // 2], wu=ffn_wu[l // 2], wd=ffn_wd[l // 2])
        else:
            ffn = functools.partial(moe_swiglu, router=router[l // 2], wg=moe_wg[l // 2], wu=moe_wu[l // 2], wd=moe_wd[l // 2])
        h2 = rmsnorm(x, norm2[l]) * (1 + sc2) + sh2
        x = x + gt2 * ffn(h2)
        if not last:
            hc2 = rmsnorm(xc, norm2[l]) * (1 + csc2) + csh2
            xc = xc + cgt2 * ffn(hc2)
    return rmsnorm(x, final_norm)
```

```python
import functools
import math

import jax
import jax.numpy as jnp
from jax import lax
from jax.experimental import pallas as pl
from jax.experimental.pallas import tpu as pltpu

F32 = jnp.float32
BF16 = jnp.bfloat16

HEAD = 64
RWKV_HEADS = 8
RWKV_W = RWKV_HEADS * HEAD
LORA_W = 128
ATT_HEADS = 8
KV_HEADS = 2
GROUPS = ATT_HEADS // KV_HEADS
ATT_W = ATT_HEADS * HEAD
KV_W = KV_HEADS * HEAD
R_COLS = 3 * RWKV_W + 64 + 64 + 128
GRID_W = 64
ROPE_THETA = 10000.0
N_EXPERTS = 8
NORM_EPS = 1e-6
LNX_EPS = 64e-5
LANES = 128
NEG_BIG = -1e30
VMEM_LIMIT_BYTES = 56 * 1024 * 1024
SCAN_CHUNK = 64


def _params(*sem):
    return pltpu.CompilerParams(dimension_semantics=sem, vmem_limit_bytes=VMEM_LIMIT_BYTES)


def _tile(pref, *dims):
    t = pref
    while any(d % t for d in dims):
        t //= 2
    return t


def _dot(a, b):
    return jnp.dot(a, b, preferred_element_type=F32)


def _split(x):
    hi = x.astype(BF16)
    lo = (x - hi.astype(F32)).astype(BF16)
    return hi, lo


def _dot3(a, b):
    ah, al = _split(a)
    bh, bl = _split(b)
    return _dot(ah, bh) + (_dot(ah, bl) + _dot(al, bh))


def _dot2(a, b_bf16):
    ah, al = _split(a)
    return _dot(ah, b_bf16) + _dot(al, b_bf16)


def _sigmoid(x):
    return 1.0 / (1.0 + jnp.exp(-x))


def _normmod(x, gain, scale, shift):
    ms = jnp.mean(x * x, axis=-1, keepdims=True)
    return x * lax.rsqrt(ms + NORM_EPS) * gain * (1.0 + scale) + shift


def _head_ones(width):
    r = lax.broadcasted_iota(jnp.int32, (width, width), 0) // HEAD
    c = lax.broadcasted_iota(jnp.int32, (width, width), 1) // HEAD
    return jnp.where(r == c, 1.0, 0.0).astype(BF16)


def _ada_kernel(a_ref, w_ref, b_ref, o_ref):
    a = a_ref[...]
    o_ref[0] = _dot3(a * _sigmoid(a), w_ref[0]) + b_ref[0]


def _ada(act, ada_w, ada_b):
    L, D, N6 = ada_w.shape
    R = act.shape[0]
    tn = _tile(1536, N6)
    return pl.pallas_call(
        _ada_kernel,
        grid=(L, N6 // tn),
        in_specs=[pl.BlockSpec((R, D), lambda l, j: (0, 0)),
                  pl.BlockSpec((1, D, tn), lambda l, j: (l, 0, j)),
                  pl.BlockSpec((1, 1, tn), lambda l, j: (l, 0, j))],
        out_specs=pl.BlockSpec((1, R, tn), lambda l, j: (l, 0, j)),
        out_shape=jax.ShapeDtypeStruct((L, R, N6), F32),
        compiler_params=_params("parallel", "parallel"),
        name="ada",
    )(act, ada_w, ada_b.reshape(L, 1, N6))


def _in_kernel(x_ref, mod_ref, g_ref, w_ref, pr_ref, pq_ref, pkv_ref, pg_ref):
    m = mod_ref[0]
    h = _normmod(x_ref[...], g_ref[...], m[1:2], m[0:1]).astype(BF16)
    col = 0
    for ref in (pr_ref, pq_ref, pkv_ref, pg_ref):
        n = ref.shape[-1]
        for c0 in range(0, n, 256):
            ref[:, c0:c0 + 256] = _dot(h, w_ref[:, col + c0:col + c0 + 256])
        col += n


def _mod_index(tm, nc, seq, nb):
    def index(i):
        start = i * tm
        return (jnp.where(start < nc, nb, (start - nc) // seq), 0, 0)
    return index


def _in_proj(xf, mod, gain, w_bf16, nc, seq, nb):
    N, D = xf.shape
    tm = _tile(512, nc, seq)
    widths = (R_COLS, ATT_W, 2 * KV_W, 2 * D)
    return pl.pallas_call(
        _in_kernel,
        grid=(N // tm,),
        in_specs=[pl.BlockSpec((tm, D), lambda i: (i, 0)),
                  pl.BlockSpec((1, 8, D), _mod_index(tm, nc, seq, nb)),
                  pl.BlockSpec((1, D), lambda i: (0, 0)),
                  pl.BlockSpec(w_bf16.shape, lambda i: (0, 0))],
        out_specs=[pl.BlockSpec((tm, w), lambda i: (i, 0)) for w in widths],
        out_shape=[jax.ShapeDtypeStruct((N, w), F32) for w in widths],
        compiler_params=_params("parallel"),
        name="in_proj",
    )(xf, mod, gain.reshape(1, D), w_bf16)


def _prep_kernel(p_ref, prev_ref, next_ref, mu_ref, w0_ref, w2_ref, a0_ref, a2_ref, g2_ref, kk_ref, ka_ref,
                 rk_ref, r_ref, v_ref, nkk_ref, g_ref, bonus_ref, lw_ref, keys_ref, b_ref, *, tt, nc, ctx, seq):
    i = pl.program_id(0)
    start = i * tt
    in_ctx = start < nc
    pos = jnp.where(in_ctx, start % ctx, (start - nc) % seq)
    seg = jnp.where(in_ctx, ctx, seq)
    first = pos == 0
    last = pos + tt == seg
    p = p_ref[...]
    rows = lax.broadcasted_iota(jnp.int32, (tt, 1), 0)
    prev_row = jnp.where(first, 0.0, prev_ref[7:8, :])
    next_row = jnp.where(last, 0.0, next_ref[0:1, :])
    prev = jnp.where(rows == 0, prev_row, pltpu.roll(p, 1, 0))
    nxt = jnp.where(rows == tt - 1, next_row, pltpu.roll(p, tt - 1, 0))
    ps = p + mu_ref[0:1, :] * (prev - p) + mu_ref[1:2, :] * (nxt - p)
    W = RWKV_W
    r = ps[:, 0:W]
    k = ps[:, W:2 * W]
    v = ps[:, 2 * W:3 * W]
    wa = ps[:, 3 * W:3 * W + LORA_W]
    gd = ps[:, 3 * W + LORA_W:]
    ones = _head_ones(W)
    kkf = k * kk_ref[...]
    nrm = jnp.sqrt(_dot2(kkf * kkf, ones))
    kk = kkf / jnp.maximum(nrm, 1e-12)
    tw = jnp.tanh(wa)
    ksum = jnp.zeros_like(k)
    for d in range(2):
        z = w0_ref[d:d + 1, :] + _dot3(tw, w2_ref[d])
        lw_ref[d] = -math.exp(-0.5) * _sigmoid(z)
        a = _sigmoid(a0_ref[d:d + 1, :] + _dot3(wa, a2_ref[d]))
        keys = k * (1.0 + (a - 1.0) * ka_ref[...])
        keys_ref[d] = keys
        b_ref[d] = kk * a
        ksum = ksum + keys
    r_ref[...] = r
    v_ref[...] = v
    nkk_ref[...] = -kk
    g_ref[...] = _dot3(_sigmoid(gd), g2_ref[...])
    bonus_ref[...] = _dot2(r * (0.5 * ksum) * rk_ref[...], ones) * v


def _rwkv_prep(p_r, mu, w0, w2, a0, a2, g2, k_k, k_a, r_k, nc, ctx, seq):
    N = p_r.shape[0]
    W = RWKV_W
    tt = _tile(256, ctx, seq)
    nblk8 = N // 8
    zeros = jnp.zeros((2, 64, W), F32)
    w2p = jnp.concatenate([w2, zeros], axis=1)
    a2p = jnp.concatenate([zeros, a2], axis=1)
    row = lambda t: t.reshape(1, W)
    full = lambda a: pl.BlockSpec(a.shape, lambda i: (0,) * a.ndim)
    consts = (mu, w0, w2p, a0, a2p, g2, row(k_k), row(k_a), row(r_k))
    one = jax.ShapeDtypeStruct((N, W), F32)
    two = jax.ShapeDtypeStruct((2, N, W), F32)
    s1 = pl.BlockSpec((tt, W), lambda i: (i, 0))
    s2 = pl.BlockSpec((2, tt, W), lambda i: (0, i, 0))
    return pl.pallas_call(
        functools.partial(_prep_kernel, tt=tt, nc=nc, ctx=ctx, seq=seq),
        grid=(N // tt,),
        in_specs=[pl.BlockSpec((tt, R_COLS), lambda i: (i, 0)),
                  pl.BlockSpec((8, R_COLS), lambda i: (jnp.maximum(i * (tt // 8) - 1, 0), 0)),
                  pl.BlockSpec((8, R_COLS), lambda i: (jnp.minimum((i + 1) * (tt // 8), nblk8 - 1), 0))]
                 + [full(a) for a in consts],
        out_specs=[s1, s1, s1, s1, s1, s2, s2, s2],
        out_shape=[one, one, one, one, one, two, two, two],
        compiler_params=_params("parallel"),
        name="rwkv_prep",
    )(p_r, p_r, p_r, *consts)


def _scan_kernel(r_ref, v_ref, a_ref, lw_ref, k_ref, b_ref, y_ref, st_ref, *, C):
    d = pl.program_id(1)
    j = pl.program_id(2)

    @pl.when(j == 0)
    def _():
        st_ref[...] = jnp.zeros_like(st_ref)

    row = lax.broadcasted_iota(jnp.int32, (C, C), 0)
    col = lax.broadcasted_iota(jnp.int32, (C, C), 1)
    before = jnp.where(d == 0, row - col, col - row)
    strict = before > 0
    incl = before >= 0
    eye_c = row == col
    rk = lax.broadcasted_iota(jnp.int32, (HEAD, HEAD), 0)
    ck = lax.broadcasted_iota(jnp.int32, (HEAD, HEAD), 1)
    eye_k = rk == ck

    lw = lw_ref[0]
    linc =jnp.where(incl, 1.0, 0.0).astype(BF16)
    lh, ll = _split(lw)
    lam = _dot(linc, lh) + _dot(linc, ll)
    tot = jnp.sum(lw, axis=0, keepdims=True)
    g_exc = jnp.exp(lam - lw)
    g_inc = jnp.exp(lam)
    g_inv = jnp.exp(-lam)
    g_rem = jnp.exp(tot - lam)
    g_tot = jnp.exp(tot)
    r = r_ref[...]
    k = k_ref[0]
    b = b_ref[0]
    at = (a_ref[...] * g_exc).astype(BF16)
    rt = r * g_inc
    rtb = rt.astype(BF16)
    bt = (b * g_inv).astype(BF16)
    kt = (k * g_inv).astype(BF16)
    bh = (b * g_rem).astype(BF16)
    kh = (k * g_rem).astype(BF16)
    vb = v_ref[...].astype(BF16)
    nlev = int(math.log2(C)) - 1
    tdims = (((0,), (0,)), ((), ()))
    ys = []
    for h in range(RWKV_HEADS):
        sl = slice(h * HEAD, (h + 1) * HEAD)
        ar = jnp.concatenate([at[:, sl], rtb[:, sl]], axis=0)
        bk = jnp.concatenate([bt[:, sl], kt[:, sl]], axis=0)
        x1 = lax.dot_general(ar, bk, (((1,), (1,)), ((), ())), preferred_element_type=F32)
        m_ab = jnp.where(strict, x1[:C, :C], 0.0)
        m_ak = jnp.where(strict, x1[:C, C:], 0.0)
        m_rb = jnp.where(incl, x1[C:, :C], 0.0)
        m_rk = jnp.where(incl, x1[C:, C:], 0.0)
        t_inv = jnp.where(eye_c, 1.0, 0.0) + m_ab
        mb = m_ab.astype(BF16)
        s_pow = _dot(mb, mb)
        for lev in range(nlev):
            sb = s_pow.astype(BF16)
            if lev < nlev - 1:
                ps = _dot(jnp.concatenate([t_inv, s_pow], axis=0).astype(BF16), sb)
                t_inv = t_inv + ps[:C]
                s_pow = ps[C:]
            else:
                t_inv = t_inv + _dot(t_inv.astype(BF16), sb)
        vh = vb[:, sl]
        mv = _dot(jnp.concatenate([m_ak, m_rk], axis=0).astype(BF16), vh)
        x2 = _dot(t_inv.astype(BF16), jnp.concatenate([at[:, sl], mv[:C].astype(BF16)], axis=1))
        x2b = x2.astype(BF16)
        x3 = _dot(m_rb.astype(BF16), x2b)
        r_hat = rt[:, sl] + x3[:, :HEAD]
        y_in = mv[C:] + x3[:, HEAD:]
        x4 = lax.dot_general(bh[:, sl], x2b, tdims, preferred_element_type=F32)
        x5 = lax.dot_general(kh[:, sl], vh, tdims, preferred_element_type=F32)
        g_mat = jnp.where(eye_k, g_tot[:, sl], 0.0) + x4[:, :HEAD]
        h_mat = x4[:, HEAD:] + x5
        z = _dot3(jnp.concatenate([g_mat, r_hat], axis=0), st_ref[h])
        st_ref[h] = z[:HEAD] + h_mat
        ys.append(z[HEAD:] + y_in)
    y_ref[0] = jnp.concatenate(ys, axis=1)


def _wkv_scan(r, v, nkk, lw, keys, b, nb, ctx, seq):
    N, W = r.shape
    C = SCAN_CHUNK
    ncc, nlc = ctx // C, seq // C

    def blk(bi, d, j):
        jc = jnp.where(d == 0, j, ncc - 1 - j)
        jl = jnp.where(d == 0, j - ncc, nlc - 1 - (j - ncc))
        return jnp.where(j < ncc, bi * ncc + jc, nb * ncc + bi * nlc + jl)

    s1 = pl.BlockSpec((C, W), lambda bi, d, j: (blk(bi, d, j), 0))
    s2 = pl.BlockSpec((1, C, W), lambda bi, d, j: (d, blk(bi, d, j), 0))
    return pl.pallas_call(
        functools.partial(_scan_kernel, C=C),
        grid=(nb, 2, ncc + nlc),
        in_specs=[s1, s1, s1, s2, s2, s2],
        out_specs=s2,
        out_shape=jax.ShapeDtypeStruct((2, N, W), F32),
        scratch_shapes=[pltpu.VMEM((RWKV_HEADS, HEAD, HEAD), F32)],
        compiler_params=_params("parallel", "parallel", "arbitrary"),
        name="wkv_scan",
    )(r, v, nkk, lw, keys, b)


def _rope(t, cos, sin_signed):
    w = t.shape[-1]
    lane = lax.broadcasted_iota(jnp.int32, t.shape, 1)
    swapped = jnp.where(lane % 2 == 0, pltpu.roll(t, w - 1, 1), pltpu.roll(t, 1, 1))
    return t * cos + swapped * sin_signed


def _head_rms(t, gain):
    ms = _dot2(t * t, _head_ones(t.shape[-1])) * (1.0 / HEAD)
    return t * lax.rsqrt(ms + NORM_EPS) * gain


def _kv_kernel(p_ref, kn_ref, cos_ref, sin_ref, k_ref, v_ref, *, tt, nc):
    p = p_ref[...]
    kn = _head_rms(p[:, :KV_W], kn_ref[...])
    kr = _rope(kn, cos_ref[...], sin_ref[...])
    k = jnp.where(pl.program_id(0) * tt < nc, kn, kr).astype(BF16)
    v = p[:, KV_W:].astype(BF16)
    for h in range(KV_HEADS):
        k_ref[h] = k[:, h * HEAD:(h + 1) * HEAD]
        v_ref[h] = v[:, h * HEAD:(h + 1) * HEAD]


def _kv_prep(p_kv, k_norm, cos, sin_signed, nc, ctx, seq):
    N = p_kv.shape[0]
    tt = _tile(512, ctx, seq)
    npos = seq // tt
    pos = lambda i: ((jnp.maximum(i * tt - nc, 0) // tt) % npos, 0)
    out = jax.ShapeDtypeStruct((KV_HEADS, N, HEAD), BF16)
    spec = pl.BlockSpec((KV_HEADS, tt, HEAD), lambda i: (0, i, 0))
    return pl.pallas_call(
        functools.partial(_kv_kernel, tt=tt, nc=nc),
        grid=(N // tt,),
        in_specs=[pl.BlockSpec((tt, 2 * KV_W), lambda i: (i, 0)),
                  pl.BlockSpec((1, KV_W), lambda i: (0, 0)),
                  pl.BlockSpec((tt, KV_W), pos),
                  pl.BlockSpec((tt, KV_W), pos)],
        out_specs=[spec, spec],
        out_shape=[out, out],
        compiler_params=_params("parallel"),
        name="kv_prep",
    )(p_kv, jnp.tile(k_norm, KV_HEADS).reshape(1, KV_W), cos[:, :KV_W], sin_signed[:, :KV_W])


def _attn_kernel(*refs, nseg, rope):
    q_ref, qn_ref = refs[0], refs[1]
    pos = 2
    if rope:
        cos_ref, sin_ref = refs[2], refs[3]
        pos = 4
    kv_refs = refs[pos:pos + 2 * nseg]
    o_ref = refs[pos + 2 * nseg]
    q = _head_rms(q_ref[...], qn_ref[...])
    if rope:
        q = _rope(q, cos_ref[...], sin_ref[...])
    q = (q * HEAD ** -0.5).astype(BF16)
    outs = []
    for g in range(GROUPS):
        qh = q[:, g * HEAD:(g + 1) * HEAD]
        s = [lax.dot_general(qh, kv_refs[2 * i][0], (((1,), (1,)), ((), ())), preferred_element_type=F32)
             for i in range(nseg)]
        m = s[0].max(axis=-1, keepdims=True)
        for si in s[1:]:
            m = jnp.maximum(m, si.max(axis=-1, keepdims=True))
        acc = None
        den = None
        for i in range(nseg):
            e = jnp.exp(s[i] - m)
            pv = _dot(e.astype(BF16), kv_refs[2 * i + 1][0])
            es = e.sum(axis=-1, keepdims=True)
            acc = pv if acc is None else acc + pv
            den = es if den is None else den + es
        outs.append(acc / den)
    o_ref[...] = jnp.concatenate(outs, axis=1)


def _attention(p_q, k, v, q_norm, cos, sin_signed, nb, nc, ctx, seq, latent):
    N = p_q.shape[0]
    qlen = seq if latent else ctx
    tq = _tile(256, qlen)
    nqt = qlen // tq
    qw = GROUPS * HEAD
    qoff = nc // tq if latent else 0
    qmap = lambda bi, g, t: (qoff + bi * nqt + t, g)
    in_specs = [pl.BlockSpec((tq, qw), qmap), pl.BlockSpec((1, qw), lambda bi, g, t: (0, 0))]
    args = [p_q, jnp.tile(q_norm, GROUPS).reshape(1, qw)]
    ctx_spec = pl.BlockSpec((1, ctx, HEAD), lambda bi, g, t: (g, bi, 0))
    if latent:
        in_specs += [pl.BlockSpec((tq, qw), lambda bi, g, t: (t, 0))] * 2
        args += [cos, sin_signed]
        lat_spec = pl.BlockSpec((1, seq, HEAD), lambda bi, g, t: (g, nc // seq + bi, 0))
        in_specs += [lat_spec, lat_spec, ctx_spec, ctx_spec]
        args += [k, v, k, v]
    else:
        in_specs += [ctx_spec, ctx_spec]
        args += [k, v]
    return pl.pallas_call(
        functools.partial(_attn_kernel, nseg=2 if latent else 1, rope=latent),
        grid=(nb, KV_HEADS, nqt),
        in_specs=in_specs,
        out_specs=pl.BlockSpec((tq, qw), qmap),
        out_shape=jax.ShapeDtypeStruct((N, ATT_W), F32),
        compiler_params=_params("parallel", "parallel", "parallel"),
        name="attn_lat" if latent else "attn_ctx",
    )(*args)


def _merge_kernel(x_ref, mod_ref, y_ref, bonus_ref, g_ref, ya_ref, pg_ref, lw_ref, lb_ref, wpa_ref, wpb_ref,
                  wo_ref, o_ref):
    D = x_ref.shape[-1]
    ones = _head_ones(RWKV_W)
    y = y_ref[0] + y_ref[1]
    mean = _dot2(y, ones) * (1.0 / HEAD)
    yc = y - mean
    var = _dot2(yc * yc, ones) * (1.0 / HEAD)
    yn = yc * lax.rsqrt(var + LNX_EPS) * lw_ref[...] + lb_ref[...]
    y_rwkv = (yn + bonus_ref[...]) * g_ref[...]
    pg = pg_ref[...]
    mix = (_sigmoid(pg[:, :D]) * _dot(y_rwkv.astype(BF16), wpa_ref[...])
           + _sigmoid(pg[:, D:]) * _dot(ya_ref[...].astype(BF16), wpb_ref[...]))
    o_ref[...] = x_ref[...] + mod_ref[0][2:3] * _dot(mix.astype(BF16), wo_ref[...])


def _merge(xf, mod, y, bonus, g, y_att, p_g, lnx_w, lnx_b, w_pa, w_pb, w_o, nc, seq, nb, row0):
    N, D = xf.shape
    W = RWKV_W
    tm = _tile(512, nc, seq)
    off = row0 // tm
    n_out = N - row0
    mod_idx = _mod_index(tm, nc, seq, nb)
    rows = lambda w: pl.BlockSpec((tm, w), lambda i: (i + off, 0))
    full = lambda a: pl.BlockSpec(a.shape, lambda i: (0,) * a.ndim)
    consts = (lnx_w.reshape(1, W), lnx_b.reshape(1, W), w_pa, w_pb, w_o)
    return pl.pallas_call(
        _merge_kernel,
        grid=(n_out // tm,),
        in_specs=[rows(D), pl.BlockSpec((1, 8, D), lambda i: mod_idx(i + off)),
                  pl.BlockSpec((2, tm, W), lambda i: (0, i + off, 0)),
                  rows(W), rows(W), rows(W), rows(2 * D)] + [full(a) for a in consts],
        out_specs=pl.BlockSpec((tm, D), lambda i: (i, 0)),
        out_shape=jax.ShapeDtypeStruct((n_out, D), F32),
        compiler_params=_params("parallel"),
        name="merge",
    )(xf, mod, y, bonus, g, y_att, p_g, *consts)


def _ffn_kernel(x_ref, mod_ref, g_ref, wg_ref, wu_ref, wd_ref, o_ref, h_ref, acc_ref):
    f = pl.program_id(1)

    @pl.when(f == 0)
    def _():
        m = mod_ref[0]
        h_ref[...] = _normmod(x_ref[...], g_ref[...], m[4:5], m[3:4]).astype(BF16)
        acc_ref[...] = jnp.zeros_like(acc_ref)

    h = h_ref[...]
    a = _dot(h, wg_ref[...])
    z = a * _sigmoid(a) * _dot(h, wu_ref[...])
    acc_ref[...] += _dot(z.astype(BF16), wd_ref[...])

    @pl.when(f == pl.num_programs(1) - 1)
    def _():
        o_ref[...] = x_ref[...] + mod_ref[0][5:6] * acc_ref[...]


def _ffn(xf, mod, gain, wg, wu, wd, nc, seq, nb):
    N, D = xf.shape
    F = wg.shape[-1]
    tm = _tile(1024, nc, seq)
    tf = _tile(512, F)
    return pl.pallas_call(
        _ffn_kernel,
        grid=(N // tm, F // tf),
        in_specs=[pl.BlockSpec((tm, D), lambda i, f: (i, 0)),
                  pl.BlockSpec((1, 8, D), lambda i, f: _mod_index(tm, nc, seq, nb)(i)),
                  pl.BlockSpec((1, D), lambda i, f: (0, 0)),
                  pl.BlockSpec((D, tf), lambda i, f: (0, f)),
                  pl.BlockSpec((D, tf), lambda i, f: (0, f)),
                  pl.BlockSpec((tf, D), lambda i, f: (f, 0))],
        out_specs=pl.BlockSpec((tm, D), lambda i, f: (i, 0)),
        out_shape=jax.ShapeDtypeStruct((N, D), F32),
        scratch_shapes=[pltpu.VMEM((tm, D), BF16), pltpu.VMEM((tm, D), F32)],
        compiler_params=_params("parallel", "arbitrary"),
        name="ffn",
    )(xf, mod, gain.reshape(1, D), wg, wu, wd)


def _router_kernel(x_ref, mod_ref, g_ref, wr_ref, o_ref):
    m = mod_ref[0]
    h = _normmod(x_ref[...], g_ref[...], m[4:5], m[3:4])
    lane = lax.broadcasted_iota(jnp.int32, (h.shape[0], LANES), 1).astype(F32)
    logits = jnp.where(lane < N_EXPERTS, _dot3(h, wr_ref[...]), NEG_BIG)
    m1 = logits.max(axis=-1, keepdims=True)
    i1 = jnp.where(logits == m1, lane, float(LANES)).min(axis=-1, keepdims=True)
    rest = jnp.where(lane == i1, NEG_BIG, logits)
    m2 = rest.max(axis=-1, keepdims=True)
    i2 = jnp.where(rest == m2, lane, float(LANES)).min(axis=-1, keepdims=True)
    e2 = jnp.exp(m2 - m1)
    w1 = 1.0 / (1.0 + e2)
    w2 = e2 / (1.0 + e2)
    o_ref[...] = jnp.where(lane == i1, w1, jnp.where(lane == i2, w2, 0.0))


def _router(xl, mod, gain, router, seq):
    N, D = xl.shape
    tm = _tile(512, seq)
    wr = jnp.zeros((D, LANES), F32).at[:, :N_EXPERTS].set(router)
    return pl.pallas_call(
        _router_kernel,
        grid=(N // tm,),
        in_specs=[pl.BlockSpec((tm, D), lambda i: (i, 0)),
                  pl.BlockSpec((1, 8, D), lambda i: ((i * tm) // seq, 0, 0)),
                  pl.BlockSpec((1, D), lambda i: (0, 0)),
                  pl.BlockSpec((D, LANES), lambda i: (0, 0))],
        out_specs=pl.BlockSpec((tm, LANES), lambda i: (i, 0)),
        out_shape=jax.ShapeDtypeStruct((N, LANES), F32),
        compiler_params=_params("parallel"),
        name="router",
    )(xl, mod, gain.reshape(1, D), wr)


def _moe_kernel(x_ref, mod_ref, g_ref, route_ref, wg_ref, wu_ref, wd_ref, fn_ref, o_ref, h_ref, acc_ref, wt_ref):
    e = pl.program_id(1)
    f = pl.program_id(2)
    last_e = pl.num_programs(1) - 1
    last_f = pl.num_programs(2) - 1

    @pl.when((e == 0) & (f == 0))
    def _():
        m = mod_ref[0]
        h_ref[...] = _normmod(x_ref[...], g_ref[...], m[4:5], m[3:4]).astype(BF16)
        acc_ref[...] = jnp.zeros_like(acc_ref)

    @pl.when(f == 0)
    def _():
        route = route_ref[...]
        lane = lax.broadcasted_iota(jnp.int32, route.shape, 1)
        wt = jnp.where(lane == e, route, 0.0).sum(axis=-1, keepdims=True)
        wt_ref[...] = jnp.broadcast_to(wt, wt_ref.shape)

    h = h_ref[...]
    a = _dot(h, wg_ref[0])
    z = a * _sigmoid(a) * _dot(h, wu_ref[0])
    wt = wt_ref[...]
    wz = jnp.concatenate([wt] * (z.shape[-1] // LANES), axis=1)
    z = jnp.where(wz > 0.0, z * wz, 0.0)
    acc_ref[...] += _dot(z.astype(BF16), wd_ref[0])

    @pl.when((e == last_e) & (f == last_f))
    def _():
        y = x_ref[...] + mod_ref[0][5:6] * acc_ref[...]
        ms = jnp.mean(y * y, axis=-1, keepdims=True)
        o_ref[...] = y * lax.rsqrt(ms + NORM_EPS) * fn_ref[...]


def _moe_final(xl, mod, gain, route, wg, wu, wd, final_norm, seq):
    N, D = xl.shape
    E, _, F = wg.shape
    tm = _tile(1024, seq)
    tf = _tile(512, F)
    return pl.pallas_call(
        _moe_kernel,
        grid=(N // tm, E, F // tf),
        in_specs=[pl.BlockSpec((tm, D), lambda i, e, f: (i, 0)),
                  pl.BlockSpec((1, 8, D), lambda i, e, f: ((i * tm) // seq, 0, 0)),
                  pl.BlockSpec((1, D), lambda i, e, f: (0, 0)),
                  pl.BlockSpec((tm, LANES), lambda i, e, f: (i, 0)),
                  pl.BlockSpec((1, D, tf), lambda i, e, f: (e, 0, f)),
                  pl.BlockSpec((1, D, tf), lambda i, e, f: (e, 0, f)),
                  pl.BlockSpec((1, tf, D), lambda i, e, f: (e, f, 0)),
                  pl.BlockSpec((1, D), lambda i, e, f: (0, 0))],
        out_specs=pl.BlockSpec((tm, D), lambda i, e, f: (i, 0)),
        out_shape=jax.ShapeDtypeStruct((N, D), F32),
        scratch_shapes=[pltpu.VMEM((tm, D), BF16), pltpu.VMEM((tm, D), F32), pltpu.VMEM((tm, LANES), F32)],
        compiler_params=_params("parallel", "arbitrary", "arbitrary"),
        name="moe_final",
    )(xl, mod, gain.reshape(1, D), route, wg, wu, wd, final_norm.reshape(1, D))


def _rope_tables(seq):
    rows = seq // GRID_W
    row, col = jnp.meshgrid(jnp.arange(rows), jnp.arange(GRID_W), indexing='ij')
    axis = HEAD // 2
    inv = ROPE_THETA ** (-jnp.arange(0, axis, 2, dtype=F32) / axis)
    ang = jnp.concatenate([row.reshape(-1, 1).astype(F32) * inv, col.reshape(-1, 1).astype(F32) * inv], axis=-1)
    cos = jnp.repeat(jnp.cos(ang), 2, axis=-1)
    sin = jnp.repeat(jnp.sin(ang), 2, axis=-1) * jnp.tile(jnp.array([-1.0, 1.0], F32), HEAD // 2)
    return jnp.tile(cos, (1, GROUPS)), jnp.tile(sin, (1, GROUPS))


def kernel(x, c, ctx, c_ctx, ada_w, ada_b, norm1, norm2, w_in, shift_mu, rwkv_w0, rwkv_w2, rwkv_a0, rwkv_a2, rwkv_g2, rwkv_kk, rwkv_ka, rwkv_rk, lnx_w, lnx_b, q_norm, k_norm, w_pa, w_pb, w_o, ffn_wg, ffn_wu, ffn_wd, router, moe_wg, moe_wu, moe_wd, final_norm):
    B, T, D = x.shape
    CX = ctx.shape[1]
    depth = ada_w.shape[0]
    assert depth == 2 and ffn_wg.shape[0] == 1 and router.shape[0] == 1
    nc = B * CX
    bf = lambda w: w.astype(BF16)

    nrow = -(-(B + 1) // 8) * 8
    act = jnp.zeros((nrow, D), F32).at[:B].set(c).at[B].set(c_ctx)
    mod = _ada(act, ada_w, ada_b).reshape(depth, nrow, 6, D)
    mod = jnp.concatenate([mod, jnp.zeros((depth, nrow, 2, D), F32)], axis=2)

    cos, sin_signed = _rope_tables(T)
    xf = jnp.concatenate([ctx.reshape(nc, D), x.reshape(B * T, D)], axis=0)
    out = None
    for l in range(depth):
        last = l == depth - 1
        p_r, p_q, p_kv, p_g = _in_proj(xf, mod[l], norm1[l], bf(w_in[l]), nc, T, B)
        r, v, nkk, g, bonus, lw, keys, b = _rwkv_prep(
            p_r, shift_mu[l], rwkv_w0[l], rwkv_w2[l], rwkv_a0[l], rwkv_a2[l], rwkv_g2[l], rwkv_kk[l], rwkv_ka[l],
            rwkv_rk[l].reshape(-1), nc, CX, T)
        y = _wkv_scan(r, v, nkk, lw, keys, b, B, CX, T)
        k_att, v_att = _kv_prep(p_kv, k_norm[l], cos, sin_signed, nc, CX, T)
        y_att = _attention(p_q, k_att, v_att, q_norm[l], cos, sin_signed, B, nc, CX, T, latent=True)
        if not last:
            y_ctx = _attention(p_q, k_att, v_att, q_norm[l], cos, sin_signed, B, nc, CX, T, latent=False)
            y_att = jnp.concatenate([y_ctx[:nc], y_att[nc:]], axis=0)
        xm = _merge(xf, mod[l], y, bonus, g, y_att, p_g, lnx_w[l], lnx_b[l], bf(w_pa[l]), bf(w_pb[l]), bf(w_o[l]),
                    nc, T, B, nc if last else 0)
        if not last:
            xf = _ffn(xm, mod[l], norm2[l], bf(ffn_wg[l // 2]), bf(ffn_wu[l // 2]), bf(ffn_wd[l // 2]), nc, T, B)
        else:
            route = _router(xm, mod[l], norm2[l], router[l // 2], T)
            out = _moe_final(xm, mod[l], norm2[l], route, bf(moe_wg[l // 2]), bf(moe_wu[l // 2]),
                             bf(moe_wd[l // 2]), final_norm, T)
    return out.reshape(B, T, D)
```

```python
import functools
import math

import jax
import jax.numpy as jnp
from jax import lax
from jax.experimental import pallas as pl
from jax.experimental.pallas import tpu as pltpu

F32 = jnp.float32
BF16 = jnp.bfloat16

HEAD = 64
RWKV_HEADS = 8
RWKV_W = RWKV_HEADS * HEAD
LORA_W = 128
ATT_HEADS = 8
KV_HEADS = 2
GROUPS = ATT_HEADS // KV_HEADS
ATT_W = ATT_HEADS * HEAD
KV_W = KV_HEADS * HEAD
R_COLS = 3 * RWKV_W + 64 + 64 + 128
GRID_W = 64
ROPE_THETA = 10000.0
N_EXPERTS = 8
NORM_EPS = 1e-6
LNX_EPS = 64e-5
LANES = 128
NEG_BIG = -1e30
VMEM_LIMIT_BYTES = 56 * 1024 * 1024
SCAN_CHUNK = 64
SCAN_HEAD_GROUP = 8
MOE_BLOCK_ROWS = 1024


def _params(*sem):
    return pltpu.CompilerParams(dimension_semantics=sem, vmem_limit_bytes=VMEM_LIMIT_BYTES)


def _tile(pref, *dims):
    t = pref
    while any(d % t for d in dims):
        t //= 2
    return t


def _dot(a, b):
    return jnp.dot(a, b, preferred_element_type=F32)


def _split(x):
    hi = x.astype(BF16)
    lo = (x - hi.astype(F32)).astype(BF16)
    return hi, lo


def _dot3(a, b):
    ah, al = _split(a)
    bh, bl = _split(b)
    return _dot(ah, bh) + (_dot(ah, bl) + _dot(al, bh))


def _dot2(a, b_bf16):
    ah, al = _split(a)
    return _dot(ah, b_bf16) + _dot(al, b_bf16)


def _sigmoid(x):
    return 1.0 / (1.0 + jnp.exp(-x))


def _normmod(x, gain, scale, shift):
    ms = jnp.mean(x * x, axis=-1, keepdims=True)
    return x * lax.rsqrt(ms + NORM_EPS) * gain * (1.0 + scale) + shift


def _head_ones(width):
    r = lax.broadcasted_iota(jnp.int32, (width, width), 0) // HEAD
    c = lax.broadcasted_iota(jnp.int32, (width, width), 1) // HEAD
    return jnp.where(r == c, 1.0, 0.0).astype(BF16)


def _ada_kernel(a_ref, w_ref, b_ref, o_ref):
    a = a_ref[...]
    o_ref[0] = _dot3(a * _sigmoid(a), w_ref[0]) + b_ref[0]


def _ada(act, ada_w, ada_b):
    L, D, N6 = ada_w.shape
    R = act.shape[0]
    tn = _tile(1536, N6)
    return pl.pallas_call(
        _ada_kernel,
        grid=(L, N6 // tn),
        in_specs=[pl.BlockSpec((R, D), lambda l, j: (0, 0)),
                  pl.BlockSpec((1, D, tn), lambda l, j: (l, 0, j)),
                  pl.BlockSpec((1, 1, tn), lambda l, j: (l, 0, j))],
        out_specs=pl.BlockSpec((1, R, tn), lambda l, j: (l, 0, j)),
        out_shape=jax.ShapeDtypeStruct((L, R, N6), F32),
        compiler_params=_params("parallel", "parallel"),
        name="ada",
    )(act, ada_w, ada_b.reshape(L, 1, N6))


def _in_kernel(x_ref, mod_ref, g_ref, w_ref, pr_ref, pq_ref, pkv_ref, pg_ref):
    m = mod_ref[0]
    h = _normmod(x_ref[...], g_ref[...], m[1:2], m[0:1]).astype(BF16)
    col = 0
    for ref in (pr_ref, pq_ref, pkv_ref, pg_ref):
        n = ref.shape[-1]
        for c0 in range(0, n, 256):
            ref[:, c0:c0 + 256] = _dot(h, w_ref[:, col + c0:col + c0 + 256])
        col += n


def _mod_index(tm, nc, seq, nb):
    def index(i):
        start = i * tm
        return (jnp.where(start < nc, nb, (start - nc) // seq), 0, 0)
    return index


def _in_proj(xf, mod, gain, w_bf16, nc, seq, nb):
    N, D = xf.shape
    tm = _tile(512, nc, seq)
    widths = (R_COLS, ATT_W, 2 * KV_W, 2 * D)
    return pl.pallas_call(
        _in_kernel,
        grid=(N // tm,),
        in_specs=[pl.BlockSpec((tm, D), lambda i: (i, 0)),
                  pl.BlockSpec((1, 8, D), _mod_index(tm, nc, seq, nb)),
                  pl.BlockSpec((1, D), lambda i: (0, 0)),
                  pl.BlockSpec(w_bf16.shape, lambda i: (0, 0))],
        out_specs=[pl.BlockSpec((tm, w), lambda i: (i, 0)) for w in widths],
        out_shape=[jax.ShapeDtypeStruct((N, w), F32) for w in widths],
        compiler_params=_params("parallel"),
        name="in_proj",
    )(xf, mod, gain.reshape(1, D), w_bf16)


def _prep_kernel(p_ref, prev_ref, next_ref, mu_ref, w0_ref, w2_ref, a0_ref, a2_ref, g2_ref, kk_ref, ka_ref,
                 rk_ref, r_ref, v_ref, nkk_ref, g_ref, bonus_ref, lw_ref, keys_ref, b_ref, *, tt, nc, ctx, seq):
    i = pl.program_id(0)
    start = i * tt
    in_ctx = start < nc
    pos = jnp.where(in_ctx, start % ctx, (start - nc) % seq)
    seg = jnp.where(in_ctx, ctx, seq)
    first = pos == 0
    last = pos + tt == seg
    p = p_ref[...]
    rows = lax.broadcasted_iota(jnp.int32, (tt, 1), 0)
    prev_row = jnp.where(first, 0.0, prev_ref[7:8, :])
    next_row = jnp.where(last, 0.0, next_ref[0:1, :])
    prev = jnp.where(rows == 0, prev_row, pltpu.roll(p, 1, 0))
    nxt = jnp.where(rows == tt - 1, next_row, pltpu.roll(p, tt - 1, 0))
    ps = p + mu_ref[0:1, :] * (prev - p) + mu_ref[1:2, :] * (nxt - p)
    W = RWKV_W
    r = ps[:, 0:W]
    k = ps[:, W:2 * W]
    v = ps[:, 2 * W:3 * W]
    wa = ps[:, 3 * W:3 * W + LORA_W]
    gd = ps[:, 3 * W + LORA_W:]
    ones = _head_ones(W)
    kkf = k * kk_ref[...]
    nrm = jnp.sqrt(_dot2(kkf * kkf, ones))
    kk = kkf / jnp.maximum(nrm, 1e-12)
    tw = jnp.tanh(wa)
    ksum = jnp.zeros_like(k)
    for d in range(2):
        z = w0_ref[d:d + 1, :] + _dot3(tw, w2_ref[d])
        lw_ref[d] = -math.exp(-0.5) * _sigmoid(z)
        a = _sigmoid(a0_ref[d:d + 1, :] + _dot3(wa, a2_ref[d]))
        keys = k * (1.0 + (a - 1.0) * ka_ref[...])
        keys_ref[d] = keys
        b_ref[d] = kk * a
        ksum = ksum + keys
    r_ref[...] = r
    v_ref[...] = v
    nkk_ref[...] = -kk
    g_ref[...] = _dot3(_sigmoid(gd), g2_ref[...])
    bonus_ref[...] = _dot2(r * (0.5 * ksum) * rk_ref[...], ones) * v


def _rwkv_prep(p_r, mu, w0, w2, a0, a2, g2, k_k, k_a, r_k, nc, ctx, seq):
    N = p_r.shape[0]
    W = RWKV_W
    tt = _tile(256, ctx, seq)
    nblk8 = N // 8
    zeros = jnp.zeros((2, 64, W), F32)
    w2p = jnp.concatenate([w2, zeros], axis=1)
    a2p = jnp.concatenate([zeros, a2], axis=1)
    row = lambda t: t.reshape(1, W)
    full = lambda a: pl.BlockSpec(a.shape, lambda i: (0,) * a.ndim)
    consts = (mu, w0, w2p, a0, a2p, g2, row(k_k), row(k_a), row(r_k))
    one = jax.ShapeDtypeStruct((N, W), F32)
    two = jax.ShapeDtypeStruct((2, N, W), F32)
    s1 = pl.BlockSpec((tt, W), lambda i: (i, 0))
    s2 = pl.BlockSpec((2, tt, W), lambda i: (0, i, 0))
    return pl.pallas_call(
        functools.partial(_prep_kernel, tt=tt, nc=nc, ctx=ctx, seq=seq),
        grid=(N // tt,),
        in_specs=[pl.BlockSpec((tt, R_COLS), lambda i: (i, 0)),
                  pl.BlockSpec((8, R_COLS), lambda i: (jnp.maximum(i * (tt // 8) - 1, 0), 0)),
                  pl.BlockSpec((8, R_COLS), lambda i: (jnp.minimum((i + 1) * (tt // 8), nblk8 - 1), 0))]
                 + [full(a) for a in consts],
        out_specs=[s1, s1, s1, s1, s1, s2, s2, s2],
        out_shape=[one, one, one, one, one, two, two, two],
        compiler_params=_params("parallel"),
        name="rwkv_prep",
    )(p_r, p_r, p_r, *consts)


def _scan_kernel(r_ref, v_ref, a_ref, lw_ref, k_ref, b_ref, y_ref, st_ref, *, C, direction, group):
    @pl.when(pl.program_id(1) == 0)
    def _():
        st_ref[...] = jnp.zeros_like(st_ref)

    K = HEAD
    H = RWKV_HEADS
    row = lax.broadcasted_iota(jnp.int32, (C, C), 0)
    col = lax.broadcasted_iota(jnp.int32, (C, C), 1)
    eye_cb = jnp.where(row == col, 1.0, 0.0).astype(BF16)
    eye_k = lax.broadcasted_iota(jnp.int32, (K, K), 0) == lax.broadcasted_iota(jnp.int32, (K, K), 1)
    nt = (((1,), (1,)), ((), ()))
    tn = (((0,), (0,)), ((), ()))
    sls = [slice(h * K, (h + 1) * K) for h in range(H)]

    before = row - col if direction == 0 else col - row
    strict = before > 0
    incl = before >= 0
    lw = lw_ref[0]
    linc = jnp.where(incl, 1.0, 0.0).astype(BF16)
    lh, ll = _split(lw)
    lam = _dot(linc, lh) + _dot(linc, ll)
    tot = jnp.sum(lw, axis=0, keepdims=True)
    g_rem = jnp.exp(tot - lam)
    g_inv = jnp.exp(-lam)
    g_tot = jnp.exp(tot)
    k = k_ref[0]
    b = b_ref[0]
    at = a_ref[...] * jnp.exp(lam - lw)
    rt = r_ref[...] * jnp.exp(lam)
    atb = at.astype(BF16)
    rtb = rt.astype(BF16)
    bt = (b * g_inv).astype(BF16)
    kt = (k * g_inv).astype(BF16)
    bh = (b * g_rem).astype(BF16)
    kh = (k * g_rem).astype(BF16)
    vb = v_ref[...].astype(BF16)
    nlev = int(math.log2(C))
    ys = []
    for h0 in range(0, H, group):
        heads = list(range(h0, h0 + group))
        gsl = [sls[h] for h in heads]
        x1 = [lax.dot_general(jnp.concatenate([atb[:, sl], rtb[:, sl]], axis=0),
                              jnp.concatenate([bt[:, sl], kt[:, sl]], axis=0), nt, preferred_element_type=F32)
              for sl in gsl]
        m_rb = [jnp.where(incl, x[C:, :C], 0.0).astype(BF16) for x in x1]
        m_rk = [jnp.where(incl, x[C:, C:], 0.0) for x in x1]
        s_pow = [jnp.where(strict, x[:C, :C], 0.0) for x in x1]
        w = [jnp.concatenate([at[:, sl], jnp.where(strict, x[:C, C:], 0.0)], axis=1) for sl, x in zip(gsl, x1)]
        for lev in range(nlev):
            for c in range(group):
                sb = s_pow[c].astype(BF16)
                if lev < nlev - 1:
                    p = _dot(sb, jnp.concatenate([sb, w[c].astype(BF16)], axis=1))
                    s_pow[c] = p[:, :C]
                    w[c] = w[c] + p[:, C:]
                else:
                    w[c] = w[c] + _dot(sb, w[c].astype(BF16))
        wb = [x.astype(BF16) for x in w]
        xa = [_dot(m, x) for m, x in zip(m_rb, wb)]
        xb = [lax.dot_general(jnp.concatenate([bh[:, sl], kh[:, sl]], axis=1),
                              jnp.concatenate([x, eye_cb], axis=1), tn, preferred_element_type=F32)
              for sl, x in zip(gsl, wb)]
        for c, (h, sl) in enumerate(zip(heads, gsl)):
            r_hat = rt[:, sl] + xa[c][:, :K]
            g_mat = jnp.where(eye_k, g_tot[:, sl], 0.0) + xb[c][:K, :K]
            y_v = m_rk[c] + xa[c][:, K:]
            h_v = xb[c][:K, K:K + C] + xb[c][K:, K + C:]
            yh = _dot(jnp.concatenate([y_v, h_v], axis=0).astype(BF16), vb[:, sl])
            z = _dot3(jnp.concatenate([g_mat, r_hat], axis=0), st_ref[h])
            st_ref[h] = z[:K] + yh[C:]
            ys.append(z[K:] + yh[:C])
    y_ref[...] = jnp.concatenate(ys, axis=1)


def _wkv_scan(r, v, nkk, lw, keys, b, nb, ctx, seq):
    N, W = r.shape
    C = SCAN_CHUNK
    ncc, nlc = ctx // C, seq // C

    def blk(d):
        def index(bi, j):
            jc = j if d == 0 else ncc - 1 - j
            jl = j - ncc if d == 0 else nlc - 1 - (j - ncc)
            return jnp.where(j < ncc, bi * ncc + jc, nb * ncc + bi * nlc + jl)
        return index

    def one(d):
        s1 = pl.BlockSpec((C, W), lambda bi, j: (blk(d)(bi, j), 0))
        s2 = pl.BlockSpec((1, C, W), lambda bi, j: (d, blk(d)(bi, j), 0))
        return pl.pallas_call(
            functools.partial(_scan_kernel, C=C, direction=d, group=SCAN_HEAD_GROUP),
            grid=(nb, ncc + nlc),
            in_specs=[s1, s1, s1, s2, s2, s2],
            out_specs=s1,
            out_shape=jax.ShapeDtypeStruct((N, W), F32),
            scratch_shapes=[pltpu.VMEM((RWKV_HEADS, HEAD, HEAD), F32)],
            compiler_params=_params("parallel", "arbitrary"),
            name="wkv_scan",
        )(r, v, nkk, lw, keys, b)

    return one(0), one(1)


def _rope(t, cos, sin_signed):
    w = t.shape[-1]
    lane = lax.broadcasted_iota(jnp.int32, t.shape, 1)
    swapped = jnp.where(lane % 2 == 0, pltpu.roll(t, w - 1, 1), pltpu.roll(t, 1, 1))
    return t * cos + swapped * sin_signed


def _head_rms(t, gain):
    ms = _dot2(t * t, _head_ones(t.shape[-1])) * (1.0 / HEAD)
    return t * lax.rsqrt(ms + NORM_EPS) * gain


def _kv_kernel(p_ref, kn_ref, cos_ref, sin_ref, k_ref, v_ref, *, tt, nc):
    p = p_ref[...]
    kn = _head_rms(p[:, :KV_W], kn_ref[...])
    kr = _rope(kn, cos_ref[...], sin_ref[...])
    k = jnp.where(pl.program_id(0) * tt < nc, kn, kr).astype(BF16)
    v = p[:, KV_W:].astype(BF16)
    for h in range(KV_HEADS):
        k_ref[h] = k[:, h * HEAD:(h + 1) * HEAD]
        v_ref[h] = v[:, h * HEAD:(h + 1) * HEAD]


def _kv_prep(p_kv, k_norm, cos, sin_signed, nc, ctx, seq):
    N = p_kv.shape[0]
    tt = _tile(512, ctx, seq)
    npos = seq // tt
    pos = lambda i: ((jnp.maximum(i * tt - nc, 0) // tt) % npos, 0)
    out = jax.ShapeDtypeStruct((KV_HEADS, N, HEAD), BF16)
    spec = pl.BlockSpec((KV_HEADS, tt, HEAD), lambda i: (0, i, 0))
    return pl.pallas_call(
        functools.partial(_kv_kernel, tt=tt, nc=nc),
        grid=(N // tt,),
        in_specs=[pl.BlockSpec((tt, 2 * KV_W), lambda i: (i, 0)),
                  pl.BlockSpec((1, KV_W), lambda i: (0, 0)),
                  pl.BlockSpec((tt, KV_W), pos),
                  pl.BlockSpec((tt, KV_W), pos)],
        out_specs=[spec, spec],
        out_shape=[out, out],
        compiler_params=_params("parallel"),
        name="kv_prep",
    )(p_kv, jnp.tile(k_norm, KV_HEADS).reshape(1, KV_W), cos[:, :KV_W], sin_signed[:, :KV_W])


def _attn_kernel(*refs, nseg, rope):
    q_ref, qn_ref = refs[0], refs[1]
    pos = 2
    if rope:
        cos_ref, sin_ref = refs[2], refs[3]
        pos = 4
    kv_refs = refs[pos:pos + 2 * nseg]
    o_ref = refs[pos + 2 * nseg]
    q = _head_rms(q_ref[...], qn_ref[...])
    if rope:
        q = _rope(q, cos_ref[...], sin_ref[...])
    q = (q * HEAD ** -0.5).astype(BF16)
    outs = []
    for g in range(GROUPS):
        qh = q[:, g * HEAD:(g + 1) * HEAD]
        s = [lax.dot_general(qh, kv_refs[2 * i][0], (((1,), (1,)), ((), ())), preferred_element_type=F32)
             for i in range(nseg)]
        m = s[0].max(axis=-1, keepdims=True)
        for si in s[1:]:
            m = jnp.maximum(m, si.max(axis=-1, keepdims=True))
        acc = None
        den = None
        for i in range(nseg):
            e = jnp.exp(s[i] - m)
            pv = _dot(e.astype(BF16), kv_refs[2 * i + 1][0])
            es = e.sum(axis=-1, keepdims=True)
            acc = pv if acc is None else acc + pv
            den = es if den is None else den + es
        outs.append(acc / den)
    o_ref[...] = jnp.concatenate(outs, axis=1)


def _attention(p_q, k, v, q_norm, cos, sin_signed, nb, nc, ctx, seq, latent):
    qlen = seq if latent else ctx
    tq = _tile(256, qlen)
    nqt = qlen // tq
    qw = GROUPS * HEAD
    qoff = nc // tq if latent else 0
    qmap = lambda bi, g, t: (qoff + bi * nqt + t, g)
    in_specs = [pl.BlockSpec((tq, qw), qmap), pl.BlockSpec((1, qw), lambda bi, g, t: (0, 0))]
    args = [p_q, jnp.tile(q_norm, GROUPS).reshape(1, qw)]
    ctx_spec = pl.BlockSpec((1, ctx, HEAD), lambda bi, g, t: (g, bi, 0))
    if latent:
        in_specs += [pl.BlockSpec((tq, qw), lambda bi, g, t: (t, 0))] * 2
        args += [cos, sin_signed]
        lat_spec = pl.BlockSpec((1, seq, HEAD), lambda bi, g, t: (g, nc // seq + bi, 0))
        in_specs += [lat_spec, lat_spec, ctx_spec, ctx_spec]
        args += [k, v, k, v]
    else:
        in_specs += [ctx_spec, ctx_spec]
        args += [k, v]
    return pl.pallas_call(
        functools.partial(_attn_kernel, nseg=2 if latent else 1, rope=latent),
        grid=(nb, KV_HEADS, nqt),
        in_specs=in_specs,
        out_specs=pl.BlockSpec((tq, qw), lambda bi, g, t: (bi * nqt + t, g)),
        out_shape=jax.ShapeDtypeStruct((nb * qlen, ATT_W), F32),
        compiler_params=_params("parallel", "parallel", "parallel"),
        name="attn_lat" if latent else "attn_ctx",
    )(*args)


def _merge_kernel(x_ref, mod_ref, yf_ref, yb_ref, bonus_ref, g_ref, ya_ref, pg_ref, lw_ref, lb_ref, wpa_ref, wpb_ref,
                  wo_ref, o_ref):
    D = x_ref.shape[-1]
    ones = _head_ones(RWKV_W)
    y = yf_ref[...] + yb_ref[...]
    mean = _dot2(y, ones) * (1.0 / HEAD)
    yc = y - mean
    var = _dot2(yc * yc, ones) * (1.0 / HEAD)
    yn = yc * lax.rsqrt(var + LNX_EPS) * lw_ref[...] + lb_ref[...]
    y_rwkv = (yn + bonus_ref[...]) * g_ref[...]
    pg = pg_ref[...]
    mix = (_sigmoid(pg[:, :D]) * _dot(y_rwkv.astype(BF16), wpa_ref[...])
           + _sigmoid(pg[:, D:]) * _dot(ya_ref[...].astype(BF16), wpb_ref[...]))
    o_ref[...] = x_ref[...] + mod_ref[0][2:3] * _dot(mix.astype(BF16), wo_ref[...])


def _merge(xf, mod, y_fwd, y_bwd, bonus, g, y_att, p_g, lnx_w, lnx_b, w_pa, w_pb, w_o, nc, seq, nb, row0):
    N, D = xf.shape
    W = RWKV_W
    tm = _tile(512, nc, seq)
    off = row0 // tm
    n_out = N - row0
    mod_idx = _mod_index(tm, nc, seq, nb)
    rows = lambda w: pl.BlockSpec((tm, w), lambda i: (i + off, 0))
    full = lambda a: pl.BlockSpec(a.shape, lambda i: (0,) * a.ndim)
    consts = (lnx_w.reshape(1, W), lnx_b.reshape(1, W), w_pa, w_pb, w_o)
    return pl.pallas_call(
        _merge_kernel,
        grid=(n_out // tm,),
        in_specs=[rows(D), pl.BlockSpec((1, 8, D), lambda i: mod_idx(i + off)),
                  rows(W), rows(W), rows(W), rows(W), pl.BlockSpec((tm, W), lambda i: (i, 0)), rows(2 * D)]
                 + [full(a) for a in consts],
        out_specs=pl.BlockSpec((tm, D), lambda i: (i, 0)),
        out_shape=jax.ShapeDtypeStruct((n_out, D), F32),
        compiler_params=_params("parallel"),
        name="merge",
    )(xf, mod, y_fwd, y_bwd, bonus, g, y_att, p_g, *consts)


def _ffn_kernel(x_ref, mod_ref, g_ref, wg_ref, wu_ref, wd_ref, o_ref, h_ref, acc_ref):
    f = pl.program_id(1)

    @pl.when(f == 0)
    def _():
        m = mod_ref[0]
        h_ref[...] = _normmod(x_ref[...], g_ref[...], m[4:5], m[3:4]).astype(BF16)
        acc_ref[...] = jnp.zeros_like(acc_ref)

    h = h_ref[...]
    a = _dot(h, wg_ref[...])
    z = a * _sigmoid(a) * _dot(h, wu_ref[...])
    acc_ref[...] += _dot(z.astype(BF16), wd_ref[...])

    @pl.when(f == pl.num_programs(1) - 1)
    def _():
        o_ref[...] = x_ref[...] + mod_ref[0][5:6] * acc_ref[...]


def _ffn(xf, mod, gain, wg, wu, wd, nc, seq, nb):
    N, D = xf.shape
    F = wg.shape[-1]
    tm = _tile(1024, nc, seq)
    tf = _tile(512, F)
    return pl.pallas_call(
        _ffn_kernel,
        grid=(N // tm, F // tf),
        in_specs=[pl.BlockSpec((tm, D), lambda i, f: (i, 0)),
                  pl.BlockSpec((1, 8, D), lambda i, f: _mod_index(tm, nc, seq, nb)(i)),
                  pl.BlockSpec((1, D), lambda i, f: (0, 0)),
                  pl.BlockSpec((D, tf), lambda i, f: (0, f)),
                  pl.BlockSpec((D, tf), lambda i, f: (0, f)),
                  pl.BlockSpec((tf, D), lambda i, f: (f, 0))],
        out_specs=pl.BlockSpec((tm, D), lambda i, f: (i, 0)),
        out_shape=jax.ShapeDtypeStruct((N, D), F32),
        scratch_shapes=[pltpu.VMEM((tm, D), BF16), pltpu.VMEM((tm, D), F32)],
        compiler_params=_params("parallel", "arbitrary"),
        name="ffn",
    )(xf, mod, gain.reshape(1, D), wg, wu, wd)


R_E1, R_E2, R_RANK1, R_RANK2, R_W1, R_W2 = range(6)


def _router_kernel(x_ref, mod_ref, g_ref, wr_ref, route_ref, cnt_ref, carry_ref):
    @pl.when(pl.program_id(0) == 0)
    def _():
        carry_ref[...] = jnp.zeros_like(carry_ref)

    m = mod_ref[0]
    h = _normmod(x_ref[...], g_ref[...], m[4:5], m[3:4])
    tm = h.shape[0]
    lane = lax.broadcasted_iota(jnp.int32, (tm, LANES), 1).astype(F32)
    logits = jnp.where(lane < N_EXPERTS, _dot3(h, wr_ref[...]), NEG_BIG)
    m1 = logits.max(axis=-1, keepdims=True)
    i1 = jnp.where(logits == m1, lane, float(LANES)).min(axis=-1, keepdims=True)
    rest = jnp.where(lane == i1, NEG_BIG, logits)
    m2 = rest.max(axis=-1, keepdims=True)
    i2 = jnp.where(rest == m2, lane, float(LANES)).min(axis=-1, keepdims=True)
    e2 = jnp.exp(m2 - m1)
    w1 = 1.0 / (1.0 + e2)
    w2 = e2 / (1.0 + e2)
    onehot = jnp.where(lane == i1, 1.0, jnp.where(lane == i2, 1.0, 0.0))
    earlier = (lax.broadcasted_iota(jnp.int32, (tm, tm), 0) > lax.broadcasted_iota(jnp.int32, (tm, tm), 1))
    before = _dot(jnp.where(earlier, 1.0, 0.0).astype(BF16), onehot.astype(BF16)) + carry_ref[...]
    rank1 = jnp.where(lane == i1, before, 0.0).sum(axis=-1, keepdims=True)
    rank2 = jnp.where(lane == i2, before, 0.0).sum(axis=-1, keepdims=True)
    carry_ref[...] += onehot.sum(axis=0, keepdims=True)
    cnt_ref[...] = jnp.broadcast_to(carry_ref[...], cnt_ref.shape)
    rec = jnp.zeros_like(lane)
    for idx, val in ((R_E1, i1), (R_E2, i2), (R_RANK1, rank1), (R_RANK2, rank2), (R_W1, w1), (R_W2, w2)):
        rec = jnp.where(lane == float(idx), val, rec)
    route_ref[...] = rec


def _router(xl, mod, gain, router, seq):
    N, D = xl.shape
    tm = _tile(512, seq)
    wr = jnp.zeros((D, LANES), F32).at[:, :N_EXPERTS].set(router)
    return pl.pallas_call(
        _router_kernel,
        grid=(N // tm,),
        in_specs=[pl.BlockSpec((tm, D), lambda i: (i, 0)),
                  pl.BlockSpec((1, 8, D), lambda i: ((i * tm) // seq, 0, 0)),
                  pl.BlockSpec((1, D), lambda i: (0, 0)),
                  pl.BlockSpec((D, LANES), lambda i: (0, 0))],
        out_specs=[pl.BlockSpec((tm, LANES), lambda i: (i, 0)), pl.BlockSpec((8, LANES), lambda i: (0, 0))],
        out_shape=[jax.ShapeDtypeStruct((N, LANES), F32), jax.ShapeDtypeStruct((8, LANES), F32)],
        scratch_shapes=[pltpu.VMEM((1, LANES), F32)],
        compiler_params=_params("arbitrary"),
        name="router",
    )(xl, mod, gain.reshape(1, D), wr)


def _all_rows(n, copy):
    def start(r, c):
        copy(r).start()
        return c

    def wait(r, c):
        copy(r).wait()
        return c

    lax.fori_loop(0, n, start, 0, unroll=8)
    lax.fori_loop(0, n, wait, 0, unroll=8)


def _dispatch_kernel(dest_ref, x_ref, mod_ref, g_ref, xs_in_ref, xs_ref, h_ref, sem):
    del xs_in_ref
    tm = x_ref.shape[0]
    base = pl.program_id(0) * tm
    m = mod_ref[0]
    h_ref[...] = _normmod(x_ref[...], g_ref[...], m[4:5], m[3:4])
    _all_rows(tm, lambda r: pltpu.make_async_copy(
        h_ref.at[pl.ds(r, 1)], xs_ref.at[pl.ds(dest_ref[base + r], 1)], sem))


def _dispatch(dest, xl, mod, gain, xs, seq):
    N, D = xl.shape
    tm = _tile(512, seq)
    return pl.pallas_call(
        _dispatch_kernel,
        grid_spec=pltpu.PrefetchScalarGridSpec(
            num_scalar_prefetch=1,
            grid=(N // tm,),
            in_specs=[pl.BlockSpec((tm, D), lambda i, dest: (i, 0)),
                      pl.BlockSpec((1, 8, D), lambda i, dest: ((i * tm) // seq, 0, 0)),
                      pl.BlockSpec((1, D), lambda i, dest: (0, 0)),
                      pl.BlockSpec(memory_space=pl.ANY)],
            out_specs=pl.BlockSpec(memory_space=pl.ANY),
            scratch_shapes=[pltpu.VMEM((tm, D), F32), pltpu.SemaphoreType.DMA]),
        out_shape=jax.ShapeDtypeStruct(xs.shape, xs.dtype),
        input_output_aliases={4: 0},
        compiler_params=_params("arbitrary"),
        name="moe_dispatch",
    )(dest, xl, mod, gain.reshape(1, D), xs)


def _expert_kernel(tbl_ref, xs_ref, wg_ref, wu_ref, wd_ref, ys_ref, hb_ref, acc_ref, *, nblk):
    i = pl.program_id(0)
    f = pl.program_id(1)
    used = i < tbl_ref[nblk]
    last = f == pl.num_programs(1) - 1

    @pl.when(used & (f == 0))
    def _():
        hb_ref[...] = xs_ref[...].astype(BF16)
        acc_ref[...] = jnp.zeros_like(acc_ref)

    @pl.when(used)
    def _():
        h = hb_ref[...]
        a = _dot(h, wg_ref[0])
        z = a * _sigmoid(a) * _dot(h, wu_ref[0])
        acc_ref[...] += _dot(z.astype(BF16), wd_ref[0])

    @pl.when(used & last)
    def _():
        ys_ref[...] = acc_ref[...]

    @pl.when(jnp.logical_not(used) & last)
    def _():
        ys_ref[...] = jnp.zeros_like(ys_ref)


def _experts(tbl, xs, wg, wu, wd, blk):
    R, D = xs.shape
    F = wg.shape[-1]
    nblk = R // blk
    tf = _tile(512, F)
    return pl.pallas_call(
        functools.partial(_expert_kernel, nblk=nblk),
        grid_spec=pltpu.PrefetchScalarGridSpec(
            num_scalar_prefetch=1,
            grid=(nblk, F // tf),
            in_specs=[pl.BlockSpec((blk, D), lambda i, f, tbl: (i, 0)),
                      pl.BlockSpec((1, D, tf), lambda i, f, tbl: (tbl[i], 0, f)),
                      pl.BlockSpec((1, D, tf), lambda i, f, tbl: (tbl[i], 0, f)),
                      pl.BlockSpec((1, tf, D), lambda i, f, tbl: (tbl[i], f, 0))],
            out_specs=pl.BlockSpec((blk, D), lambda i, f, tbl: (i, 0)),
            scratch_shapes=[pltpu.VMEM((blk, D), BF16), pltpu.VMEM((blk, D), F32)]),
        out_shape=jax.ShapeDtypeStruct((R, D), F32),
        compiler_params=_params("parallel", "arbitrary"),
        name="moe_experts",
    )(tbl, xs, wg, wu, wd)


def _combine_kernel(dest_ref, acc_ref, route_ref, mod_ref, fn_ref, ys_ref, o_ref, buf_ref, sem, *, wlane, final):
    tm = acc_ref.shape[0]
    base = pl.program_id(0) * tm
    _all_rows(tm, lambda r: pltpu.make_async_copy(
        ys_ref.at[pl.ds(dest_ref[base + r], 1)], buf_ref.at[pl.ds(r, 1)], sem))
    y = acc_ref[...] + mod_ref[0][5:6] * (route_ref[:, wlane:wlane + 1] * buf_ref[...])
    if final:
        ms = jnp.mean(y * y, axis=-1, keepdims=True)
        y = y * lax.rsqrt(ms + NORM_EPS) * fn_ref[...]
    o_ref[...] = y


def _combine(dest, acc, route, mod, final_norm, ys, seq, wlane, final):
    N, D = acc.shape
    tm = _tile(512, seq)
    return pl.pallas_call(
        functools.partial(_combine_kernel, wlane=wlane, final=final),
        grid_spec=pltpu.PrefetchScalarGridSpec(
            num_scalar_prefetch=1,
            grid=(N // tm,),
            in_specs=[pl.BlockSpec((tm, D), lambda i, dest: (i, 0)),
                      pl.BlockSpec((tm, LANES), lambda i, dest: (i, 0)),
                      pl.BlockSpec((1, 8, D), lambda i, dest: ((i * tm) // seq, 0, 0)),
                      pl.BlockSpec((1, D), lambda i, dest: (0, 0)),
                      pl.BlockSpec(memory_space=pl.ANY)],
            out_specs=pl.BlockSpec((tm, D), lambda i, dest: (i, 0)),
            scratch_shapes=[pltpu.VMEM((tm, D), F32), pltpu.SemaphoreType.DMA]),
        out_shape=jax.ShapeDtypeStruct((N, D), F32),
        compiler_params=_params("arbitrary"),
        name="moe_combine",
    )(dest, acc, route, mod, final_norm.reshape(1, D), ys)


def _moe_final(xl, mod, gain, router, wg, wu, wd, final_norm, seq):
    N, D = xl.shape
    blk = MOE_BLOCK_ROWS
    route, counts = _router(xl, mod, gain, router, seq)
    counts = counts[0, :N_EXPERTS].astype(jnp.int32)
    padded = (counts + blk - 1) // blk * blk
    ends = jnp.cumsum(padded)
    starts = ends - padded
    nblk = -(-(2 * N) // blk) + N_EXPERTS
    blk_expert = jnp.minimum(jnp.searchsorted(ends, jnp.arange(nblk, dtype=jnp.int32) * blk, side='right'),
                             N_EXPERTS - 1).astype(jnp.int32)
    tbl = jnp.concatenate([blk_expert, (ends[-1:] // blk).astype(jnp.int32)])
    dests = [starts[route[:, e].astype(jnp.int32)] + route[:, r].astype(jnp.int32)
             for e, r in ((R_E1, R_RANK1), (R_E2, R_RANK2))]
    xs = jnp.zeros((nblk * blk, D), F32)
    for dest in dests:
        xs = _dispatch(dest, xl, mod, gain, xs, seq)
    ys = _experts(tbl, xs, wg, wu, wd, blk)
    acc = _combine(dests[0], xl, route, mod, final_norm, ys, seq, R_W1, False)
    return _combine(dests[1], acc, route, mod, final_norm, ys, seq, R_W2, True)


def _rope_tables(seq):
    rows = seq // GRID_W
    row, col = jnp.meshgrid(jnp.arange(rows), jnp.arange(GRID_W), indexing='ij')
    axis = HEAD // 2
    inv = ROPE_THETA ** (-jnp.arange(0, axis, 2, dtype=F32) / axis)
    ang = jnp.concatenate([row.reshape(-1, 1).astype(F32) * inv, col.reshape(-1, 1).astype(F32) * inv], axis=-1)
    cos = jnp.repeat(jnp.cos(ang), 2, axis=-1)
    sin = jnp.repeat(jnp.sin(ang), 2, axis=-1) * jnp.tile(jnp.array([-1.0, 1.0], F32), HEAD // 2)
    return jnp.tile(cos, (1, GROUPS)), jnp.tile(sin, (1, GROUPS))


def kernel(x, c, ctx, c_ctx, ada_w, ada_b, norm1, norm2, w_in, shift_mu, rwkv_w0, rwkv_w2, rwkv_a0, rwkv_a2, rwkv_g2, rwkv_kk, rwkv_ka, rwkv_rk, lnx_w, lnx_b, q_norm, k_norm, w_pa, w_pb, w_o, ffn_wg, ffn_wu, ffn_wd, router, moe_wg, moe_wu, moe_wd, final_norm):
    B, T, D = x.shape
    CX = ctx.shape[1]
    depth = ada_w.shape[0]
    assert depth == 2 and ffn_wg.shape[0] == 1 and router.shape[0] == 1
    nc = B * CX
    bf = lambda w: w.astype(BF16)

    nrow = -(-(B + 1) // 8) * 8
    act = jnp.zeros((nrow, D), F32).at[:B].set(c).at[B].set(c_ctx)
    mod = _ada(act, ada_w, ada_b).reshape(depth, nrow, 6, D)
    mod = jnp.concatenate([mod, jnp.zeros((depth, nrow, 2, D), F32)], axis=2)

    cos, sin_signed = _rope_tables(T)
    xf = jnp.concatenate([ctx.reshape(nc, D), x.reshape(B * T, D)], axis=0)
    out = None
    for l in range(depth):
        last = l == depth - 1
        p_r, p_q, p_kv, p_g = _in_proj(xf, mod[l], norm1[l], bf(w_in[l]), nc, T, B)
        r, v, nkk, g, bonus, lw, keys, b = _rwkv_prep(
            p_r, shift_mu[l], rwkv_w0[l], rwkv_w2[l], rwkv_a0[l], rwkv_a2[l], rwkv_g2[l], rwkv_kk[l], rwkv_ka[l],
            rwkv_rk[l].reshape(-1), nc, CX, T)
        y_fwd, y_bwd = _wkv_scan(r, v, nkk, lw, keys, b, B, CX, T)
        k_att, v_att = _kv_prep(p_kv, k_norm[l], cos, sin_signed, nc, CX, T)
        y_att = _attention(p_q, k_att, v_att, q_norm[l], cos, sin_signed, B, nc, CX, T, latent=True)
        if not last:
            y_ctx = _attention(p_q, k_att, v_att, q_norm[l], cos, sin_signed, B, nc, CX, T, latent=False)
            y_att = jnp.concatenate([y_ctx, y_att], axis=0)
        xm = _merge(xf, mod[l], y_fwd, y_bwd, bonus, g, y_att, p_g, lnx_w[l], lnx_b[l], bf(w_pa[l]), bf(w_pb[l]), bf(w_o[l]),
                    nc, T, B, nc if last else 0)
        if not last:
            xf = _ffn(xm, mod[l], norm2[l], bf(ffn_wg[l // 2]), bf(ffn_wu[l // 2]), bf(ffn_wd[l // 2]), nc, T, B)
        else:
            out = _moe_final(xm, mod[l], norm2[l], router[l // 2], bf(moe_wg[l // 2]), bf(moe_wu[l // 2]),
                             bf(moe_wd[l // 2]), final_norm, T)
    return out.reshape(B, T, D)
```

```python
import functools
import math

import jax
import jax.numpy as jnp
from jax import lax
from jax.experimental import pallas as pl
from jax.experimental.pallas import tpu as pltpu

F32 = jnp.float32
BF16 = jnp.bfloat16

HEAD = 64
RWKV_HEADS = 8
RWKV_W = RWKV_HEADS * HEAD
LORA_W = 128
ATT_HEADS = 8
KV_HEADS = 2
GROUPS = ATT_HEADS // KV_HEADS
ATT_W = ATT_HEADS * HEAD
KV_W = KV_HEADS * HEAD
R_COLS = 3 * RWKV_W + 64 + 64 + 128
GRID_W = 64
ROPE_THETA = 10000.0
N_EXPERTS = 8
NORM_EPS = 1e-6
LNX_EPS = 64e-5
LANES = 128
NEG_BIG = -1e30
VMEM_LIMIT_BYTES = 56 * 1024 * 1024
SCAN_CHUNK = 64
SCAN_BATCHES_PER_STEP = 4
MOE_BLOCK_ROWS = 1024


def _params(*sem):
    return pltpu.CompilerParams(dimension_semantics=sem, vmem_limit_bytes=VMEM_LIMIT_BYTES)


def _tile(pref, *dims):
    t = pref
    while any(d % t for d in dims):
        t //= 2
    return t


def _dot(a, b):
    return jnp.dot(a, b, preferred_element_type=F32)


def _split(x):
    hi = x.astype(BF16)
    lo = (x - hi.astype(F32)).astype(BF16)
    return hi, lo


def _dot3(a, b):
    ah, al = _split(a)
    bh, bl = _split(b)
    return _dot(ah, bh) + (_dot(ah, bl) + _dot(al, bh))


def _dot2(a, b_bf16):
    ah, al = _split(a)
    return _dot(ah, b_bf16) + _dot(al, b_bf16)


def _sigmoid(x):
    return 1.0 / (1.0 + jnp.exp(-x))


def _normmod(x, gain, scale, shift):
    ms = jnp.mean(x * x, axis=-1, keepdims=True)
    return x * lax.rsqrt(ms + NORM_EPS) * gain * (1.0 + scale) + shift


def _head_ones(width):
    r = lax.broadcasted_iota(jnp.int32, (width, width), 0) // HEAD
    c = lax.broadcasted_iota(jnp.int32, (width, width), 1) // HEAD
    return jnp.where(r == c, 1.0, 0.0).astype(BF16)


def _ada_kernel(a_ref, w_ref, b_ref, o_ref):
    a = a_ref[...]
    o_ref[0] = _dot3(a * _sigmoid(a), w_ref[0]) + b_ref[0]


def _ada(act, ada_w, ada_b):
    L, D, N6 = ada_w.shape
    R = act.shape[0]
    tn = _tile(1536, N6)
    return pl.pallas_call(
        _ada_kernel,
        grid=(L, N6 // tn),
        in_specs=[pl.BlockSpec((R, D), lambda l, j: (0, 0)),
                  pl.BlockSpec((1, D, tn), lambda l, j: (l, 0, j)),
                  pl.BlockSpec((1, 1, tn), lambda l, j: (l, 0, j))],
        out_specs=pl.BlockSpec((1, R, tn), lambda l, j: (l, 0, j)),
        out_shape=jax.ShapeDtypeStruct((L, R, N6), F32),
        compiler_params=_params("parallel", "parallel"),
        name="ada",
    )(act, ada_w, ada_b.reshape(L, 1, N6))


def _in_kernel(x_ref, mod_ref, g_ref, w_ref, pr_ref, pq_ref, pkv_ref, pg_ref):
    m = mod_ref[0]
    h = _normmod(x_ref[...], g_ref[...], m[1:2], m[0:1]).astype(BF16)
    col = 0
    for ref in (pr_ref, pq_ref, pkv_ref, pg_ref):
        n = ref.shape[-1]
        for c0 in range(0, n, 256):
            ref[:, c0:c0 + 256] = _dot(h, w_ref[:, col + c0:col + c0 + 256])
        col += n


def _mod_index(tm, nc, seq, nb):
    def index(i):
        start = i * tm
        return (jnp.where(start < nc, nb, (start - nc) // seq), 0, 0)
    return index


def _in_proj(xf, mod, gain, w_bf16, nc, seq, nb):
    N, D = xf.shape
    tm = _tile(512, nc, seq)
    widths = (R_COLS, ATT_W, 2 * KV_W, 2 * D)
    return pl.pallas_call(
        _in_kernel,
        grid=(N // tm,),
        in_specs=[pl.BlockSpec((tm, D), lambda i: (i, 0)),
                  pl.BlockSpec((1, 8, D), _mod_index(tm, nc, seq, nb)),
                  pl.BlockSpec((1, D), lambda i: (0, 0)),
                  pl.BlockSpec(w_bf16.shape, lambda i: (0, 0))],
        out_specs=[pl.BlockSpec((tm, w), lambda i: (i, 0)) for w in widths],
        out_shape=[jax.ShapeDtypeStruct((N, w), F32) for w in widths],
        compiler_params=_params("parallel"),
        name="in_proj",
    )(xf, mod, gain.reshape(1, D), w_bf16)


def _prep_kernel(p_ref, prev_ref, next_ref, mu_ref, w0_ref, w2_ref, a0_ref, a2_ref, g2_ref, kk_ref, ka_ref,
                 rk_ref, r_ref, v_ref, nkk_ref, g_ref, bonus_ref, lw_ref, keys_ref, b_ref, *, tt, nc, ctx, seq):
    i = pl.program_id(0)
    start = i * tt
    in_ctx = start < nc
    pos = jnp.where(in_ctx, start % ctx, (start - nc) % seq)
    seg = jnp.where(in_ctx, ctx, seq)
    first = pos == 0
    last = pos + tt == seg
    p = p_ref[...]
    rows = lax.broadcasted_iota(jnp.int32, (tt, 1), 0)
    prev_row = jnp.where(first, 0.0, prev_ref[7:8, :])
    next_row = jnp.where(last, 0.0, next_ref[0:1, :])
    prev = jnp.where(rows == 0, prev_row, pltpu.roll(p, 1, 0))
    nxt = jnp.where(rows == tt - 1, next_row, pltpu.roll(p, tt - 1, 0))
    ps = p + mu_ref[0:1, :] * (prev - p) + mu_ref[1:2, :] * (nxt - p)
    W = RWKV_W
    r = ps[:, 0:W]
    k = ps[:, W:2 * W]
    v = ps[:, 2 * W:3 * W]
    wa = ps[:, 3 * W:3 * W + LORA_W]
    gd = ps[:, 3 * W + LORA_W:]
    ones = _head_ones(W)
    kkf = k * kk_ref[...]
    nrm = jnp.sqrt(_dot2(kkf * kkf, ones))
    kk = kkf / jnp.maximum(nrm, 1e-12)
    tw = jnp.tanh(wa)
    ksum = jnp.zeros_like(k)
    for d in range(2):
        z = w0_ref[d:d + 1, :] + _dot3(tw, w2_ref[d])
        lw_ref[d] = -math.exp(-0.5) * _sigmoid(z)
        a = _sigmoid(a0_ref[d:d + 1, :] + _dot3(wa, a2_ref[d]))
        keys = k * (1.0 + (a - 1.0) * ka_ref[...])
        keys_ref[d] = keys
        b_ref[d] = kk * a
        ksum = ksum + keys
    r_ref[...] = r
    v_ref[...] = v
    nkk_ref[...] = -kk
    g_ref[...] = _dot3(_sigmoid(gd), g2_ref[...])
    bonus_ref[...] = _dot2(r * (0.5 * ksum) * rk_ref[...], ones) * v


def _rwkv_prep(p_r, mu, w0, w2, a0, a2, g2, k_k, k_a, r_k, nc, ctx, seq):
    N = p_r.shape[0]
    W = RWKV_W
    tt = _tile(256, ctx, seq)
    nblk8 = N // 8
    zeros = jnp.zeros((2, 64, W), F32)
    w2p = jnp.concatenate([w2, zeros], axis=1)
    a2p = jnp.concatenate([zeros, a2], axis=1)
    row = lambda t: t.reshape(1, W)
    full = lambda a: pl.BlockSpec(a.shape, lambda i: (0,) * a.ndim)
    consts = (mu, w0, w2p, a0, a2p, g2, row(k_k), row(k_a), row(r_k))
    one = jax.ShapeDtypeStruct((N, W), F32)
    two = jax.ShapeDtypeStruct((2, N, W), F32)
    s1 = pl.BlockSpec((tt, W), lambda i: (i, 0))
    s2 = pl.BlockSpec((2, tt, W), lambda i: (0, i, 0))
    return pl.pallas_call(
        functools.partial(_prep_kernel, tt=tt, nc=nc, ctx=ctx, seq=seq),
        grid=(N // tt,),
        in_specs=[pl.BlockSpec((tt, R_COLS), lambda i: (i, 0)),
                  pl.BlockSpec((8, R_COLS), lambda i: (jnp.maximum(i * (tt // 8) - 1, 0), 0)),
                  pl.BlockSpec((8, R_COLS), lambda i: (jnp.minimum((i + 1) * (tt // 8), nblk8 - 1), 0))]
                 + [full(a) for a in consts],
        out_specs=[s1, s1, s1, s1, s1, s2, s2, s2],
        out_shape=[one, one, one, one, one, two, two, two],
        compiler_params=_params("parallel"),
        name="rwkv_prep",
    )(p_r, p_r, p_r, *consts)


def _pair_blocks(x):
    left = lax.broadcasted_iota(jnp.int32, x.shape, 1) < HEAD
    zero = jnp.zeros_like(x)
    return jnp.concatenate([jnp.where(left, x, zero), jnp.where(left, zero, x)], axis=0)


def _pair_diag(x):
    left = lax.broadcasted_iota(jnp.int32, (HEAD, LANES), 1) < HEAD
    return jnp.where(left, x[:HEAD], x[HEAD:])


def _scan_pair_kernel(*refs, C, direction, nslot):
    assert C == HEAD
    y_ref, st_ref = refs[6 * nslot], refs[6 * nslot + 1]

    @pl.when(pl.program_id(1) == 0)
    def _():
        st_ref[...] = jnp.zeros_like(st_ref)

    npair = RWKV_HEADS // 2
    row = lax.broadcasted_iota(jnp.int32, (C, C), 0)
    col = lax.broadcasted_iota(jnp.int32, (C, C), 1)
    before = row - col if direction == 0 else col - row
    linc = jnp.where(before >= 0, 1.0, 0.0).astype(BF16)
    row2 = lax.broadcasted_iota(jnp.int32, (C, LANES), 0)
    col2 = lax.broadcasted_iota(jnp.int32, (C, LANES), 1) % C
    before2 = row2 - col2 if direction == 0 else col2 - row2
    strict = before2 > 0
    incl = before2 >= 0
    eye2 = row2 == col2
    eye2b = jnp.where(eye2, 1.0, 0.0).astype(BF16)
    nt = (((1,), (1,)), ((), ()))
    tn = (((0,), (0,)), ((), ()))
    nlev = int(math.log2(C))

    at, rt, bt, kt, bh, kh, vb, g_tot = ([] for _ in range(8))
    for s in range(nslot):
        r_ref, v_ref, a_ref, lw_ref, k_ref, b_ref = refs[6 * s:6 * s + 6]
        lw = lw_ref[0]
        lh, ll = _split(lw)
        lam = _dot(linc, lh) + _dot(linc, ll)
        tot = jnp.sum(lw, axis=0, keepdims=True)
        g_rem = jnp.exp(tot - lam)
        g_inv = jnp.exp(-lam)
        gt = jnp.exp(tot)
        k = k_ref[0]
        b = b_ref[0]
        a_s = a_ref[...] * jnp.exp(lam - lw)
        r_s = r_ref[...] * jnp.exp(lam)
        b_t = (b * g_inv).astype(BF16)
        k_t = (k * g_inv).astype(BF16)
        b_h = (b * g_rem).astype(BF16)
        k_h = (k * g_rem).astype(BF16)
        v_b = v_ref[...].astype(BF16)
        for p in range(npair):
            sl = slice(p * LANES, (p + 1) * LANES)
            for dst, src in ((at, a_s), (rt, r_s), (bt, b_t), (kt, k_t), (bh, b_h), (kh, k_h), (vb, v_b), (g_tot, gt)):
                dst.append(src[:, sl])
    nch = nslot * npair

    ar = [jnp.concatenate([at[c].astype(BF16), rt[c].astype(BF16)], axis=0) for c in range(nch)]
    xb = [lax.dot_general(ar[c], _pair_blocks(bt[c]), nt, preferred_element_type=F32) for c in range(nch)]
    xk = [lax.dot_general(ar[c], _pair_blocks(kt[c]), nt, preferred_element_type=F32) for c in range(nch)]
    s_pow = [jnp.where(strict, x[:C], 0.0) for x in xb]
    m_rb = [jnp.where(incl, x[C:], 0.0).astype(BF16) for x in xb]
    q = [jnp.where(strict, x[:C], 0.0) for x in xk]
    m_rk = [jnp.where(incl, x[C:], 0.0) for x in xk]
    ah = list(at)
    for lev in range(nlev):
        for c in range(nch):
            sb = s_pow[c].astype(BF16)
            ah[c] = ah[c] + _dot(sb, _pair_blocks(ah[c].astype(BF16)))
            q[c] = q[c] + _dot(sb, _pair_blocks(q[c].astype(BF16)))
            if lev < nlev - 1:
                s_pow[c] = _dot(sb, _pair_blocks(sb))
    ahb = [x.astype(BF16) for x in ah]
    qb = [x.astype(BF16) for x in q]
    ra = [_dot(m_rb[c], _pair_blocks(ahb[c])) for c in range(nch)]
    rq = [_dot(m_rb[c], _pair_blocks(qb[c])) for c in range(nch)]
    ba = [_pair_diag(lax.dot_general(bh[c], ahb[c], tn, preferred_element_type=F32)) for c in range(nch)]
    bq = [_pair_diag(lax.dot_general(bh[c], qb[c], tn, preferred_element_type=F32)) for c in range(nch)]
    k_tr = [_pair_diag(lax.dot_general(kh[c], eye2b, tn, preferred_element_type=F32)) for c in range(nch)]
    for c in range(nch):
        s, p = divmod(c, npair)
        r_hat = rt[c] + ra[c]
        g_mat = jnp.where(eye2, g_tot[c], 0.0) + ba[c]
        v2 = _pair_blocks(vb[c])
        y_in = _dot((m_rk[c] + rq[c]).astype(BF16), v2)
        h_mat = _dot((bq[c] + k_tr[c]).astype(BF16), v2)
        sh, sl_ = _split(st_ref[c])
        sh, sl_ = _pair_blocks(sh), _pair_blocks(sl_)
        gh, gl = _split(g_mat)
        rh, rl = _split(r_hat)
        st_ref[c] = _dot(gh, sh) + (_dot(gh, sl_) + _dot(gl, sh)) + h_mat
        y_ref[s, :, p * LANES:(p + 1) * LANES] =_dot(rh, sh) + (_dot(rh, sl_) + _dot(rl, sh)) + y_in


def _wkv_scan_pairs(r, v, nkk, lw, keys, b, nb, ctx, seq):
    N, W = r.shape
    C = SCAN_CHUNK
    ncc, nlc = ctx // C, seq // C
    nslot = math.gcd(SCAN_BATCHES_PER_STEP, nb)
    hb = nb // nslot

    def blk(d, j, bi, nbat):
        jc = j if d == 0 else ncc - 1 - j
        jl = j - ncc if d == 0 else nlc - 1 - (j - ncc)
        return jnp.where(j < ncc, bi * ncc + jc, nbat * ncc + bi * nlc + jl)

    def one(d):
        s1 = lambda s: pl.BlockSpec((C, W), lambda i, j: (blk(d, j, s * hb + i, nb), 0))
        s2 = lambda s: pl.BlockSpec((1, C, W), lambda i, j: (d, blk(d, j, s * hb + i, nb), 0))
        in_specs = [spec for s in range(nslot) for spec in (s1(s), s1(s), s1(s), s2(s), s2(s), s2(s))]
        return pl.pallas_call(
            functools.partial(_scan_pair_kernel, C=C, direction=d, nslot=nslot),
            grid=(hb, ncc + nlc),
            in_specs=in_specs,
            out_specs=pl.BlockSpec((nslot, C, W), lambda i, j: (0, blk(d, j, i, hb), 0)),
            out_shape=jax.ShapeDtypeStruct((nslot, N // nslot, W), F32),
            scratch_shapes=[pltpu.VMEM((nslot * RWKV_HEADS // 2, HEAD, LANES), F32)],
            compiler_params=_params("parallel", "arbitrary"),
            name="wkv_scan",
        )(*([r, v, nkk, lw, keys, b] * nslot))

    return one(0), one(1)


def _slot_rows(tm, nc, nl, nslot):
    def index(i):
        start = i * tm
        in_ctx = start < nc
        rel = jnp.where(in_ctx, start, start - nc)
        per = jnp.where(in_ctx, nc // nslot, nl // nslot)
        local = jnp.where(in_ctx, 0, nc // nslot) + rel % per
        return (rel // per, local // tm, 0)
    return index


def _rope(t, cos, sin_signed):
    w = t.shape[-1]
    lane = lax.broadcasted_iota(jnp.int32, t.shape, 1)
    swapped = jnp.where(lane % 2 == 0, pltpu.roll(t, w - 1, 1), pltpu.roll(t, 1, 1))
    return t * cos + swapped * sin_signed


def _head_rms(t, gain):
    ms = _dot2(t * t, _head_ones(t.shape[-1])) * (1.0 / HEAD)
    return t * lax.rsqrt(ms + NORM_EPS) * gain


def _kv_kernel(p_ref, kn_ref, cos_ref, sin_ref, k_ref, v_ref, *, tt, nc):
    p = p_ref[...]
    kn = _head_rms(p[:, :KV_W], kn_ref[...])
    kr = _rope(kn, cos_ref[...], sin_ref[...])
    k = jnp.where(pl.program_id(0) * tt < nc, kn, kr).astype(BF16)
    v = p[:, KV_W:].astype(BF16)
    for h in range(KV_HEADS):
        k_ref[h] = k[:, h * HEAD:(h + 1) * HEAD]
        v_ref[h] = v[:, h * HEAD:(h + 1) * HEAD]


def _kv_prep(p_kv, k_norm, cos, sin_signed, nc, ctx, seq):
    N = p_kv.shape[0]
    tt = _tile(512, ctx, seq)
    npos = seq // tt
    pos = lambda i: ((jnp.maximum(i * tt - nc, 0) // tt) % npos, 0)
    out = jax.ShapeDtypeStruct((KV_HEADS, N, HEAD), BF16)
    spec = pl.BlockSpec((KV_HEADS, tt, HEAD), lambda i: (0, i, 0))
    return pl.pallas_call(
        functools.partial(_kv_kernel, tt=tt, nc=nc),
        grid=(N // tt,),
        in_specs=[pl.BlockSpec((tt, 2 * KV_W), lambda i: (i, 0)),
                  pl.BlockSpec((1, KV_W), lambda i: (0, 0)),
                  pl.BlockSpec((tt, KV_W), pos),
                  pl.BlockSpec((tt, KV_W), pos)],
        out_specs=[spec, spec],
        out_shape=[out, out],
        compiler_params=_params("parallel"),
        name="kv_prep",
    )(p_kv, jnp.tile(k_norm, KV_HEADS).reshape(1, KV_W), cos[:, :KV_W], sin_signed[:, :KV_W])


def _attn_kernel(*refs, nseg, rope):
    q_ref, qn_ref = refs[0], refs[1]
    pos = 2
    if rope:
        cos_ref, sin_ref = refs[2], refs[3]
        pos = 4
    kv_refs = refs[pos:pos + 2 * nseg]
    o_ref = refs[pos + 2 * nseg]
    q = _head_rms(q_ref[...], qn_ref[...])
    if rope:
        q = _rope(q, cos_ref[...], sin_ref[...])
    q = (q * HEAD ** -0.5).astype(BF16)
    outs = []
    for g in range(GROUPS):
        qh = q[:, g * HEAD:(g + 1) * HEAD]
        s = [lax.dot_general(qh, kv_refs[2 * i][0], (((1,), (1,)), ((), ())), preferred_element_type=F32)
             for i in range(nseg)]
        m = s[0].max(axis=-1, keepdims=True)
        for si in s[1:]:
            m = jnp.maximum(m, si.max(axis=-1, keepdims=True))
        acc = None
        den = None
        for i in range(nseg):
            e = jnp.exp(s[i] - m)
            pv = _dot(e.astype(BF16), kv_refs[2 * i + 1][0])
            es = e.sum(axis=-1, keepdims=True)
            acc = pv if acc is None else acc + pv
            den = es if den is None else den + es
        outs.append(acc / den)
    o_ref[...] = jnp.concatenate(outs, axis=1)


def _attention(p_q, k, v, q_norm, cos, sin_signed, nb, nc, ctx, seq, latent):
    qlen = seq if latent else ctx
    tq = _tile(256, qlen)
    nqt = qlen // tq
    qw = GROUPS * HEAD
    qoff = nc // tq if latent else 0
    qmap = lambda bi, g, t: (qoff + bi * nqt + t, g)
    in_specs = [pl.BlockSpec((tq, qw), qmap), pl.BlockSpec((1, qw), lambda bi, g, t: (0, 0))]
    args = [p_q, jnp.tile(q_norm, GROUPS).reshape(1, qw)]
    ctx_spec = pl.BlockSpec((1, ctx, HEAD), lambda bi, g, t: (g, bi, 0))
    if latent:
        in_specs += [pl.BlockSpec((tq, qw), lambda bi, g, t: (t, 0))] * 2
        args += [cos, sin_signed]
        lat_spec = pl.BlockSpec((1, seq, HEAD), lambda bi, g, t: (g, nc // seq + bi, 0))
        in_specs += [lat_spec, lat_spec, ctx_spec, ctx_spec]
        args += [k, v, k, v]
    else:
        in_specs += [ctx_spec, ctx_spec]
        args += [k, v]
    return pl.pallas_call(
        functools.partial(_attn_kernel, nseg=2 if latent else 1, rope=latent),
        grid=(nb, KV_HEADS, nqt),
        in_specs=in_specs,
        out_specs=pl.BlockSpec((tq, qw), lambda bi, g, t: (bi * nqt + t, g)),
        out_shape=jax.ShapeDtypeStruct((nb * qlen, ATT_W), F32),
        compiler_params=_params("parallel", "parallel", "parallel"),
        name="attn_lat" if latent else "attn_ctx",
    )(*args)


def _merge_kernel(x_ref, mod_ref, yf_ref, yb_ref, bonus_ref, g_ref, ya_ref, pg_ref, lw_ref, lb_ref, wpa_ref, wpb_ref,
                  wo_ref, o_ref):
    D = x_ref.shape[-1]
    ones = _head_ones(RWKV_W)
    y = yf_ref[0] + yb_ref[0]
    mean = _dot2(y, ones) * (1.0 / HEAD)
    yc = y - mean
    var = _dot2(yc * yc, ones) * (1.0 / HEAD)
    yn = yc * lax.rsqrt(var + LNX_EPS) * lw_ref[...] + lb_ref[...]
    y_rwkv = (yn + bonus_ref[...]) * g_ref[...]
    pg = pg_ref[...]
    mix = (_sigmoid(pg[:, :D]) * _dot(y_rwkv.astype(BF16), wpa_ref[...])
           + _sigmoid(pg[:, D:]) * _dot(ya_ref[...].astype(BF16), wpb_ref[...]))
    o_ref[...] = x_ref[...] + mod_ref[0][2:3] * _dot(mix.astype(BF16), wo_ref[...])


def _merge(xf, mod, y_fwd, y_bwd, bonus, g, y_att, p_g, lnx_w, lnx_b, w_pa, w_pb, w_o, nc, seq, nb, row0):
    N, D = xf.shape
    W = RWKV_W
    tm = _tile(512, nc // y_fwd.shape[0], seq)
    off = row0 // tm
    n_out = N - row0
    mod_idx = _mod_index(tm, nc, seq, nb)
    rows = lambda w: pl.BlockSpec((tm, w), lambda i: (i + off, 0))
    slot_index = _slot_rows(tm, nc, N - nc, y_fwd.shape[0])
    scan_rows = pl.BlockSpec((1, tm, W), lambda i: slot_index(i + off))
    full = lambda a: pl.BlockSpec(a.shape, lambda i: (0,) * a.ndim)
    consts = (lnx_w.reshape(1, W), lnx_b.reshape(1, W), w_pa, w_pb, w_o)
    return pl.pallas_call(
        _merge_kernel,
        grid=(n_out // tm,),
        in_specs=[rows(D), pl.BlockSpec((1, 8, D), lambda i: mod_idx(i + off)),
                  scan_rows, scan_rows, rows(W), rows(W), pl.BlockSpec((tm, W), lambda i: (i, 0)), rows(2 * D)]
                 + [full(a) for a in consts],
        out_specs=pl.BlockSpec((tm, D), lambda i: (i, 0)),
        out_shape=jax.ShapeDtypeStruct((n_out, D), F32),
        compiler_params=_params("parallel"),
        name="merge",
    )(xf, mod, y_fwd, y_bwd, bonus, g, y_att, p_g, *consts)


def _ffn_kernel(x_ref, mod_ref, g_ref, wg_ref, wu_ref, wd_ref, o_ref, h_ref, acc_ref):
    f = pl.program_id(1)

    @pl.when(f == 0)
    def _():
        m = mod_ref[0]
        h_ref[...] = _normmod(x_ref[...], g_ref[...], m[4:5], m[3:4]).astype(BF16)
        acc_ref[...] = jnp.zeros_like(acc_ref)

    h = h_ref[...]
    a = _dot(h, wg_ref[...])
    z = a * _sigmoid(a) * _dot(h, wu_ref[...])
    acc_ref[...] += _dot(z.astype(BF16), wd_ref[...])

    @pl.when(f == pl.num_programs(1) - 1)
    def _():
        o_ref[...] = x_ref[...] + mod_ref[0][5:6] * acc_ref[...]


def _ffn(xf, mod, gain, wg, wu, wd, nc, seq, nb):
    N, D = xf.shape
    F = wg.shape[-1]
    tm = _tile(1024, nc, seq)
    tf = _tile(512, F)
    return pl.pallas_call(
        _ffn_kernel,
        grid=(N // tm, F // tf),
        in_specs=[pl.BlockSpec((tm, D), lambda i, f: (i, 0)),
                  pl.BlockSpec((1, 8, D), lambda i, f: _mod_index(tm, nc, seq, nb)(i)),
                  pl.BlockSpec((1, D), lambda i, f: (0, 0)),
                  pl.BlockSpec((D, tf), lambda i, f: (0, f)),
                  pl.BlockSpec((D, tf), lambda i, f: (0, f)),
                  pl.BlockSpec((tf, D), lambda i, f: (f, 0))],
        out_specs=pl.BlockSpec((tm, D), lambda i, f: (i, 0)),
        out_shape=jax.ShapeDtypeStruct((N, D), F32),
        scratch_shapes=[pltpu.VMEM((tm, D), BF16), pltpu.VMEM((tm, D), F32)],
        compiler_params=_params("parallel", "arbitrary"),
        name="ffn",
    )(xf, mod, gain.reshape(1, D), wg, wu, wd)


R_E1, R_E2, R_RANK1, R_RANK2, R_W1, R_W2 = range(6)


def _router_kernel(x_ref, mod_ref, g_ref, wr_ref, route_ref, cnt_ref, carry_ref):
    @pl.when(pl.program_id(0) == 0)
    def _():
        carry_ref[...] = jnp.zeros_like(carry_ref)

    m = mod_ref[0]
    h = _normmod(x_ref[...], g_ref[...], m[4:5], m[3:4])
    tm = h.shape[0]
    lane = lax.broadcasted_iota(jnp.int32, (tm, LANES), 1).astype(F32)
    logits = jnp.where(lane < N_EXPERTS, _dot3(h, wr_ref[...]), NEG_BIG)
    m1 = logits.max(axis=-1, keepdims=True)
    i1 = jnp.where(logits == m1, lane, float(LANES)).min(axis=-1, keepdims=True)
    rest = jnp.where(lane == i1, NEG_BIG, logits)
    m2 = rest.max(axis=-1, keepdims=True)
    i2 = jnp.where(rest == m2, lane, float(LANES)).min(axis=-1, keepdims=True)
    e2 = jnp.exp(m2 - m1)
    w1 = 1.0 / (1.0 + e2)
    w2 = e2 / (1.0 + e2)
    onehot = jnp.where(lane == i1, 1.0, jnp.where(lane == i2, 1.0, 0.0))
    earlier = (lax.broadcasted_iota(jnp.int32, (tm, tm), 0) > lax.broadcasted_iota(jnp.int32, (tm, tm), 1))
    before = _dot(jnp.where(earlier, 1.0, 0.0).astype(BF16), onehot.astype(BF16)) + carry_ref[...]
    rank1 = jnp.where(lane == i1, before, 0.0).sum(axis=-1, keepdims=True)
    rank2 = jnp.where(lane == i2, before, 0.0).sum(axis=-1, keepdims=True)
    carry_ref[...] += onehot.sum(axis=0, keepdims=True)
    cnt_ref[...] = jnp.broadcast_to(carry_ref[...], cnt_ref.shape)
    rec = jnp.zeros_like(lane)
    for idx, val in ((R_E1, i1), (R_E2, i2), (R_RANK1, rank1), (R_RANK2, rank2), (R_W1, w1), (R_W2, w2)):
        rec = jnp.where(lane == float(idx), val, rec)
    route_ref[...] = rec


def _router(xl, mod, gain, router, seq):
    N, D = xl.shape
    tm = _tile(512, seq)
    wr = jnp.zeros((D, LANES), F32).at[:, :N_EXPERTS].set(router)
    return pl.pallas_call(
        _router_kernel,
        grid=(N // tm,),
        in_specs=[pl.BlockSpec((tm, D), lambda i: (i, 0)),
                  pl.BlockSpec((1, 8, D), lambda i: ((i * tm) // seq, 0, 0)),
                  pl.BlockSpec((1, D), lambda i: (0, 0)),
                  pl.BlockSpec((D, LANES), lambda i: (0, 0))],
        out_specs=[pl.BlockSpec((tm, LANES), lambda i: (i, 0)), pl.BlockSpec((8, LANES), lambda i: (0, 0))],
        out_shape=[jax.ShapeDtypeStruct((N, LANES), F32), jax.ShapeDtypeStruct((8, LANES), F32)],
        scratch_shapes=[pltpu.VMEM((1, LANES), F32)],
        compiler_params=_params("arbitrary"),
        name="router",
    )(xl, mod, gain.reshape(1, D), wr)


def _all_rows(n, copy):
    def start(r, c):
        copy(r).start()
        return c

    def wait(r, c):
        copy(r).wait()
        return c

    lax.fori_loop(0, n, start, 0, unroll=8)
    lax.fori_loop(0, n, wait, 0, unroll=8)


def _dispatch_kernel(dest_ref, x_ref, mod_ref, g_ref, xs_in_ref, xs_ref, h_ref, sem):
    del xs_in_ref
    tm = x_ref.shape[0]
    base = pl.program_id(0) * tm
    m = mod_ref[0]
    h_ref[...] = _normmod(x_ref[...], g_ref[...], m[4:5], m[3:4])
    _all_rows(tm, lambda r: pltpu.make_async_copy(
        h_ref.at[pl.ds(r, 1)], xs_ref.at[pl.ds(dest_ref[base + r], 1)], sem))


def _dispatch(dest, xl, mod, gain, xs, seq):
    N, D = xl.shape
    tm = _tile(512, seq)
    return pl.pallas_call(
        _dispatch_kernel,
        grid_spec=pltpu.PrefetchScalarGridSpec(
            num_scalar_prefetch=1,
            grid=(N // tm,),
            in_specs=[pl.BlockSpec((tm, D), lambda i, dest: (i, 0)),
                      pl.BlockSpec((1, 8, D), lambda i, dest: ((i * tm) // seq, 0, 0)),
                      pl.BlockSpec((1, D), lambda i, dest: (0, 0)),
                      pl.BlockSpec(memory_space=pl.ANY)],
            out_specs=pl.BlockSpec(memory_space=pl.ANY),
            scratch_shapes=[pltpu.VMEM((tm, D), F32), pltpu.SemaphoreType.DMA]),
        out_shape=jax.ShapeDtypeStruct(xs.shape, xs.dtype),
        input_output_aliases={4: 0},
        compiler_params=_params("arbitrary"),
        name="moe_dispatch",
    )(dest, xl, mod, gain.reshape(1, D), xs)


def _expert_kernel(tbl_ref, xs_ref, wg_ref, wu_ref, wd_ref, ys_ref, hb_ref, acc_ref, *, nblk):
    i = pl.program_id(0)
    f = pl.program_id(1)
    used = i < tbl_ref[nblk]
    last = f == pl.num_programs(1) - 1

    @pl.when(used & (f == 0))
    def _():
        hb_ref[...] = xs_ref[...].astype(BF16)
        acc_ref[...] = jnp.zeros_like(acc_ref)

    @pl.when(used)
    def _():
        h = hb_ref[...]
        a = _dot(h, wg_ref[0])
        z = a * _sigmoid(a) * _dot(h, wu_ref[0])
        acc_ref[...] += _dot(z.astype(BF16), wd_ref[0])

    @pl.when(used & last)
    def _():
        ys_ref[...] = acc_ref[...]

    @pl.when(jnp.logical_not(used) & last)
    def _():
        ys_ref[...] = jnp.zeros_like(ys_ref)


def _experts(tbl, xs, wg, wu, wd, blk):
    R, D = xs.shape
    F = wg.shape[-1]
    nblk = R // blk
    tf = _tile(512, F)
    return pl.pallas_call(
        functools.partial(_expert_kernel, nblk=nblk),
        grid_spec=pltpu.PrefetchScalarGridSpec(
            num_scalar_prefetch=1,
            grid=(nblk, F // tf),
            in_specs=[pl.BlockSpec((blk, D), lambda i, f, tbl: (i, 0)),
                      pl.BlockSpec((1, D, tf), lambda i, f, tbl: (tbl[i], 0, f)),
                      pl.BlockSpec((1, D, tf), lambda i, f, tbl: (tbl[i], 0, f)),
                      pl.BlockSpec((1, tf, D), lambda i, f, tbl: (tbl[i], f, 0))],
            out_specs=pl.BlockSpec((blk, D), lambda i, f, tbl: (i, 0)),
            scratch_shapes=[pltpu.VMEM((blk, D), BF16), pltpu.VMEM((blk, D), F32)]),
        out_shape=jax.ShapeDtypeStruct((R, D), F32),
        compiler_params=_params("parallel", "arbitrary"),
        name="moe_experts",
    )(tbl, xs, wg, wu, wd)


def _combine_kernel(dest_ref, acc_ref, route_ref, mod_ref, fn_ref, ys_ref, o_ref, buf_ref, sem, *, wlane, final):
    tm = acc_ref.shape[0]
    base = pl.program_id(0) * tm
    _all_rows(tm, lambda r: pltpu.make_async_copy(
        ys_ref.at[pl.ds(dest_ref[base + r], 1)], buf_ref.at[pl.ds(r, 1)], sem))
    y = acc_ref[...] + mod_ref[0][5:6] * (route_ref[:, wlane:wlane + 1] * buf_ref[...])
    if final:
        ms = jnp.mean(y * y, axis=-1, keepdims=True)
        y = y * lax.rsqrt(ms + NORM_EPS) * fn_ref[...]
    o_ref[...] = y


def _combine(dest, acc, route, mod, final_norm, ys, seq, wlane, final):
    N, D = acc.shape
    tm = _tile(512, seq)
    return pl.pallas_call(
        functools.partial(_combine_kernel, wlane=wlane, final=final),
        grid_spec=pltpu.PrefetchScalarGridSpec(
            num_scalar_prefetch=1,
            grid=(N // tm,),
            in_specs=[pl.BlockSpec((tm, D), lambda i, dest: (i, 0)),
                      pl.BlockSpec((tm, LANES), lambda i, dest: (i, 0)),
                      pl.BlockSpec((1, 8, D), lambda i, dest: ((i * tm) // seq, 0, 0)),
                      pl.BlockSpec((1, D), lambda i, dest: (0, 0)),
                      pl.BlockSpec(memory_space=pl.ANY)],
            out_specs=pl.BlockSpec((tm, D), lambda i, dest: (i, 0)),
            scratch_shapes=[pltpu.VMEM((tm, D), F32), pltpu.SemaphoreType.DMA]),
        out_shape=jax.ShapeDtypeStruct((N, D), F32),
        compiler_params=_params("arbitrary"),
        name="moe_combine",
    )(dest, acc, route, mod, final_norm.reshape(1, D), ys)


def _moe_final(xl, mod, gain, router, wg, wu, wd, final_norm, seq):
    N, D = xl.shape
    blk = MOE_BLOCK_ROWS
    route, counts = _router(xl, mod, gain, router, seq)
    counts = counts[0, :N_EXPERTS].astype(jnp.int32)
    padded = (counts + blk - 1) // blk * blk
    ends = jnp.cumsum(padded)
    starts = ends - padded
    nblk = -(-(2 * N) // blk) + N_EXPERTS
    blk_expert = jnp.minimum(jnp.searchsorted(ends, jnp.arange(nblk, dtype=jnp.int32) * blk, side='right'),
                             N_EXPERTS - 1).astype(jnp.int32)
    tbl = jnp.concatenate([blk_expert, (ends[-1:] // blk).astype(jnp.int32)])
    dests = [starts[route[:, e].astype(jnp.int32)] + route[:, r].astype(jnp.int32)
             for e, r in ((R_E1, R_RANK1), (R_E2, R_RANK2))]
    xs = jnp.zeros((nblk * blk, D), F32)
    for dest in dests:
        xs = _dispatch(dest, xl, mod, gain, xs, seq)
    ys = _experts(tbl, xs, wg, wu, wd, blk)
    acc = _combine(dests[0], xl, route, mod, final_norm, ys, seq, R_W1, False)
    return _combine(dests[1], acc, route, mod, final_norm, ys, seq, R_W2, True)


def _rope_tables(seq):
    rows = seq // GRID_W
    row, col = jnp.meshgrid(jnp.arange(rows), jnp.arange(GRID_W), indexing='ij')
    axis = HEAD // 2
    inv = ROPE_THETA ** (-jnp.arange(0, axis, 2, dtype=F32) / axis)
    ang = jnp.concatenate([row.reshape(-1, 1).astype(F32) * inv, col.reshape(-1, 1).astype(F32) * inv], axis=-1)
    cos = jnp.repeat(jnp.cos(ang), 2, axis=-1)
    sin = jnp.repeat(jnp.sin(ang), 2, axis=-1) * jnp.tile(jnp.array([-1.0, 1.0], F32), HEAD // 2)
    return jnp.tile(cos, (1, GROUPS)), jnp.tile(sin, (1, GROUPS))


def kernel(x, c, ctx, c_ctx, ada_w, ada_b, norm1, norm2, w_in, shift_mu, rwkv_w0, rwkv_w2, rwkv_a0, rwkv_a2, rwkv_g2, rwkv_kk, rwkv_ka, rwkv_rk, lnx_w, lnx_b, q_norm, k_norm, w_pa, w_pb, w_o, ffn_wg, ffn_wu, ffn_wd, router, moe_wg, moe_wu, moe_wd, final_norm):
    B, T, D = x.shape
    CX = ctx.shape[1]
    depth = ada_w.shape[0]
    assert depth == 2 and ffn_wg.shape[0] == 1 and router.shape[0] == 1
    nc = B * CX
    bf = lambda w: w.astype(BF16)

    nrow = -(-(B + 1) // 8) * 8
    act = jnp.zeros((nrow, D), F32).at[:B].set(c).at[B].set(c_ctx)
    mod = _ada(act, ada_w, ada_b).reshape(depth, nrow, 6, D)
    mod = jnp.concatenate([mod, jnp.zeros((depth, nrow, 2, D), F32)], axis=2)

    cos, sin_signed = _rope_tables(T)
    xf = jnp.concatenate([ctx.reshape(nc, D), x.reshape(B * T, D)], axis=0)
    out = None
    for l in range(depth):
        last = l == depth - 1
        p_r, p_q, p_kv, p_g = _in_proj(xf, mod[l], norm1[l], bf(w_in[l]), nc, T, B)
        r, v, nkk, g, bonus, lw, keys, b = _rwkv_prep(
            p_r, shift_mu[l], rwkv_w0[l], rwkv_w2[l], rwkv_a0[l], rwkv_a2[l], rwkv_g2[l], rwkv_kk[l], rwkv_ka[l],
            rwkv_rk[l].reshape(-1), nc, CX, T)
        y_fwd, y_bwd = _wkv_scan_pairs(r, v, nkk, lw, keys, b, B, CX, T)
        k_att, v_att = _kv_prep(p_kv, k_norm[l], cos, sin_signed, nc, CX, T)
        y_att = _attention(p_q, k_att, v_att, q_norm[l], cos, sin_signed, B, nc, CX, T, latent=True)
        if not last:
            y_ctx = _attention(p_q, k_att, v_att, q_norm[l], cos, sin_signed, B, nc, CX, T, latent=False)
            y_att = jnp.concatenate([y_ctx, y_att], axis=0)
        xm = _merge(xf, mod[l], y_fwd, y_bwd, bonus, g, y_att, p_g, lnx_w[l], lnx_b[l], bf(w_pa[l]), bf(w_pb[l]), bf(w_o[l]),
                    nc, T, B, nc if last else 0)
        if not last:
            xf = _ffn(xm, mod[l], norm2[l], bf(ffn_wg[l // 2]), bf(ffn_wu[l // 2]), bf(ffn_wd[l // 2]), nc, T, B)
        else:
            out = _moe_final(xm, mod[l], norm2[l], router[l // 2], bf(moe_wg[l // 2]), bf(moe_wu[l // 2]),
                             bf(moe_wd[l // 2]), final_norm, T)
    return out.reshape(B, T, D)
```

```python
import functools
import math

import jax
import jax.numpy as jnp
from jax import lax
from jax.experimental import pallas as pl
from jax.experimental.pallas import tpu as pltpu

F32 = jnp.float32
BF16 = jnp.bfloat16

HEAD = 64
RWKV_HEADS = 8
RWKV_W = RWKV_HEADS * HEAD
LORA_W = 128
ATT_HEADS = 8
KV_HEADS = 2
GROUPS = ATT_HEADS // KV_HEADS
ATT_W = ATT_HEADS * HEAD
KV_W = KV_HEADS * HEAD
R_COLS = 3 * RWKV_W + 64 + 64 + 128
GRID_W = 64
ROPE_THETA = 10000.0
N_EXPERTS = 8
NORM_EPS = 1e-6
LNX_EPS = 64e-5
LANES = 128
NEG_BIG = -1e30
VMEM_LIMIT_BYTES = 56 * 1024 * 1024
SCAN_CHUNK = 64
SCAN_BATCHES_PER_STEP = 4
MOE_BLOCK_ROWS = 1024


def _params(*sem):
    return pltpu.CompilerParams(dimension_semantics=sem, vmem_limit_bytes=VMEM_LIMIT_BYTES)


def _tile(pref, *dims):
    t = pref
    while any(d % t for d in dims):
        t //= 2
    return t


def _dot(a, b):
    return jnp.dot(a, b, preferred_element_type=F32)


def _split(x):
    hi = x.astype(BF16)
    lo = (x - hi.astype(F32)).astype(BF16)
    return hi, lo


def _dot3(a, b):
    ah, al = _split(a)
    bh, bl = _split(b)
    return _dot(ah, bh) + (_dot(ah, bl) + _dot(al, bh))


def _dot2(a, b_bf16):
    ah, al = _split(a)
    return _dot(ah, b_bf16) + _dot(al, b_bf16)


def _sigmoid(x):
    return 1.0 / (1.0 + jnp.exp(-x))


def _normmod(x, gain, scale, shift):
    ms = jnp.mean(x * x, axis=-1, keepdims=True)
    return x * lax.rsqrt(ms + NORM_EPS) * gain * (1.0 + scale) + shift


def _head_ones(width):
    r = lax.broadcasted_iota(jnp.int32, (width, width), 0) // HEAD
    c = lax.broadcasted_iota(jnp.int32, (width, width), 1) // HEAD
    return jnp.where(r == c, 1.0, 0.0).astype(BF16)


def _ada_kernel(a_ref, w_ref, b_ref, o_ref):
    a = a_ref[...]
    o_ref[0] = _dot3(a * _sigmoid(a), w_ref[0]) + b_ref[0]


def _ada(act, ada_w, ada_b):
    L, D, N6 = ada_w.shape
    R = act.shape[0]
    tn = _tile(1536, N6)
    return pl.pallas_call(
        _ada_kernel,
        grid=(L, N6 // tn),
        in_specs=[pl.BlockSpec((R, D), lambda l, j: (0, 0)),
                  pl.BlockSpec((1, D, tn), lambda l, j: (l, 0, j)),
                  pl.BlockSpec((1, 1, tn), lambda l, j: (l, 0, j))],
        out_specs=pl.BlockSpec((1, R, tn), lambda l, j: (l, 0, j)),
        out_shape=jax.ShapeDtypeStruct((L, R, N6), F32),
        compiler_params=_params("parallel", "parallel"),
        name="ada",
    )(act, ada_w, ada_b.reshape(L, 1, N6))


def _in_kernel(x_ref, mod_ref, g_ref, w_ref, pr_ref, pq_ref, pkv_ref, pg_ref):
    m = mod_ref[0]
    h = _normmod(x_ref[...], g_ref[...], m[1:2], m[0:1]).astype(BF16)
    col = 0
    for ref in (pr_ref, pq_ref, pkv_ref, pg_ref):
        n = ref.shape[-1]
        for c0 in range(0, n, 256):
            ref[:, c0:c0 + 256] = _dot(h, w_ref[:, col + c0:col + c0 + 256])
        col += n


def _mod_index(tm, nc, seq, nb):
    def index(i):
        start = i * tm
        return (jnp.where(start < nc, nb, (start - nc) // seq), 0, 0)
    return index


def _in_proj(xf, mod, gain, w_bf16, nc, seq, nb):
    N, D = xf.shape
    tm = _tile(512, nc, seq)
    widths = (R_COLS, ATT_W, 2 * KV_W, 2 * D)
    return pl.pallas_call(
        _in_kernel,
        grid=(N // tm,),
        in_specs=[pl.BlockSpec((tm, D), lambda i: (i, 0)),
                  pl.BlockSpec((1, 8, D), _mod_index(tm, nc, seq, nb)),
                  pl.BlockSpec((1, D), lambda i: (0, 0)),
                  pl.BlockSpec(w_bf16.shape, lambda i: (0, 0))],
        out_specs=[pl.BlockSpec((tm, w), lambda i: (i, 0)) for w in widths],
        out_shape=[jax.ShapeDtypeStruct((N, w), F32) for w in widths],
        compiler_params=_params("parallel"),
        name="in_proj",
    )(xf, mod, gain.reshape(1, D), w_bf16)


def _prep_kernel(p_ref, prev_ref, next_ref, mu_ref, w0_ref, w2_ref, a0_ref, a2_ref, g2_ref, kk_ref, ka_ref,
                 rk_ref, r_ref, v_ref, nkk_ref, g_ref, bonus_ref, lw_ref, keys_ref, b_ref, *, tt, nc, ctx, seq):
    i = pl.program_id(0)
    start = i * tt
    in_ctx = start < nc
    pos = jnp.where(in_ctx, start % ctx, (start - nc) % seq)
    seg = jnp.where(in_ctx, ctx, seq)
    first = pos == 0
    last = pos + tt == seg
    p = p_ref[...]
    rows = lax.broadcasted_iota(jnp.int32, (tt, 1), 0)
    prev_row = jnp.where(first, 0.0, prev_ref[7:8, :])
    next_row = jnp.where(last, 0.0, next_ref[0:1, :])
    prev = jnp.where(rows == 0, prev_row, pltpu.roll(p, 1, 0))
    nxt = jnp.where(rows == tt - 1, next_row, pltpu.roll(p, tt - 1, 0))
    ps = p + mu_ref[0:1, :] * (prev - p) + mu_ref[1:2, :] * (nxt - p)
    W = RWKV_W
    r = ps[:, 0:W]
    k = ps[:, W:2 * W]
    v = ps[:, 2 * W:3 * W]
    wa = ps[:, 3 * W:3 * W + LORA_W]
    gd = ps[:, 3 * W + LORA_W:]
    ones = _head_ones(W)
    kkf = k * kk_ref[...]
    nrm = jnp.sqrt(_dot2(kkf * kkf, ones))
    kk = kkf / jnp.maximum(nrm, 1e-12)
    tw = jnp.tanh(wa)
    ksum = jnp.zeros_like(k)
    for d in range(2):
        z = w0_ref[d:d + 1, :] + _dot3(tw, w2_ref[d])
        lw_ref[d] = -math.exp(-0.5) * _sigmoid(z)
        a = _sigmoid(a0_ref[d:d + 1, :] + _dot3(wa, a2_ref[d]))
        keys = k * (1.0 + (a - 1.0) * ka_ref[...])
        keys_ref[d] = keys
        b_ref[d] = kk * a
        ksum = ksum + keys
    r_ref[...] = r
    v_ref[...] = v
    nkk_ref[...] = -kk
    g_ref[...] = _dot3(_sigmoid(gd), g2_ref[...])
    bonus_ref[...] = _dot2(r * (0.5 * ksum) * rk_ref[...], ones) * v


def _rwkv_prep(p_r, mu, w0, w2, a0, a2, g2, k_k, k_a, r_k, nc, ctx, seq):
    N = p_r.shape[0]
    W = RWKV_W
    tt = _tile(256, ctx, seq)
    nblk8 = N // 8
    zeros = jnp.zeros((2, 64, W), F32)
    w2p = jnp.concatenate([w2, zeros], axis=1)
    a2p = jnp.concatenate([zeros, a2], axis=1)
    row = lambda t: t.reshape(1, W)
    full = lambda a: pl.BlockSpec(a.shape, lambda i: (0,) * a.ndim)
    consts = (mu, w0, w2p, a0, a2p, g2, row(k_k), row(k_a), row(r_k))
    one = jax.ShapeDtypeStruct((N, W), F32)
    two = jax.ShapeDtypeStruct((2, N, W), F32)
    s1 = pl.BlockSpec((tt, W), lambda i: (i, 0))
    s2 = pl.BlockSpec((2, tt, W), lambda i: (0, i, 0))
    return pl.pallas_call(
        functools.partial(_prep_kernel, tt=tt, nc=nc, ctx=ctx, seq=seq),
        grid=(N // tt,),
        in_specs=[pl.BlockSpec((tt, R_COLS), lambda i: (i, 0)),
                  pl.BlockSpec((8, R_COLS), lambda i: (jnp.maximum(i * (tt // 8) - 1, 0), 0)),
                  pl.BlockSpec((8, R_COLS), lambda i: (jnp.minimum((i + 1) * (tt // 8), nblk8 - 1), 0))]
                 + [full(a) for a in consts],
        out_specs=[s1, s1, s1, s1, s1, s2, s2, s2],
        out_shape=[one, one, one, one, one, two, two, two],
        compiler_params=_params("parallel"),
        name="rwkv_prep",
    )(p_r, p_r, p_r, *consts)


def _pair_blocks(x):
    left = lax.broadcasted_iota(jnp.int32, x.shape, 1) < HEAD
    zero = jnp.zeros_like(x)
    return jnp.concatenate([jnp.where(left, x, zero), jnp.where(left, zero, x)], axis=0)


def _pair_diag(x):
    left = lax.broadcasted_iota(jnp.int32, (HEAD, LANES), 1) < HEAD
    return jnp.where(left, x[:HEAD], x[HEAD:])


def _scan_pair_kernel(*refs, C, direction, nslot):
    assert C == HEAD
    y_ref, st_ref = refs[6 * nslot], refs[6 * nslot + 1]

    @pl.when(pl.program_id(1) == 0)
    def _():
        st_ref[...] = jnp.zeros_like(st_ref)

    npair = RWKV_HEADS // 2
    row = lax.broadcasted_iota(jnp.int32, (C, C), 0)
    col = lax.broadcasted_iota(jnp.int32, (C, C), 1)
    before = row - col if direction == 0 else col - row
    linc = jnp.where(before >= 0, 1.0, 0.0).astype(BF16)
    row2 = lax.broadcasted_iota(jnp.int32, (C, LANES), 0)
    col2 = lax.broadcasted_iota(jnp.int32, (C, LANES), 1) % C
    before2 = row2 - col2 if direction == 0 else col2 - row2
    strict = before2 > 0
    incl = before2 >= 0
    eye2 = row2 == col2
    eye2b = jnp.where(eye2, 1.0, 0.0).astype(BF16)
    nt = (((1,), (1,)), ((), ()))
    tn = (((0,), (0,)), ((), ()))
    nlev = int(math.log2(C))

    at, rt, bt, kt, bh, kh, vb, g_tot = ([] for _ in range(8))
    for s in range(nslot):
        r_ref, v_ref, a_ref, lw_ref, k_ref, b_ref = refs[6 * s:6 * s + 6]
        lw = lw_ref[0]
        lh, ll = _split(lw)
        lam = _dot(linc, lh) + _dot(linc, ll)
        tot = jnp.sum(lw, axis=0, keepdims=True)
        g_rem = jnp.exp(tot - lam)
        g_inv = jnp.exp(-lam)
        gt = jnp.exp(tot)
        k = k_ref[0]
        b = b_ref[0]
        a_s = a_ref[...] * jnp.exp(lam - lw)
        r_s = r_ref[...] * jnp.exp(lam)
        b_t = (b * g_inv).astype(BF16)
        k_t = (k * g_inv).astype(BF16)
        b_h = (b * g_rem).astype(BF16)
        k_h = (k * g_rem).astype(BF16)
        v_b = v_ref[...].astype(BF16)
        for p in range(npair):
            sl = slice(p * LANES, (p + 1) * LANES)
            for dst, src in ((at, a_s), (rt, r_s), (bt, b_t), (kt, k_t), (bh, b_h), (kh, k_h), (vb, v_b), (g_tot, gt)):
                dst.append(src[:, sl])
    nch = nslot * npair

    ar = [jnp.concatenate([at[c].astype(BF16), rt[c].astype(BF16)], axis=0) for c in range(nch)]
    xb = [lax.dot_general(ar[c], _pair_blocks(bt[c]), nt, preferred_element_type=F32) for c in range(nch)]
    xk = [lax.dot_general(ar[c], _pair_blocks(kt[c]), nt, preferred_element_type=F32) for c in range(nch)]
    s_pow = [jnp.where(strict, x[:C], 0.0) for x in xb]
    m_rb = [jnp.where(incl, x[C:], 0.0).astype(BF16) for x in xb]
    q = [jnp.where(strict, x[:C], 0.0) for x in xk]
    m_rk = [jnp.where(incl, x[C:], 0.0) for x in xk]
    ah = list(at)
    for lev in range(nlev):
        for c in range(nch):
            sb = s_pow[c].astype(BF16)
            ah[c] = ah[c] + _dot(sb, _pair_blocks(ah[c].astype(BF16)))
            q[c] = q[c] + _dot(sb, _pair_blocks(q[c].astype(BF16)))
            if lev < nlev - 1:
                s_pow[c] = _dot(sb, _pair_blocks(sb))
    ahb = [x.astype(BF16) for x in ah]
    qb = [x.astype(BF16) for x in q]
    ra = [_dot(m_rb[c], _pair_blocks(ahb[c])) for c in range(nch)]
    rq = [_dot(m_rb[c], _pair_blocks(qb[c])) for c in range(nch)]
    ba = [_pair_diag(lax.dot_general(bh[c], ahb[c], tn, preferred_element_type=F32)) for c in range(nch)]
    bq = [_pair_diag(lax.dot_general(bh[c], qb[c], tn, preferred_element_type=F32)) for c in range(nch)]
    k_tr = [_pair_diag(lax.dot_general(kh[c], eye2b, tn, preferred_element_type=F32)) for c in range(nch)]
    for c in range(nch):
        s, p = divmod(c, npair)
        r_hat = rt[c] + ra[c]
        g_mat = jnp.where(eye2, g_tot[c], 0.0) + ba[c]
        v2 = _pair_blocks(vb[c])
        y_in = _dot((m_rk[c] + rq[c]).astype(BF16), v2)
        h_mat = _dot((bq[c] + k_tr[c]).astype(BF16), v2)
        sh, sl_ = _split(st_ref[c])
        sh, sl_ = _pair_blocks(sh), _pair_blocks(sl_)
        gh, gl = _split(g_mat)
        st_ref[c] = _dot(gh, sh) + (_dot(gh, sl_) + _dot(gl, sh)) + h_mat
        y_ref[s, :, p * LANES:(p + 1) * LANES] = _dot(r_hat.astype(BF16), sh) + y_in


def _wkv_scan_pairs(r, v, nkk, lw, keys, b, nb, ctx, seq):
    N, W = r.shape
    C = SCAN_CHUNK
    ncc, nlc = ctx // C, seq // C
    nslot = math.gcd(SCAN_BATCHES_PER_STEP, nb)
    hb = nb // nslot

    def blk(d, j, bi, nbat):
        jc = j if d == 0 else ncc - 1 - j
        jl = j - ncc if d == 0 else nlc - 1 - (j - ncc)
        return jnp.where(j < ncc, bi * ncc + jc, nbat * ncc + bi * nlc + jl)

    def one(d):
        s1 = lambda s: pl.BlockSpec((C, W), lambda i, j: (blk(d, j, s * hb + i, nb), 0))
        s2 = lambda s: pl.BlockSpec((1, C, W), lambda i, j: (d, blk(d, j, s * hb + i, nb), 0))
        in_specs = [spec for s in range(nslot) for spec in (s1(s), s1(s), s1(s), s2(s), s2(s), s2(s))]
        return pl.pallas_call(
            functools.partial(_scan_pair_kernel, C=C, direction=d, nslot=nslot),
            grid=(hb, ncc + nlc),
            in_specs=in_specs,
            out_specs=pl.BlockSpec((nslot, C, W), lambda i, j: (0, blk(d, j, i, hb), 0)),
            out_shape=jax.ShapeDtypeStruct((nslot, N // nslot, W), F32),
            scratch_shapes=[pltpu.VMEM((nslot * RWKV_HEADS // 2, HEAD, LANES), F32)],
            compiler_params=_params("parallel", "arbitrary"),
            name="wkv_scan",
        )(*([r, v, nkk, lw, keys, b] * nslot))

    return one(0), one(1)


def _slot_rows(tm, nc, nl, nslot):
    def index(i):
        start = i * tm
        in_ctx = start < nc
        rel = jnp.where(in_ctx, start, start - nc)
        per = jnp.where(in_ctx, nc // nslot, nl // nslot)
        local = jnp.where(in_ctx, 0, nc // nslot) + rel % per
        return (rel // per, local // tm, 0)
    return index


def _rope(t, cos, sin_signed):
    w = t.shape[-1]
    lane = lax.broadcasted_iota(jnp.int32, t.shape, 1)
    swapped = jnp.where(lane % 2 == 0, pltpu.roll(t, w - 1, 1), pltpu.roll(t, 1, 1))
    return t * cos + swapped * sin_signed


def _head_rms(t, gain):
    ms = _dot2(t * t, _head_ones(t.shape[-1])) * (1.0 / HEAD)
    return t * lax.rsqrt(ms + NORM_EPS) * gain


def _kv_kernel(p_ref, kn_ref, cos_ref, sin_ref, k_ref, v_ref, *, tt, nc):
    p = p_ref[...]
    kn = _head_rms(p[:, :KV_W], kn_ref[...])
    kr = _rope(kn, cos_ref[...], sin_ref[...])
    k = jnp.where(pl.program_id(0) * tt < nc, kn, kr).astype(BF16)
    v = p[:, KV_W:].astype(BF16)
    one_col = jnp.where(lax.broadcasted_iota(jnp.int32, (tt, HEAD), 1) == 0, 1.0, 0.0).astype(BF16)
    for h in range(KV_HEADS):
        k_ref[h] = k[:, h * HEAD:(h + 1) * HEAD]
        v_ref[h] = jnp.concatenate([v[:, h * HEAD:(h + 1) * HEAD], one_col], axis=1)


def _kv_prep(p_kv, k_norm, cos, sin_signed, nc, ctx, seq):
    N = p_kv.shape[0]
    tt = _tile(512, ctx, seq)
    npos = seq // tt
    pos = lambda i: ((jnp.maximum(i * tt - nc, 0) // tt) % npos, 0)
    spec = lambda w: pl.BlockSpec((KV_HEADS, tt, w), lambda i: (0, i, 0))
    return pl.pallas_call(
        functools.partial(_kv_kernel, tt=tt, nc=nc),
        grid=(N // tt,),
        in_specs=[pl.BlockSpec((tt, 2 * KV_W), lambda i: (i, 0)),
                  pl.BlockSpec((1, KV_W), lambda i: (0, 0)),
                  pl.BlockSpec((tt, KV_W), pos),
                  pl.BlockSpec((tt, KV_W), pos)],
        out_specs=[spec(HEAD), spec(2 * HEAD)],
        out_shape=[jax.ShapeDtypeStruct((KV_HEADS, N, w), BF16) for w in (HEAD, 2 * HEAD)],
        compiler_params=_params("parallel"),
        name="kv_prep",
    )(p_kv, jnp.tile(k_norm, KV_HEADS).reshape(1, KV_W), cos[:, :KV_W], sin_signed[:, :KV_W])


def _attn_kernel(*refs, nseg, rope):
    q_ref, qn_ref = refs[0], refs[1]
    pos = 2
    if rope:
        cos_ref, sin_ref = refs[2], refs[3]
        pos = 4
    kv_refs = refs[pos:pos + 2 * nseg]
    o_ref = refs[pos + 2 * nseg]
    q = _head_rms(q_ref[...], qn_ref[...])
    if rope:
        q = _rope(q, cos_ref[...], sin_ref[...])
    q = (q * (HEAD ** -0.5 * math.log2(math.e))).astype(BF16)
    outs = []
    for g in range(GROUPS):
        qh = q[:, g * HEAD:(g + 1) * HEAD]
        s = [lax.dot_general(qh, kv_refs[2 * i][0], (((1,), (1,)), ((), ())), preferred_element_type=F32)
             for i in range(nseg)]
        m = s[0].max(axis=-1, keepdims=True)
        for si in s[1:]:
            m = jnp.maximum(m, si.max(axis=-1, keepdims=True))
        acc = None
        for i in range(nseg):
            pv = _dot(jnp.exp2(s[i] - m).astype(BF16), kv_refs[2 * i + 1][0])
            acc = pv if acc is None else acc + pv
        outs.append(acc[:, :HEAD] / acc[:, HEAD:HEAD + 1])
    o_ref[...] = jnp.concatenate(outs, axis=1)


def _attention(p_q, k, v, q_norm, cos, sin_signed, nb, nc, ctx, seq, latent):
    qlen = seq if latent else ctx
    tq = _tile(256, qlen)
    nqt = qlen // tq
    qw = GROUPS * HEAD
    qoff = nc // tq if latent else 0
    qmap = lambda bi, g, t: (qoff + bi * nqt + t, g)
    in_specs = [pl.BlockSpec((tq, qw), qmap), pl.BlockSpec((1, qw), lambda bi, g, t: (0, 0))]
    args = [p_q, jnp.tile(q_norm, GROUPS).reshape(1, qw)]
    ctx_spec = lambda a: pl.BlockSpec((1, ctx, a.shape[-1]), lambda bi, g, t: (g, bi, 0))
    if latent:
        in_specs += [pl.BlockSpec((tq, qw), lambda bi, g, t: (t, 0))] * 2
        args += [cos, sin_signed]
        lat_spec = lambda a: pl.BlockSpec((1, seq, a.shape[-1]), lambda bi, g, t: (g, nc // seq + bi, 0))
        in_specs += [lat_spec(k), lat_spec(v), ctx_spec(k), ctx_spec(v)]
        args += [k, v, k, v]
    else:
        in_specs += [ctx_spec(k), ctx_spec(v)]
        args += [k, v]
    return pl.pallas_call(
        functools.partial(_attn_kernel, nseg=2 if latent else 1, rope=latent),
        grid=(nb, KV_HEADS, nqt),
        in_specs=in_specs,
        out_specs=pl.BlockSpec((tq, qw), lambda bi, g, t: (bi * nqt + t, g)),
        out_shape=jax.ShapeDtypeStruct((nb * qlen, ATT_W), F32),
        compiler_params=_params("parallel", "parallel", "parallel"),
        name="attn_lat" if latent else "attn_ctx",
    )(*args)


def _merge_kernel(x_ref, mod_ref, yf_ref, yb_ref, bonus_ref, g_ref, ya_ref, pg_ref, lw_ref, lb_ref, wpa_ref, wpb_ref,
                  wo_ref, o_ref):
    D = x_ref.shape[-1]
    ones = _head_ones(RWKV_W)
    y = yf_ref[0] + yb_ref[0]
    mean = _dot2(y, ones) * (1.0 / HEAD)
    yc = y - mean
    var = _dot2(yc * yc, ones) * (1.0 / HEAD)
    yn = yc * lax.rsqrt(var + LNX_EPS) * lw_ref[...] + lb_ref[...]
    y_rwkv = (yn + bonus_ref[...]) * g_ref[...]
    pg = pg_ref[...]
    mix = (_sigmoid(pg[:, :D]) * _dot(y_rwkv.astype(BF16), wpa_ref[...])
           + _sigmoid(pg[:, D:]) * _dot(ya_ref[...].astype(BF16), wpb_ref[...]))
    o_ref[...] = x_ref[...] + mod_ref[0][2:3] * _dot(mix.astype(BF16), wo_ref[...])


def _merge(xf, mod, y_fwd, y_bwd, bonus, g, y_att, p_g, lnx_w, lnx_b, w_pa, w_pb, w_o, nc, seq, nb, row0):
    N, D = xf.shape
    W = RWKV_W
    tm = _tile(512, nc // y_fwd.shape[0], seq)
    off = row0 // tm
    n_out = N - row0
    mod_idx = _mod_index(tm, nc, seq, nb)
    rows = lambda w: pl.BlockSpec((tm, w), lambda i: (i + off, 0))
    slot_index = _slot_rows(tm, nc, N - nc, y_fwd.shape[0])
    scan_rows = pl.BlockSpec((1, tm, W), lambda i: slot_index(i + off))
    full = lambda a: pl.BlockSpec(a.shape, lambda i: (0,) * a.ndim)
    consts = (lnx_w.reshape(1, W), lnx_b.reshape(1, W), w_pa, w_pb, w_o)
    return pl.pallas_call(
        _merge_kernel,
        grid=(n_out // tm,),
        in_specs=[rows(D), pl.BlockSpec((1, 8, D), lambda i: mod_idx(i + off)),
                  scan_rows, scan_rows, rows(W), rows(W), pl.BlockSpec((tm, W), lambda i: (i, 0)), rows(2 * D)]
                 + [full(a) for a in consts],
        out_specs=pl.BlockSpec((tm, D), lambda i: (i, 0)),
        out_shape=jax.ShapeDtypeStruct((n_out, D), F32),
        compiler_params=_params("parallel"),
        name="merge",
    )(xf, mod, y_fwd, y_bwd, bonus, g, y_att, p_g, *consts)


def _ffn_kernel(x_ref, mod_ref, g_ref, wg_ref, wu_ref, wd_ref, o_ref, h_ref, acc_ref):
    f = pl.program_id(1)

    @pl.when(f == 0)
    def _():
        m = mod_ref[0]
        h_ref[...] = _normmod(x_ref[...], g_ref[...], m[4:5], m[3:4]).astype(BF16)
        acc_ref[...] = jnp.zeros_like(acc_ref)

    h = h_ref[...]
    a = _dot(h, wg_ref[...])
    z = a * _sigmoid(a) * _dot(h, wu_ref[...])
    acc_ref[...] += _dot(z.astype(BF16), wd_ref[...])

    @pl.when(f == pl.num_programs(1) - 1)
    def _():
        o_ref[...] = x_ref[...] + mod_ref[0][5:6] * acc_ref[...]


def _ffn(xf, mod, gain, wg, wu, wd, nc, seq, nb):
    N, D = xf.shape
    F = wg.shape[-1]
    tm = _tile(1024, nc, seq)
    tf = _tile(512, F)
    return pl.pallas_call(
        _ffn_kernel,
        grid=(N // tm, F // tf),
        in_specs=[pl.BlockSpec((tm, D), lambda i, f: (i, 0)),
                  pl.BlockSpec((1, 8, D), lambda i, f: _mod_index(tm, nc, seq, nb)(i)),
                  pl.BlockSpec((1, D), lambda i, f: (0, 0)),
                  pl.BlockSpec((D, tf), lambda i, f: (0, f)),
                  pl.BlockSpec((D, tf), lambda i, f: (0, f)),
                  pl.BlockSpec((tf, D), lambda i, f: (f, 0))],
        out_specs=pl.BlockSpec((tm, D), lambda i, f: (i, 0)),
        out_shape=jax.ShapeDtypeStruct((N, D), F32),
        scratch_shapes=[pltpu.VMEM((tm, D), BF16), pltpu.VMEM((tm, D), F32)],
        compiler_params=_params("parallel", "arbitrary"),
        name="ffn",
    )(xf, mod, gain.reshape(1, D), wg, wu, wd)


R_E1, R_E2, R_RANK1, R_RANK2, R_W1, R_W2 = range(6)


def _router_kernel(x_ref, mod_ref, g_ref, wr_ref, route_ref, cnt_ref, carry_ref):
    @pl.when(pl.program_id(0) == 0)
    def _():
        carry_ref[...] = jnp.zeros_like(carry_ref)

    m = mod_ref[0]
    h = _normmod(x_ref[...], g_ref[...], m[4:5], m[3:4])
    tm = h.shape[0]
    lane = lax.broadcasted_iota(jnp.int32, (tm, LANES), 1).astype(F32)
    logits = jnp.where(lane < N_EXPERTS, _dot3(h, wr_ref[...]), NEG_BIG)
    m1 = logits.max(axis=-1, keepdims=True)
    i1 = jnp.where(logits == m1, lane, float(LANES)).min(axis=-1, keepdims=True)
    rest = jnp.where(lane == i1, NEG_BIG, logits)
    m2 = rest.max(axis=-1, keepdims=True)
    i2 = jnp.where(rest == m2, lane, float(LANES)).min(axis=-1, keepdims=True)
    e2 = jnp.exp(m2 - m1)
    w1 = 1.0 / (1.0 + e2)
    w2 = e2 / (1.0 + e2)
    onehot = jnp.where(lane == i1, 1.0, jnp.where(lane == i2, 1.0, 0.0))
    earlier = (lax.broadcasted_iota(jnp.int32, (tm, tm), 0) > lax.broadcasted_iota(jnp.int32, (tm, tm), 1))
    before = _dot(jnp.where(earlier, 1.0, 0.0).astype(BF16), onehot.astype(BF16)) + carry_ref[...]
    rank1 = jnp.where(lane == i1, before, 0.0).sum(axis=-1, keepdims=True)
    rank2 = jnp.where(lane == i2, before, 0.0).sum(axis=-1, keepdims=True)
    carry_ref[...] += onehot.sum(axis=0, keepdims=True)
    cnt_ref[...] = jnp.broadcast_to(carry_ref[...], cnt_ref.shape)
    rec = jnp.zeros_like(lane)
    for idx, val in ((R_E1, i1), (R_E2, i2), (R_RANK1, rank1), (R_RANK2, rank2), (R_W1, w1), (R_W2, w2)):
        rec = jnp.where(lane == float(idx), val, rec)
    route_ref[...] = rec


def _router(xl, mod, gain, router, seq):
    N, D = xl.shape
    tm = _tile(512, seq)
    wr = jnp.zeros((D, LANES), F32).at[:, :N_EXPERTS].set(router)
    return pl.pallas_call(
        _router_kernel,
        grid=(N // tm,),
        in_specs=[pl.BlockSpec((tm, D), lambda i: (i, 0)),
                  pl.BlockSpec((1, 8, D), lambda i: ((i * tm) // seq, 0, 0)),
                  pl.BlockSpec((1, D), lambda i: (0, 0)),
                  pl.BlockSpec((D, LANES), lambda i: (0, 0))],
        out_specs=[pl.BlockSpec((tm, LANES), lambda i: (i, 0)), pl.BlockSpec((8, LANES), lambda i: (0, 0))],
        out_shape=[jax.ShapeDtypeStruct((N, LANES), F32), jax.ShapeDtypeStruct((8, LANES), F32)],
        scratch_shapes=[pltpu.VMEM((1, LANES), F32)],
        compiler_params=_params("arbitrary"),
        name="router",
    )(xl, mod, gain.reshape(1, D), wr)


def _all_rows(n, copy):
    def start(r, c):
        copy(r).start()
        return c

    def wait(r, c):
        copy(r).wait()
        return c

    lax.fori_loop(0, n, start, 0, unroll=8)
    lax.fori_loop(0, n, wait, 0, unroll=8)


def _dispatch_kernel(dest_ref, x_ref, mod_ref, g_ref, xs_in_ref, xs_ref, h_ref, sem):
    del xs_in_ref
    tm = x_ref.shape[0]
    n = tm * pl.num_programs(0)
    base = pl.program_id(0) * tm
    m = mod_ref[0]
    h_ref[...] = _normmod(x_ref[...], g_ref[...], m[4:5], m[3:4])
    _all_rows(2 * tm, lambda r: pltpu.make_async_copy(
        h_ref.at[pl.ds(r % tm, 1)], xs_ref.at[pl.ds(dest_ref[(r // tm) * n + base + r % tm], 1)], sem))


def _dispatch(dest, xl, mod, gain, xs, seq):
    N, D = xl.shape
    tm = _tile(512, seq)
    return pl.pallas_call(
        _dispatch_kernel,
        grid_spec=pltpu.PrefetchScalarGridSpec(
            num_scalar_prefetch=1,
            grid=(N // tm,),
            in_specs=[pl.BlockSpec((tm, D), lambda i, dest: (i, 0)),
                      pl.BlockSpec((1, 8, D), lambda i, dest: ((i * tm) // seq, 0, 0)),
                      pl.BlockSpec((1, D), lambda i, dest: (0, 0)),
                      pl.BlockSpec(memory_space=pl.ANY)],
            out_specs=pl.BlockSpec(memory_space=pl.ANY),
            scratch_shapes=[pltpu.VMEM((tm, D), F32), pltpu.SemaphoreType.DMA]),
        out_shape=jax.ShapeDtypeStruct(xs.shape, xs.dtype),
        input_output_aliases={4: 0},
        compiler_params=_params("arbitrary"),
        name="moe_dispatch",
    )(dest, xl, mod, gain.reshape(1, D), xs)


def _expert_kernel(tbl_ref, xs_ref, wg_ref, wu_ref, wd_ref, ys_ref, hb_ref, acc_ref, *, nblk):
    i = pl.program_id(0)
    f = pl.program_id(1)
    used = i < tbl_ref[nblk]
    last = f == pl.num_programs(1) - 1

    @pl.when(used & (f == 0))
    def _():
        hb_ref[...] = xs_ref[...].astype(BF16)
        acc_ref[...] = jnp.zeros_like(acc_ref)

    @pl.when(used)
    def _():
        h = hb_ref[...]
        a = _dot(h, wg_ref[0])
        z = a * _sigmoid(a) * _dot(h, wu_ref[0])
        acc_ref[...] += _dot(z.astype(BF16), wd_ref[0])

    @pl.when(used & last)
    def _():
        ys_ref[...] = acc_ref[...]

    @pl.when(jnp.logical_not(used) & last)
    def _():
        ys_ref[...] = jnp.zeros_like(ys_ref)


def _experts(tbl, xs, wg, wu, wd, blk):
    R, D = xs.shape
    F = wg.shape[-1]
    nblk = R // blk
    tf = _tile(512, F)
    return pl.pallas_call(
        functools.partial(_expert_kernel, nblk=nblk),
        grid_spec=pltpu.PrefetchScalarGridSpec(
            num_scalar_prefetch=1,
            grid=(nblk, F // tf),
            in_specs=[pl.BlockSpec((blk, D), lambda i, f, tbl: (i, 0)),
                      pl.BlockSpec((1, D, tf), lambda i, f, tbl: (tbl[i], 0, f)),
                      pl.BlockSpec((1, D, tf), lambda i, f, tbl: (tbl[i], 0, f)),
                      pl.BlockSpec((1, tf, D), lambda i, f, tbl: (tbl[i], f, 0))],
            out_specs=pl.BlockSpec((blk, D), lambda i, f, tbl: (i, 0)),
            scratch_shapes=[pltpu.VMEM((blk, D), BF16), pltpu.VMEM((blk, D), F32)]),
        out_shape=jax.ShapeDtypeStruct((R, D), F32),
        compiler_params=_params("parallel", "arbitrary"),
        name="moe_experts",
    )(tbl, xs, wg, wu, wd)


def _combine_kernel(dest_ref, x_ref, route_ref, mod_ref, fn_ref, ys_ref, o_ref, buf_ref, sem):
    tm = x_ref.shape[0]
    n = tm * pl.num_programs(0)
    base = pl.program_id(0) * tm
    _all_rows(2 * tm, lambda r: pltpu.make_async_copy(
        ys_ref.at[pl.ds(dest_ref[(r // tm) * n + base + r % tm], 1)], buf_ref.at[pl.ds(r, 1)], sem))
    mix = route_ref[:, R_W1:R_W1 + 1] * buf_ref[:tm] + route_ref[:, R_W2:R_W2 + 1] * buf_ref[tm:]
    y = x_ref[...] + mod_ref[0][5:6] * mix
    ms = jnp.mean(y * y, axis=-1, keepdims=True)
    o_ref[...] = y * lax.rsqrt(ms + NORM_EPS) * fn_ref[...]


def _combine(dest, acc, route, mod, final_norm, ys, seq):
    N, D = acc.shape
    tm = _tile(512, seq)
    return pl.pallas_call(
        _combine_kernel,
        grid_spec=pltpu.PrefetchScalarGridSpec(
            num_scalar_prefetch=1,
            grid=(N // tm,),
            in_specs=[pl.BlockSpec((tm, D), lambda i, dest: (i, 0)),
                      pl.BlockSpec((tm, LANES), lambda i, dest: (i, 0)),
                      pl.BlockSpec((1, 8, D), lambda i, dest: ((i * tm) // seq, 0, 0)),
                      pl.BlockSpec((1, D), lambda i, dest: (0, 0)),
                      pl.BlockSpec(memory_space=pl.ANY)],
            out_specs=pl.BlockSpec((tm, D), lambda i, dest: (i, 0)),
            scratch_shapes=[pltpu.VMEM((2 * tm, D), F32), pltpu.SemaphoreType.DMA]),
        out_shape=jax.ShapeDtypeStruct((N, D), F32),
        compiler_params=_params("arbitrary"),
        name="moe_combine",
    )(dest, acc, route, mod, final_norm.reshape(1, D), ys)


def _moe_final(xl, mod, gain, router, wg, wu, wd, final_norm, seq):
    N, D = xl.shape
    blk = MOE_BLOCK_ROWS
    route, counts = _router(xl, mod, gain, router, seq)
    counts = counts[0, :N_EXPERTS].astype(jnp.int32)
    padded = (counts + blk - 1) // blk * blk
    ends = jnp.cumsum(padded)
    starts = ends - padded
    nblk = -(-(2 * N) // blk) + N_EXPERTS
    blk_expert = jnp.minimum(jnp.searchsorted(ends, jnp.arange(nblk, dtype=jnp.int32) * blk, side='right'),
                             N_EXPERTS - 1).astype(jnp.int32)
    tbl = jnp.concatenate([blk_expert, (ends[-1:] // blk).astype(jnp.int32)])
    dest = jnp.concatenate([starts[route[:, e].astype(jnp.int32)] + route[:, r].astype(jnp.int32)
                            for e, r in ((R_E1, R_RANK1), (R_E2, R_RANK2))])
    xs = _dispatch(dest, xl, mod, gain, jnp.zeros((nblk * blk, D), F32), seq)
    ys = _experts(tbl, xs, wg, wu, wd, blk)
    return _combine(dest, xl, route, mod, final_norm, ys, seq)


def _rope_tables(seq):
    rows = seq // GRID_W
    row, col = jnp.meshgrid(jnp.arange(rows), jnp.arange(GRID_W), indexing='ij')
    axis = HEAD // 2
    inv = ROPE_THETA ** (-jnp.arange(0, axis, 2, dtype=F32) / axis)
    ang = jnp.concatenate([row.reshape(-1, 1).astype(F32) * inv, col.reshape(-1, 1).astype(F32) * inv], axis=-1)
    cos = jnp.repeat(jnp.cos(ang), 2, axis=-1)
    sin = jnp.repeat(jnp.sin(ang), 2, axis=-1) * jnp.tile(jnp.array([-1.0, 1.0], F32), HEAD // 2)
    return jnp.tile(cos, (1, GROUPS)), jnp.tile(sin, (1, GROUPS))


def kernel(x, c, ctx, c_ctx, ada_w, ada_b, norm1, norm2, w_in, shift_mu, rwkv_w0, rwkv_w2, rwkv_a0, rwkv_a2, rwkv_g2, rwkv_kk, rwkv_ka, rwkv_rk, lnx_w, lnx_b, q_norm, k_norm, w_pa, w_pb, w_o, ffn_wg, ffn_wu, ffn_wd, router, moe_wg, moe_wu, moe_wd, final_norm):
    B, T, D = x.shape
    CX = ctx.shape[1]
    depth = ada_w.shape[0]
    assert depth == 2 and ffn_wg.shape[0] == 1 and router.shape[0] == 1
    nc = B * CX
    bf = lambda w: w.astype(BF16)

    nrow = -(-(B + 1) // 8) * 8
    act = jnp.zeros((nrow, D), F32).at[:B].set(c).at[B].set(c_ctx)
    mod = _ada(act, ada_w, ada_b).reshape(depth, nrow, 6, D)
    mod = jnp.concatenate([mod, jnp.zeros((depth, nrow, 2, D), F32)], axis=2)

    cos, sin_signed = _rope_tables(T)
    xf = jnp.concatenate([ctx.reshape(nc, D), x.reshape(B * T, D)], axis=0)
    out = None
    for l in range(depth):
        last = l == depth - 1
        p_r, p_q, p_kv, p_g = _in_proj(xf, mod[l], norm1[l], bf(w_in[l]), nc, T, B)
        r, v, nkk, g, bonus, lw, keys, b = _rwkv_prep(
            p_r, shift_mu[l], rwkv_w0[l], rwkv_w2[l], rwkv_a0[l], rwkv_a2[l], rwkv_g2[l], rwkv_kk[l], rwkv_ka[l],
            rwkv_rk[l].reshape(-1), nc, CX, T)
        y_fwd, y_bwd = _wkv_scan_pairs(r, v, nkk, lw, keys, b, B, CX, T)
        k_att, v_att = _kv_prep(p_kv, k_norm[l], cos, sin_signed, nc, CX, T)
        y_att = _attention(p_q, k_att, v_att, q_norm[l], cos, sin_signed, B, nc, CX, T, latent=True)
        if not last:
            y_ctx = _attention(p_q, k_att, v_att, q_norm[l], cos, sin_signed, B, nc, CX, T, latent=False)
            y_att = jnp.concatenate([y_ctx, y_att], axis=0)
        xm = _merge(xf, mod[l], y_fwd, y_bwd, bonus, g, y_att, p_g, lnx_w[l], lnx_b[l], bf(w_pa[l]), bf(w_pb[l]), bf(w_o[l]),
                    nc, T, B, nc if last else 0)
        if not last:
            xf = _ffn(xm, mod[l], norm2[l], bf(ffn_wg[l // 2]), bf(ffn_wu[l // 2]), bf(ffn_wd[l // 2]), nc, T, B)
        else:
            out = _moe_final(xm, mod[l], norm2[l], router[l // 2], bf(moe_wg[l // 2]), bf(moe_wu[l // 2]),
                             bf(moe_wd[l // 2]), final_norm, T)
    return out.reshape(B, T, D)
```

```python
import functools
import math

import jax
import jax.numpy as jnp
from jax import lax
from jax.experimental import pallas as pl
from jax.experimental.pallas import tpu as pltpu

F32 = jnp.float32
BF16 = jnp.bfloat16

HEAD = 64
RWKV_HEADS = 8
RWKV_W = RWKV_HEADS * HEAD
LORA_W = 128
ATT_HEADS = 8
KV_HEADS = 2
GROUPS = ATT_HEADS // KV_HEADS
ATT_W = ATT_HEADS * HEAD
KV_W = KV_HEADS * HEAD
R_COLS = 3 * RWKV_W + 64 + 64 + 128
GRID_W = 64
ROPE_THETA = 10000.0
N_EXPERTS = 8
NORM_EPS = 1e-6
LNX_EPS = 64e-5
LANES = 128
NEG_BIG = -1e30
VMEM_LIMIT_BYTES = 56 * 1024 * 1024
SCAN_CHUNK = 64
FFN_TILE = 7 * LANES
SCAN_BATCHES_PER_STEP = 4
MOE_BLOCK_ROWS = 1024


def _params(*sem):
    return pltpu.CompilerParams(dimension_semantics=sem, vmem_limit_bytes=VMEM_LIMIT_BYTES)


def _tile(pref, *dims):
    t = pref
    while any(d % t for d in dims):
        t //= 2
    return t


def _ffn_tile(f):
    return FFN_TILE if f % FFN_TILE == 0 else _tile(512, f)


def _dot(a, b):
    return jnp.dot(a, b, preferred_element_type=F32)


def _split(x):
    hi = x.astype(BF16)
    lo = (x - hi.astype(F32)).astype(BF16)
    return hi, lo


def _dot3(a, b):
    ah, al = _split(a)
    bh, bl = _split(b)
    return _dot(ah, bh) + (_dot(ah, bl) + _dot(al, bh))


def _dot2(a, b_bf16):
    ah, al = _split(a)
    return _dot(ah, b_bf16) + _dot(al, b_bf16)


def _sigmoid(x):
    return 1.0 / (1.0 + jnp.exp(-x))


def _normmod(x, gain, scale, shift):
    ms = jnp.mean(x * x, axis=-1, keepdims=True)
    return x * lax.rsqrt(ms + NORM_EPS) * gain * (1.0 + scale) + shift


def _head_ones(width):
    r = lax.broadcasted_iota(jnp.int32, (width, width), 0) // HEAD
    c = lax.broadcasted_iota(jnp.int32, (width, width), 1) // HEAD
    return jnp.where(r == c, 1.0, 0.0).astype(BF16)


def _ada_kernel(a_ref, w_ref, b_ref, o_ref):
    a = a_ref[...]
    o_ref[0] = _dot3(a * _sigmoid(a), w_ref[0]) + b_ref[0]


def _ada(act, ada_w, ada_b):
    L, D, N6 = ada_w.shape
    R = act.shape[0]
    tn = _tile(1536, N6)
    return pl.pallas_call(
        _ada_kernel,
        grid=(L, N6 // tn),
        in_specs=[pl.BlockSpec((R, D), lambda l, j: (0, 0)),
                  pl.BlockSpec((1, D, tn), lambda l, j: (l, 0, j)),
                  pl.BlockSpec((1, 1, tn), lambda l, j: (l, 0, j))],
        out_specs=pl.BlockSpec((1, R, tn), lambda l, j: (l, 0, j)),
        out_shape=jax.ShapeDtypeStruct((L, R, N6), F32),
        compiler_params=_params("parallel", "parallel"),
        name="ada",
    )(act, ada_w, ada_b.reshape(L, 1, N6))


def _in_kernel(x_ref, mod_ref, g_ref, w_ref, pr_ref, pq_ref, pkv_ref, pg_ref):
    m = mod_ref[0]
    h = _normmod(x_ref[...], g_ref[...], m[1:2], m[0:1]).astype(BF16)
    col = 0
    for ref in (pr_ref, pq_ref, pkv_ref, pg_ref):
        n = ref.shape[-1]
        for c0 in range(0, n, 256):
            ref[:, c0:c0 + 256] = _dot(h, w_ref[:, col + c0:col + c0 + 256])
        col += n


def _mod_index(tm, nc, seq, nb):
    def index(i):
        start = i * tm
        return (jnp.where(start < nc, nb, (start - nc) // seq), 0, 0)
    return index


def _in_proj(xf, mod, gain, w_bf16, nc, seq, nb):
    N, D = xf.shape
    tm = _tile(512, nc, seq)
    widths = (R_COLS, ATT_W, 2 * KV_W, 2 * D)
    return pl.pallas_call(
        _in_kernel,
        grid=(N // tm,),
        in_specs=[pl.BlockSpec((tm, D), lambda i: (i, 0)),
                  pl.BlockSpec((1, 8, D), _mod_index(tm, nc, seq, nb)),
                  pl.BlockSpec((1, D), lambda i: (0, 0)),
                  pl.BlockSpec(w_bf16.shape, lambda i: (0, 0))],
        out_specs=[pl.BlockSpec((tm, w), lambda i: (i, 0)) for w in widths],
        out_shape=[jax.ShapeDtypeStruct((N, w), F32) for w in widths],
        compiler_params=_params("parallel"),
        name="in_proj",
    )(xf, mod, gain.reshape(1, D), w_bf16)


def _prep_kernel(p_ref, prev_ref, next_ref, mu_ref, w0_ref, w2_ref, a0_ref, a2_ref, g2_ref, kk_ref, ka_ref,
                 rk_ref, r_ref, v_ref, nkk_ref, g_ref, bonus_ref, lw_ref, keys_ref, b_ref, *, tt, nc, ctx, seq):
    i = pl.program_id(0)
    start = i * tt
    in_ctx = start < nc
    pos = jnp.where(in_ctx, start % ctx, (start - nc) % seq)
    seg = jnp.where(in_ctx, ctx, seq)
    first = pos == 0
    last = pos + tt == seg
    p = p_ref[...]
    rows = lax.broadcasted_iota(jnp.int32, (tt, 1), 0)
    prev_row = jnp.where(first, 0.0, prev_ref[7:8, :])
    next_row = jnp.where(last, 0.0, next_ref[0:1, :])
    prev = jnp.where(rows == 0, prev_row, pltpu.roll(p, 1, 0))
    nxt = jnp.where(rows == tt - 1, next_row, pltpu.roll(p, tt - 1, 0))
    ps = p + mu_ref[0:1, :] * (prev - p) + mu_ref[1:2, :] * (nxt - p)
    W = RWKV_W
    r = ps[:, 0:W]
    k = ps[:, W:2 * W]
    v = ps[:, 2 * W:3 * W]
    wa = ps[:, 3 * W:3 * W + LORA_W]
    gd = ps[:, 3 * W + LORA_W:]
    ones = _head_ones(W)
    kkf = k * kk_ref[...]
    nrm = jnp.sqrt(_dot2(kkf * kkf, ones))
    kk = kkf / jnp.maximum(nrm, 1e-12)
    tw = jnp.tanh(wa)
    ksum = jnp.zeros_like(k)
    for d in range(2):
        z = w0_ref[d:d + 1, :] + _dot3(tw, w2_ref[d])
        lw_ref[d] = -math.exp(-0.5) * _sigmoid(z)
        a = _sigmoid(a0_ref[d:d + 1, :] + _dot3(wa, a2_ref[d]))
        keys = k * (1.0 + (a - 1.0) * ka_ref[...])
        keys_ref[d] = keys
        b_ref[d] = kk * a
        ksum = ksum + keys
    r_ref[...] = r
    v_ref[...] = v
    nkk_ref[...] = -kk
    g_ref[...] = _dot3(_sigmoid(gd), g2_ref[...])
    bonus_ref[...] = _dot2(r * (0.5 * ksum) * rk_ref[...], ones) * v


def _rwkv_prep(p_r, mu, w0, w2, a0, a2, g2, k_k, k_a, r_k, nc, ctx, seq):
    N = p_r.shape[0]
    W = RWKV_W
    tt = _tile(256, ctx, seq)
    nblk8 = N // 8
    zeros = jnp.zeros((2, 64, W), F32)
    w2p = jnp.concatenate([w2, zeros], axis=1)
    a2p = jnp.concatenate([zeros, a2], axis=1)
    row = lambda t: t.reshape(1, W)
    full = lambda a: pl.BlockSpec(a.shape, lambda i: (0,) * a.ndim)
    consts = (mu, w0, w2p, a0, a2p, g2, row(k_k), row(k_a), row(r_k))
    one = jax.ShapeDtypeStruct((N, W), F32)
    two = jax.ShapeDtypeStruct((2, N, W), F32)
    s1 = pl.BlockSpec((tt, W), lambda i: (i, 0))
    s2 = pl.BlockSpec((2, tt, W), lambda i: (0, i, 0))
    return pl.pallas_call(
        functools.partial(_prep_kernel, tt=tt, nc=nc, ctx=ctx, seq=seq),
        grid=(N // tt,),
        in_specs=[pl.BlockSpec((tt, R_COLS), lambda i: (i, 0)),
                  pl.BlockSpec((8, R_COLS), lambda i: (jnp.maximum(i * (tt // 8) - 1, 0), 0)),
                  pl.BlockSpec((8, R_COLS), lambda i: (jnp.minimum((i + 1) * (tt // 8), nblk8 - 1), 0))]
                 + [full(a) for a in consts],
        out_specs=[s1, s1, s1, s1, s1, s2, s2, s2],
        out_shape=[one, one, one, one, one, two, two, two],
        compiler_params=_params("parallel"),
        name="rwkv_prep",
    )(p_r, p_r, p_r, *consts)


def _pair_blocks(x):
    left = lax.broadcasted_iota(jnp.int32, x.shape, 1) < HEAD
    zero = jnp.zeros_like(x)
    return jnp.concatenate([jnp.where(left, x, zero), jnp.where(left, zero, x)], axis=0)


def _pair_diag(x):
    left = lax.broadcasted_iota(jnp.int32, (HEAD, LANES), 1) < HEAD
    return jnp.where(left, x[:HEAD], x[HEAD:])


def _scan_pair_kernel(*refs, C, direction, nslot):
    assert C == HEAD
    y_ref, st_ref = refs[6 * nslot], refs[6 * nslot + 1]

    @pl.when(pl.program_id(1) == 0)
    def _():
        st_ref[...] = jnp.zeros_like(st_ref)

    npair = RWKV_HEADS // 2
    row = lax.broadcasted_iota(jnp.int32, (C, C), 0)
    col = lax.broadcasted_iota(jnp.int32, (C, C), 1)
    before = row - col if direction == 0 else col - row
    linc = jnp.where(before >= 0, 1.0, 0.0).astype(BF16)
    row2 = lax.broadcasted_iota(jnp.int32, (C, LANES), 0)
    col2 = lax.broadcasted_iota(jnp.int32, (C, LANES), 1) % C
    before2 = row2 - col2 if direction == 0 else col2 - row2
    strict = before2 > 0
    incl = before2 >= 0
    eye2 = row2 == col2
    eye2b = jnp.where(eye2, 1.0, 0.0).astype(BF16)
    nt = (((1,), (1,)), ((), ()))
    tn = (((0,), (0,)), ((), ()))
    nlev = int(math.log2(C))

    at, rt, bt, kt, bh, kh, vb, g_tot = ([] for _ in range(8))
    for s in range(nslot):
        r_ref, v_ref, a_ref, lw_ref, k_ref, b_ref = refs[6 * s:6 * s + 6]
        lw = lw_ref[0]
        lh, ll = _split(lw)
        lam = _dot(linc, lh) + _dot(linc, ll)
        tot = jnp.sum(lw, axis=0, keepdims=True)
        g_rem = jnp.exp(tot - lam)
        g_inv = jnp.exp(-lam)
        gt = jnp.exp(tot)
        k = k_ref[0]
        b = b_ref[0]
        a_s = a_ref[...] * jnp.exp(lam - lw)
        r_s = r_ref[...] * jnp.exp(lam)
        b_t = (b * g_inv).astype(BF16)
        k_t = (k * g_inv).astype(BF16)
        b_h = (b * g_rem).astype(BF16)
        k_h = (k * g_rem).astype(BF16)
        v_b = v_ref[...].astype(BF16)
        for p in range(npair):
            sl = slice(p * LANES, (p + 1) * LANES)
            for dst, src in ((at, a_s), (rt, r_s), (bt, b_t), (kt, k_t), (bh, b_h), (kh, k_h), (vb, v_b), (g_tot, gt)):
                dst.append(src[:, sl])
    nch = nslot * npair

    ar = [jnp.concatenate([at[c].astype(BF16), rt[c].astype(BF16)], axis=0) for c in range(nch)]
    L = LANES
    xbk = [lax.dot_general(ar[c], jnp.concatenate([_pair_blocks(bt[c]), _pair_blocks(kt[c])], axis=0), nt,
                           preferred_element_type=F32) for c in range(nch)]
    s_pow = [jnp.where(strict, x[:C, :L], 0.0) for x in xbk]
    m_rb = [jnp.where(incl, x[C:, :L], 0.0).astype(BF16) for x in xbk]
    q = [jnp.where(strict, x[:C, L:], 0.0) for x in xbk]
    m_rk = [jnp.where(incl, x[C:, L:], 0.0) for x in xbk]
    ah = list(at)
    for lev in range(nlev):
        for c in range(nch):
            sb = s_pow[c].astype(BF16)
            rhs = [_pair_blocks(ah[c].astype(BF16)), _pair_blocks(q[c].astype(BF16))]
            if lev < nlev - 1:
                rhs.append(_pair_blocks(sb))
            prod = _dot(sb, jnp.concatenate(rhs, axis=1))
            ah[c] = ah[c] + prod[:, :L]
            q[c] = q[c] + prod[:, L:2 * L]
            if lev < nlev - 1:
                s_pow[c] = prod[:, 2 * L:]
    ahb = [x.astype(BF16) for x in ah]
    qb = [x.astype(BF16) for x in q]
    raq = [_dot(m_rb[c], jnp.concatenate([_pair_blocks(ahb[c]), _pair_blocks(qb[c])], axis=1))
           for c in range(nch)]
    baq = [lax.dot_general(bh[c], jnp.concatenate([ahb[c], qb[c]], axis=1), tn, preferred_element_type=F32)
           for c in range(nch)]
    k_tr = [_pair_diag(lax.dot_general(kh[c], eye2b, tn, preferred_element_type=F32)) for c in range(nch)]
    for c in range(nch):
        s, p = divmod(c, npair)
        r_hat = rt[c] + raq[c][:, :L]
        g_mat = jnp.where(eye2, g_tot[c], 0.0) + _pair_diag(baq[c][:, :L])
        v2 = _pair_blocks(vb[c])
        y_in = _dot((m_rk[c] + raq[c][:, L:]).astype(BF16), v2)
        h_mat = _dot((_pair_diag(baq[c][:, L:]) + k_tr[c]).astype(BF16), v2)
        sh, sl_ = _split(st_ref[c])
        sh, sl_ = _pair_blocks(sh), _pair_blocks(sl_)
        gh, gl = _split(g_mat)
        gs = _dot(gh, jnp.concatenate([sh, sl_], axis=1))
        st_ref[c] = gs[:, :L] + (gs[:, L:] + _dot(gl, sh)) + h_mat
        y_ref[s, :, p * LANES:(p + 1) * LANES] = _dot(r_hat.astype(BF16), sh) + y_in


def _wkv_scan_pairs(r, v, nkk, lw, keys, b, nb, ctx, seq):
    N, W = r.shape
    C = SCAN_CHUNK
    ncc, nlc = ctx // C, seq // C
    nslot = math.gcd(SCAN_BATCHES_PER_STEP, nb)
    hb = nb // nslot

    def blk(d, j, bi, nbat):
        jc = j if d == 0 else ncc - 1 - j
        jl = j - ncc if d == 0 else nlc - 1 - (j - ncc)
        return jnp.where(j < ncc, bi * ncc + jc, nbat * ncc + bi * nlc + jl)

    def one(d):
        s1 = lambda s: pl.BlockSpec((C, W), lambda i, j: (blk(d, j, s * hb + i, nb), 0))
        s2 = lambda s: pl.BlockSpec((1, C, W), lambda i, j: (d, blk(d, j, s * hb + i, nb), 0))
        in_specs = [spec for s in range(nslot) for spec in (s1(s), s1(s), s1(s), s2(s), s2(s), s2(s))]
        return pl.pallas_call(
            functools.partial(_scan_pair_kernel, C=C, direction=d, nslot=nslot),
            grid=(hb, ncc + nlc),
            in_specs=in_specs,
            out_specs=pl.BlockSpec((nslot, C, W), lambda i, j: (0, blk(d, j, i, hb), 0)),
            out_shape=jax.ShapeDtypeStruct((nslot, N // nslot, W), F32),
            scratch_shapes=[pltpu.VMEM((nslot * RWKV_HEADS // 2, HEAD, LANES), F32)],
            compiler_params=_params("parallel", "arbitrary"),
            name="wkv_scan",
        )(*([r, v, nkk, lw, keys, b] * nslot))

    return one(0), one(1)


def _slot_rows(tm, nc, nl, nslot):
    def index(i):
        start = i * tm
        in_ctx = start < nc
        rel = jnp.where(in_ctx, start, start - nc)
        per = jnp.where(in_ctx, nc // nslot, nl // nslot)
        local = jnp.where(in_ctx, 0, nc // nslot) + rel % per
        return (rel // per, local // tm, 0)
    return index


def _rope(t, cos, sin_signed):
    w = t.shape[-1]
    lane = lax.broadcasted_iota(jnp.int32, t.shape, 1)
    swapped = jnp.where(lane % 2 == 0, pltpu.roll(t, w - 1, 1), pltpu.roll(t, 1, 1))
    return t * cos + swapped * sin_signed


def _head_rms(t, gain):
    ms = _dot2(t * t, _head_ones(t.shape[-1])) * (1.0 / HEAD)
    return t * lax.rsqrt(ms + NORM_EPS) * gain


def _kv_kernel(p_ref, kn_ref, cos_ref, sin_ref, k_ref, v_ref, *, tt, nc):
    p = p_ref[...]
    kn = _head_rms(p[:, :KV_W], kn_ref[...])
    kr = _rope(kn, cos_ref[...], sin_ref[...])
    k = jnp.where(pl.program_id(0) * tt < nc, kn, kr).astype(BF16)
    v = p[:, KV_W:].astype(BF16)
    one_col = jnp.where(lax.broadcasted_iota(jnp.int32, (tt, HEAD), 1) == 0, 1.0, 0.0).astype(BF16)
    for h in range(KV_HEADS):
        k_ref[h] = k[:, h * HEAD:(h + 1) * HEAD]
        v_ref[h] = jnp.concatenate([v[:, h * HEAD:(h + 1) * HEAD], one_col], axis=1)


def _kv_prep(p_kv, k_norm, cos, sin_signed, nc, ctx, seq):
    N = p_kv.shape[0]
    tt = _tile(512, ctx, seq)
    npos = seq // tt
    pos = lambda i: ((jnp.maximum(i * tt - nc, 0) // tt) % npos, 0)
    spec = lambda w: pl.BlockSpec((KV_HEADS, tt, w), lambda i: (0, i, 0))
    return pl.pallas_call(
        functools.partial(_kv_kernel, tt=tt, nc=nc),
        grid=(N // tt,),
        in_specs=[pl.BlockSpec((tt, 2 * KV_W), lambda i: (i, 0)),
                  pl.BlockSpec((1, KV_W), lambda i: (0, 0)),
                  pl.BlockSpec((tt, KV_W), pos),
                  pl.BlockSpec((tt, KV_W), pos)],
        out_specs=[spec(HEAD), spec(2 * HEAD)],
        out_shape=[jax.ShapeDtypeStruct((KV_HEADS, N, w), BF16) for w in (HEAD, 2 * HEAD)],
        compiler_params=_params("parallel"),
        name="kv_prep",
    )(p_kv, jnp.tile(k_norm, KV_HEADS).reshape(1, KV_W), cos[:, :KV_W], sin_signed[:, :KV_W])


def _attn_kernel(*refs, nseg, rope):
    q_ref, qn_ref = refs[0], refs[1]
    pos = 2
    if rope:
        cos_ref, sin_ref = refs[2], refs[3]
        pos = 4
    kv_refs = refs[pos:pos + 2 * nseg]
    o_ref = refs[pos + 2 * nseg]
    q = _head_rms(q_ref[...], qn_ref[...])
    if rope:
        q = _rope(q, cos_ref[...], sin_ref[...])
    q = (q * (HEAD ** -0.5 * math.log2(math.e))).astype(BF16)
    outs = []
    for g in range(GROUPS):
        qh = q[:, g * HEAD:(g + 1) * HEAD]
        s = [lax.dot_general(qh, kv_refs[2 * i][0], (((1,), (1,)), ((), ())), preferred_element_type=F32)
             for i in range(nseg)]
        m = s[0].max(axis=-1, keepdims=True)
        for si in s[1:]:
            m = jnp.maximum(m, si.max(axis=-1, keepdims=True))
        acc = None
        for i in range(nseg):
            pv = _dot(jnp.exp2(s[i] - m).astype(BF16), kv_refs[2 * i + 1][0])
            acc = pv if acc is None else acc + pv
        outs.append(acc[:, :HEAD] / acc[:, HEAD:HEAD + 1])
    o_ref[...] = jnp.concatenate(outs, axis=1)


def _attention(p_q, k, v, q_norm, cos, sin_signed, nb, nc, ctx, seq, latent):
    qlen = seq if latent else ctx
    tq = _tile(256, qlen)
    nqt = qlen // tq
    qw = GROUPS * HEAD
    qoff = nc // tq if latent else 0
    qmap = lambda bi, g, t: (qoff + bi * nqt + t, g)
    in_specs = [pl.BlockSpec((tq, qw), qmap), pl.BlockSpec((1, qw), lambda bi, g, t: (0, 0))]
    args = [p_q, jnp.tile(q_norm, GROUPS).reshape(1, qw)]
    ctx_spec = lambda a: pl.BlockSpec((1, ctx, a.shape[-1]), lambda bi, g, t: (g, bi, 0))
    if latent:
        in_specs += [pl.BlockSpec((tq, qw), lambda bi, g, t: (t, 0))] * 2
        args += [cos, sin_signed]
        lat_spec = lambda a: pl.BlockSpec((1, seq, a.shape[-1]), lambda bi, g, t: (g, nc // seq + bi, 0))
        in_specs += [lat_spec(k), lat_spec(v), ctx_spec(k), ctx_spec(v)]
        args += [k, v, k, v]
    else:
        in_specs += [ctx_spec(k), ctx_spec(v)]
        args += [k, v]
    return pl.pallas_call(
        functools.partial(_attn_kernel, nseg=2 if latent else 1, rope=latent),
        grid=(nb, KV_HEADS, nqt),
        in_specs=in_specs,
        out_specs=pl.BlockSpec((tq, qw), lambda bi, g, t: (bi * nqt + t, g)),
        out_shape=jax.ShapeDtypeStruct((nb * qlen, ATT_W), F32),
        compiler_params=_params("parallel", "parallel", "parallel"),
        name="attn_lat" if latent else "attn_ctx",
    )(*args)


def _merge_kernel(x_ref, mod_ref, yf_ref, yb_ref, bonus_ref, g_ref, ya_ref, pg_ref, lw_ref, lb_ref, wpa_ref, wpb_ref,
                  wo_ref, o_ref):
    D = x_ref.shape[-1]
    ones = _head_ones(RWKV_W)
    y = yf_ref[0] + yb_ref[0]
    mean = _dot2(y, ones) * (1.0 / HEAD)
    yc = y - mean
    var = _dot2(yc * yc, ones) * (1.0 / HEAD)
    yn = yc * lax.rsqrt(var + LNX_EPS) * lw_ref[...] + lb_ref[...]
    y_rwkv = (yn + bonus_ref[...]) * g_ref[...]
    pg = pg_ref[...]
    mix = (_sigmoid(pg[:, :D]) * _dot(y_rwkv.astype(BF16), wpa_ref[...])
           + _sigmoid(pg[:, D:]) * _dot(ya_ref[...].astype(BF16), wpb_ref[...]))
    o_ref[...] = x_ref[...] + mod_ref[0][2:3] * _dot(mix.astype(BF16), wo_ref[...])


def _merge(xf, mod, y_fwd, y_bwd, bonus, g, y_att, p_g, lnx_w, lnx_b, w_pa, w_pb, w_o, nc, seq, nb, row0):
    N, D = xf.shape
    W = RWKV_W
    tm = _tile(512, nc // y_fwd.shape[0], seq)
    off = row0 // tm
    n_out = N - row0
    mod_idx = _mod_index(tm, nc, seq, nb)
    rows = lambda w: pl.BlockSpec((tm, w), lambda i: (i + off, 0))
    slot_index = _slot_rows(tm, nc, N - nc, y_fwd.shape[0])
    scan_rows = pl.BlockSpec((1, tm, W), lambda i: slot_index(i + off))
    full = lambda a: pl.BlockSpec(a.shape, lambda i: (0,) * a.ndim)
    consts = (lnx_w.reshape(1, W), lnx_b.reshape(1, W), w_pa, w_pb, w_o)
    return pl.pallas_call(
        _merge_kernel,
        grid=(n_out // tm,),
        in_specs=[rows(D), pl.BlockSpec((1, 8, D), lambda i: mod_idx(i + off)),
                  scan_rows, scan_rows, rows(W), rows(W), pl.BlockSpec((tm, W), lambda i: (i, 0)), rows(2 * D)]
                 + [full(a) for a in consts],
        out_specs=pl.BlockSpec((tm, D), lambda i: (i, 0)),
        out_shape=jax.ShapeDtypeStruct((n_out, D), F32),
        compiler_params=_params("parallel"),
        name="merge",
    )(xf, mod, y_fwd, y_bwd, bonus, g, y_att, p_g, *consts)


def _ffn_kernel(x_ref, mod_ref, g_ref, wg_ref, wu_ref, wd_ref, o_ref, h_ref, acc_ref):
    f = pl.program_id(1)

    @pl.when(f == 0)
    def _():
        m = mod_ref[0]
        h_ref[...] = _normmod(x_ref[...], g_ref[...], m[4:5], m[3:4]).astype(BF16)
        acc_ref[...] = jnp.zeros_like(acc_ref)

    h = h_ref[...]
    a = _dot(h, wg_ref[...])
    z = a * _sigmoid(a) * _dot(h, wu_ref[...])
    acc_ref[...] += _dot(z.astype(BF16), wd_ref[...])

    @pl.when(f == pl.num_programs(1) - 1)
    def _():
        o_ref[...] = x_ref[...] + mod_ref[0][5:6] * acc_ref[...]


def _ffn(xf, mod, gain, wg, wu, wd, nc, seq, nb):
    N, D = xf.shape
    F = wg.shape[-1]
    tm = _tile(1024, nc, seq)
    tf = _ffn_tile(F)
    return pl.pallas_call(
        _ffn_kernel,
        grid=(N // tm, F // tf),
        in_specs=[pl.BlockSpec((tm, D), lambda i, f: (i, 0)),
                  pl.BlockSpec((1, 8, D), lambda i, f: _mod_index(tm, nc, seq, nb)(i)),
                  pl.BlockSpec((1, D), lambda i, f: (0, 0)),
                  pl.BlockSpec((D, tf), lambda i, f: (0, f)),
                  pl.BlockSpec((D, tf), lambda i, f: (0, f)),
                  pl.BlockSpec((tf, D), lambda i, f: (f, 0))],
        out_specs=pl.BlockSpec((tm, D), lambda i, f: (i, 0)),
        out_shape=jax.ShapeDtypeStruct((N, D), F32),
        scratch_shapes=[pltpu.VMEM((tm, D), BF16), pltpu.VMEM((tm, D), F32)],
        compiler_params=_params("parallel", "arbitrary"),
        name="ffn",
    )(xf, mod, gain.reshape(1, D), wg, wu, wd)


R_E1, R_E2, R_RANK1, R_RANK2, R_W1, R_W2 = range(6)


def _router_kernel(x_ref, mod_ref, g_ref, wr_ref, route_ref, cnt_ref, carry_ref):
    @pl.when(pl.program_id(0) == 0)
    def _():
        carry_ref[...] = jnp.zeros_like(carry_ref)

    m = mod_ref[0]
    h = _normmod(x_ref[...], g_ref[...], m[4:5], m[3:4])
    tm = h.shape[0]
    lane = lax.broadcasted_iota(jnp.int32, (tm, LANES), 1).astype(F32)
    logits = jnp.where(lane < N_EXPERTS, _dot3(h, wr_ref[...]), NEG_BIG)
    m1 = logits.max(axis=-1, keepdims=True)
    i1 = jnp.where(logits == m1, lane, float(LANES)).min(axis=-1, keepdims=True)
    rest = jnp.where(lane == i1, NEG_BIG, logits)
    m2 = rest.max(axis=-1, keepdims=True)
    i2 = jnp.where(rest == m2, lane, float(LANES)).min(axis=-1, keepdims=True)
    e2 = jnp.exp(m2 - m1)
    w1 = 1.0 / (1.0 + e2)
    w2 = e2 / (1.0 + e2)
    onehot = jnp.where(lane == i1, 1.0, jnp.where(lane == i2, 1.0, 0.0))
    earlier = (lax.broadcasted_iota(jnp.int32, (tm, tm), 0) > lax.broadcasted_iota(jnp.int32, (tm, tm), 1))
    before = _dot(jnp.where(earlier, 1.0, 0.0).astype(BF16), onehot.astype(BF16)) + carry_ref[...]
    rank1 = jnp.where(lane == i1, before, 0.0).sum(axis=-1, keepdims=True)
    rank2 = jnp.where(lane == i2, before, 0.0).sum(axis=-1, keepdims=True)
    carry_ref[...] += onehot.sum(axis=0, keepdims=True)
    cnt_ref[...] = jnp.broadcast_to(carry_ref[...], cnt_ref.shape)
    rec = jnp.zeros_like(lane)
    for idx, val in ((R_E1, i1), (R_E2, i2), (R_RANK1, rank1), (R_RANK2, rank2), (R_W1, w1), (R_W2, w2)):
        rec = jnp.where(lane == float(idx), val, rec)
    route_ref[...] = rec


def _router(xl, mod, gain, router, seq):
    N, D = xl.shape
    tm = _tile(512, seq)
    wr = jnp.zeros((D, LANES), F32).at[:, :N_EXPERTS].set(router)
    return pl.pallas_call(
        _router_kernel,
        grid=(N // tm,),
        in_specs=[pl.BlockSpec((tm, D), lambda i: (i, 0)),
                  pl.BlockSpec((1, 8, D), lambda i: ((i * tm) // seq, 0, 0)),
                  pl.BlockSpec((1, D), lambda i: (0, 0)),
                  pl.BlockSpec((D, LANES), lambda i: (0, 0))],
        out_specs=[pl.BlockSpec((tm, LANES), lambda i: (i, 0)), pl.BlockSpec((8, LANES), lambda i: (0, 0))],
        out_shape=[jax.ShapeDtypeStruct((N, LANES), F32), jax.ShapeDtypeStruct((8, LANES), F32)],
        scratch_shapes=[pltpu.VMEM((1, LANES), F32)],
        compiler_params=_params("arbitrary"),
        name="router",
    )(xl, mod, gain.reshape(1, D), wr)


def _all_rows(n, copies):
    def start(r, c):
        for copy in copies:
            copy(r).start()
        return c

    def wait(r, c):
        for copy in copies:
            copy(r).wait()
        return c

    lax.fori_loop(0, n, start, 0, unroll=8)
    lax.fori_loop(0, n, wait, 0, unroll=8)


def _dispatch_kernel(dest_ref, x_ref, mod_ref, g_ref, xs_in_ref, xs_ref, h_ref, sem):
    del xs_in_ref
    tm = x_ref.shape[0]
    n = tm * pl.num_programs(0)
    base = pl.program_id(0) * tm
    m = mod_ref[0]
    h_ref[...] = _normmod(x_ref[...], g_ref[...], m[4:5], m[3:4])
    _all_rows(tm, [lambda r, off=off: pltpu.make_async_copy(
        h_ref.at[pl.ds(r, 1)], xs_ref.at[pl.ds(dest_ref[off + base + r], 1)], sem) for off in (0, n)])


def _dispatch(dest, xl, mod, gain, xs, seq):
    N, D = xl.shape
    tm = _tile(512, seq)
    return pl.pallas_call(
        _dispatch_kernel,
        grid_spec=pltpu.PrefetchScalarGridSpec(
            num_scalar_prefetch=1,
            grid=(N // tm,),
            in_specs=[pl.BlockSpec((tm, D), lambda i, dest: (i, 0)),
                      pl.BlockSpec((1, 8, D), lambda i, dest: ((i * tm) // seq, 0, 0)),
                      pl.BlockSpec((1, D), lambda i, dest: (0, 0)),
                      pl.BlockSpec(memory_space=pl.ANY)],
            out_specs=pl.BlockSpec(memory_space=pl.ANY),
            scratch_shapes=[pltpu.VMEM((tm, D), F32), pltpu.SemaphoreType.DMA]),
        out_shape=jax.ShapeDtypeStruct(xs.shape, xs.dtype),
        input_output_aliases={4: 0},
        compiler_params=_params("arbitrary"),
        name="moe_dispatch",
    )(dest, xl, mod, gain.reshape(1, D), xs)


def _expert_kernel(tbl_ref, xs_ref, wg_ref, wu_ref, wd_ref, ys_ref, hb_ref, acc_ref, *, nblk):
    i = pl.program_id(0)
    f = pl.program_id(1)
    used = i < tbl_ref[nblk]
    last = f == pl.num_programs(1) - 1

    @pl.when(used & (f == 0))
    def _():
        hb_ref[...] = xs_ref[...].astype(BF16)
        acc_ref[...] = jnp.zeros_like(acc_ref)

    @pl.when(used)
    def _():
        h = hb_ref[...]
        a = _dot(h, wg_ref[0])
        z = a * _sigmoid(a) * _dot(h, wu_ref[0])
        acc_ref[...] += _dot(z.astype(BF16), wd_ref[0])

    @pl.when(used & last)
    def _():
        ys_ref[...] = acc_ref[...]

    @pl.when(jnp.logical_not(used) & last)
    def _():
        ys_ref[...] = jnp.zeros_like(ys_ref)


def _experts(tbl, xs, wg, wu, wd, blk):
    R, D = xs.shape
    F = wg.shape[-1]
    nblk = R // blk
    tf = _ffn_tile(F)
    return pl.pallas_call(
        functools.partial(_expert_kernel, nblk=nblk),
        grid_spec=pltpu.PrefetchScalarGridSpec(
            num_scalar_prefetch=1,
            grid=(nblk, F // tf),
            in_specs=[pl.BlockSpec((blk, D), lambda i, f, tbl: (i, 0)),
                      pl.BlockSpec((1, D, tf), lambda i, f, tbl: (tbl[i], 0, f)),
                      pl.BlockSpec((1, D, tf), lambda i, f, tbl: (tbl[i], 0, f)),
                      pl.BlockSpec((1, tf, D), lambda i, f, tbl: (tbl[i], f, 0))],
            out_specs=pl.BlockSpec((blk, D), lambda i, f, tbl: (i, 0)),
            scratch_shapes=[pltpu.VMEM((blk, D), BF16), pltpu.VMEM((blk, D), F32)]),
        out_shape=jax.ShapeDtypeStruct((R, D), F32),
        compiler_params=_params("parallel", "arbitrary"),
        name="moe_experts",
    )(tbl, xs, wg, wu, wd)


def _combine_kernel(dest_ref, x_ref, route_ref, mod_ref, fn_ref, ys_ref, o_ref, buf_ref, sem):
    tm = x_ref.shape[0]
    n = tm * pl.num_programs(0)
    base = pl.program_id(0) * tm
    _all_rows(tm, [lambda r, slot=slot: pltpu.make_async_copy(
        ys_ref.at[pl.ds(dest_ref[slot * n + base + r], 1)], buf_ref.at[pl.ds(slot * tm + r, 1)], sem)
        for slot in (0, 1)])
    mix = route_ref[:, R_W1:R_W1 + 1] * buf_ref[:tm] + route_ref[:, R_W2:R_W2 + 1] * buf_ref[tm:]
    y = x_ref[...] + mod_ref[0][5:6] * mix
    ms = jnp.mean(y * y, axis=-1, keepdims=True)
    o_ref[...] = y * lax.rsqrt(ms + NORM_EPS) * fn_ref[...]


def _combine(dest, acc, route, mod, final_norm, ys, seq):
    N, D = acc.shape
    tm = _tile(512, seq)
    return pl.pallas_call(
        _combine_kernel,
        grid_spec=pltpu.PrefetchScalarGridSpec(
            num_scalar_prefetch=1,
            grid=(N // tm,),
            in_specs=[pl.BlockSpec((tm, D), lambda i, dest: (i, 0)),
                      pl.BlockSpec((tm, LANES), lambda i, dest: (i, 0)),
                      pl.BlockSpec((1, 8, D), lambda i, dest: ((i * tm) // seq, 0, 0)),
                      pl.BlockSpec((1, D), lambda i, dest: (0, 0)),
                      pl.BlockSpec(memory_space=pl.ANY)],
            out_specs=pl.BlockSpec((tm, D), lambda i, dest: (i, 0)),
            scratch_shapes=[pltpu.VMEM((2 * tm, D), F32), pltpu.SemaphoreType.DMA]),
        out_shape=jax.ShapeDtypeStruct((N, D), F32),
        compiler_params=_params("arbitrary"),
        name="moe_combine",
    )(dest, acc, route, mod, final_norm.reshape(1, D), ys)


def _moe_final(xl, mod, gain, router, wg, wu, wd, final_norm, seq):
    N, D = xl.shape
    blk = MOE_BLOCK_ROWS
    route, counts = _router(xl, mod, gain, router, seq)
    counts = counts[0, :N_EXPERTS].astype(jnp.int32)
    padded = (counts + blk - 1) // blk * blk
    ends = jnp.cumsum(padded)
    starts = ends - padded
    nblk = -(-(2 * N) // blk) + N_EXPERTS
    blk_expert = jnp.minimum(jnp.searchsorted(ends, jnp.arange(nblk, dtype=jnp.int32) * blk, side='right'),
                             N_EXPERTS - 1).astype(jnp.int32)
    tbl = jnp.concatenate([blk_expert, (ends[-1:] // blk).astype(jnp.int32)])
    dest = jnp.concatenate([starts[route[:, e].astype(jnp.int32)] + route[:, r].astype(jnp.int32)
                            for e, r in ((R_E1, R_RANK1), (R_E2, R_RANK2))])
    xs = _dispatch(dest, xl, mod, gain, jnp.zeros((nblk * blk, D), F32), seq)
    ys = _experts(tbl, xs, wg, wu, wd, blk)
    return _combine(dest, xl, route, mod, final_norm, ys, seq)


def _rope_tables(seq):
    rows = seq // GRID_W
    row, col = jnp.meshgrid(jnp.arange(rows), jnp.arange(GRID_W), indexing='ij')
    axis = HEAD // 2
    inv = ROPE_THETA ** (-jnp.arange(0, axis, 2, dtype=F32) / axis)
    ang = jnp.concatenate([row.reshape(-1, 1).astype(F32) * inv, col.reshape(-1, 1).astype(F32) * inv], axis=-1)
    cos = jnp.repeat(jnp.cos(ang), 2, axis=-1)
    sin = jnp.repeat(jnp.sin(ang), 2, axis=-1) * jnp.tile(jnp.array([-1.0, 1.0], F32), HEAD // 2)
    return jnp.tile(cos, (1, GROUPS)), jnp.tile(sin, (1, GROUPS))


def kernel(x, c, ctx, c_ctx, ada_w, ada_b, norm1, norm2, w_in, shift_mu, rwkv_w0, rwkv_w2, rwkv_a0, rwkv_a2, rwkv_g2, rwkv_kk, rwkv_ka, rwkv_rk, lnx_w, lnx_b, q_norm, k_norm, w_pa, w_pb, w_o, ffn_wg, ffn_wu, ffn_wd, router, moe_wg, moe_wu, moe_wd, final_norm):
    B, T, D = x.shape
    CX = ctx.shape[1]
    depth = ada_w.shape[0]
    assert depth == 2 and ffn_wg.shape[0] == 1 and router.shape[0] == 1
    nc = B * CX
    bf = lambda w: w.astype(BF16)

    nrow = -(-(B + 1) // 8) * 8
    act = jnp.zeros((nrow, D), F32).at[:B].set(c).at[B].set(c_ctx)
    mod = _ada(act, ada_w, ada_b).reshape(depth, nrow, 6, D)
    mod = jnp.concatenate([mod, jnp.zeros((depth, nrow, 2, D), F32)], axis=2)

    cos, sin_signed = _rope_tables(T)
    xf = jnp.concatenate([ctx.reshape(nc, D), x.reshape(B * T, D)], axis=0)
    out = None
    for l in range(depth):
        last = l == depth - 1
        p_r, p_q, p_kv, p_g = _in_proj(xf, mod[l], norm1[l], bf(w_in[l]), nc, T, B)
        r, v, nkk, g, bonus, lw, keys, b = _rwkv_prep(
            p_r, shift_mu[l], rwkv_w0[l], rwkv_w2[l], rwkv_a0[l], rwkv_a2[l], rwkv_g2[l], rwkv_kk[l], rwkv_ka[l],
            rwkv_rk[l].reshape(-1), nc, CX, T)
        y_fwd, y_bwd = _wkv_scan_pairs(r, v, nkk, lw, keys, b, B, CX, T)
        k_att, v_att = _kv_prep(p_kv, k_norm[l], cos, sin_signed, nc, CX, T)
        y_att = _attention(p_q, k_att, v_att, q_norm[l], cos, sin_signed, B, nc, CX, T, latent=True)
        if not last:
            y_ctx = _attention(p_q, k_att, v_att, q_norm[l], cos, sin_signed, B, nc, CX, T, latent=False)
            y_att = jnp.concatenate([y_ctx, y_att], axis=0)
        xm = _merge(xf, mod[l], y_fwd, y_bwd, bonus, g, y_att, p_g, lnx_w[l], lnx_b[l], bf(w_pa[l]), bf(w_pb[l]), bf(w_o[l]),
                    nc, T, B, nc if last else 0)
        if not last:
            xf = _ffn(xm, mod[l], norm2[l], bf(ffn_wg[l // 2]), bf(ffn_wu[l // 2]), bf(ffn_wd[l // 2]), nc, T, B)
        else:
            out = _moe_final(xm, mod[l], norm2[l], router[l // 2], bf(moe_wg[l // 2]), bf(moe_wu[l // 2]),
                             bf(moe_wd[l // 2]), final_norm, T)
    return out.reshape(B, T, D)
```

```python
import functools
import math

import jax
import jax.numpy as jnp
from jax import lax
from jax.experimental import pallas as pl
from jax.experimental.pallas import tpu as pltpu

F32 = jnp.float32
BF16 = jnp.bfloat16

HEAD = 64
RWKV_HEADS = 8
RWKV_W = RWKV_HEADS * HEAD
LORA_W = 128
ATT_HEADS = 8
KV_HEADS = 2
GROUPS = ATT_HEADS // KV_HEADS
ATT_W = ATT_HEADS * HEAD
KV_W = KV_HEADS * HEAD
R_COLS = 3 * RWKV_W + 64 + 64 + 128
GRID_W = 64
ROPE_THETA = 10000.0
N_EXPERTS = 8
NORM_EPS = 1e-6
LNX_EPS = 64e-5
LANES = 128
NEG_BIG = -1e30
VMEM_LIMIT_BYTES = 56 * 1024 * 1024
SCAN_CHUNK = 64
FFN_TILE = 512
SCAN_BATCHES_PER_STEP = 4
MOE_BLOCK_ROWS = 1024


def _params(*sem):
    return pltpu.CompilerParams(dimension_semantics=sem, vmem_limit_bytes=VMEM_LIMIT_BYTES)


def _tile(pref, *dims):
    t = pref
    while any(d % t for d in dims):
        t //= 2
    return t


def _ffn_tile(f):
    return _tile(FFN_TILE, f)


def _dot(a, b):
    return jnp.dot(a, b, preferred_element_type=F32)


def _split(x):
    hi = x.astype(BF16)
    lo = (x - hi.astype(F32)).astype(BF16)
    return hi, lo


def _dot3(a, b):
    ah, al = _split(a)
    bh, bl = _split(b)
    return _dot(ah, bh) + (_dot(ah, bl) + _dot(al, bh))


def _dot2(a, b_bf16):
    ah, al = _split(a)
    return _dot(ah, b_bf16) + _dot(al, b_bf16)


def _sigmoid(x):
    return 1.0 / (1.0 + jnp.exp(-x))


def _normmod(x, gain, scale, shift):
    ms = jnp.mean(x * x, axis=-1, keepdims=True)
    return x * lax.rsqrt(ms + NORM_EPS) * gain * (1.0 + scale) + shift


def _head_ones(width):
    r = lax.broadcasted_iota(jnp.int32, (width, width), 0) // HEAD
    c = lax.broadcasted_iota(jnp.int32, (width, width), 1) // HEAD
    return jnp.where(r == c, 1.0, 0.0).astype(BF16)


def _ada_kernel(a_ref, w_ref, b_ref, o_ref):
    a = a_ref[...]
    o_ref[0] = _dot3(a * _sigmoid(a), w_ref[0]) + b_ref[0]


def _ada(act, ada_w, ada_b):
    L, D, N6 = ada_w.shape
    R = act.shape[0]
    tn = _tile(1536, N6)
    return pl.pallas_call(
        _ada_kernel,
        grid=(L, N6 // tn),
        in_specs=[pl.BlockSpec((R, D), lambda l, j: (0, 0)),
                  pl.BlockSpec((1, D, tn), lambda l, j: (l, 0, j)),
                  pl.BlockSpec((1, 1, tn), lambda l, j: (l, 0, j))],
        out_specs=pl.BlockSpec((1, R, tn), lambda l, j: (l, 0, j)),
        out_shape=jax.ShapeDtypeStruct((L, R, N6), F32),
        compiler_params=_params("parallel", "parallel"),
        name="ada",
    )(act, ada_w, ada_b.reshape(L, 1, N6))


def _in_kernel(x_ref, mod_ref, g_ref, w_ref, pr_ref, pq_ref, pkv_ref, pg_ref):
    m = mod_ref[0]
    h = _normmod(x_ref[...], g_ref[...], m[1:2], m[0:1]).astype(BF16)
    col = 0
    for ref in (pr_ref, pq_ref, pkv_ref, pg_ref):
        n = ref.shape[-1]
        for c0 in range(0, n, 256):
            ref[:, c0:c0 + 256] = _dot(h, w_ref[:, col + c0:col + c0 + 256])
        col += n


def _mod_index(tm, nc, seq, nb):
    def index(i):
        start = i * tm
        return (jnp.where(start < nc, nb, (start - nc) // seq), 0, 0)
    return index


def _in_proj(xf, mod, gain, w_bf16, nc, seq, nb):
    N, D = xf.shape
    tm = _tile(512, nc, seq)
    widths = (R_COLS, ATT_W, 2 * KV_W, 2 * D)
    return pl.pallas_call(
        _in_kernel,
        grid=(N // tm,),
        in_specs=[pl.BlockSpec((tm, D), lambda i: (i, 0)),
                  pl.BlockSpec((1, 8, D), _mod_index(tm, nc, seq, nb)),
                  pl.BlockSpec((1, D), lambda i: (0, 0)),
                  pl.BlockSpec(w_bf16.shape, lambda i: (0, 0))],
        out_specs=[pl.BlockSpec((tm, w), lambda i: (i, 0)) for w in widths],
        out_shape=[jax.ShapeDtypeStruct((N, w), F32) for w in widths],
        compiler_params=_params("parallel"),
        name="in_proj",
    )(xf, mod, gain.reshape(1, D), w_bf16)


def _prep_kernel(p_ref, prev_ref, next_ref, mu_ref, w0_ref, w2_ref, a0_ref, a2_ref, g2_ref, kk_ref, ka_ref,
                 rk_ref, r_ref, v_ref, nkk_ref, g_ref, bonus_ref, lw_ref, keys_ref, b_ref, *, tt, nc, ctx, seq):
    i = pl.program_id(0)
    start = i * tt
    in_ctx = start < nc
    pos = jnp.where(in_ctx, start % ctx, (start - nc) % seq)
    seg = jnp.where(in_ctx, ctx, seq)
    first = pos == 0
    last = pos + tt == seg
    p = p_ref[...]
    rows = lax.broadcasted_iota(jnp.int32, (tt, 1), 0)
    prev_row = jnp.where(first, 0.0, prev_ref[7:8, :])
    next_row = jnp.where(last, 0.0, next_ref[0:1, :])
    prev = jnp.where(rows == 0, prev_row, pltpu.roll(p, 1, 0))
    nxt = jnp.where(rows == tt - 1, next_row, pltpu.roll(p, tt - 1, 0))
    ps = p + mu_ref[0:1, :] * (prev - p) + mu_ref[1:2, :] * (nxt - p)
    W = RWKV_W
    r = ps[:, 0:W]
    k = ps[:, W:2 * W]
    v = ps[:, 2 * W:3 * W]
    wa = ps[:, 3 * W:3 * W + LORA_W]
    gd = ps[:, 3 * W + LORA_W:]
    ones = _head_ones(W)
    kkf = k * kk_ref[...]
    nrm = jnp.sqrt(_dot2(kkf * kkf, ones))
    kk = kkf / jnp.maximum(nrm, 1e-12)
    tw = jnp.tanh(wa)
    ksum = jnp.zeros_like(k)
    for d in range(2):
        z = w0_ref[d:d + 1, :] + _dot3(tw, w2_ref[d])
        lw_ref[d] = -math.exp(-0.5) * _sigmoid(z)
        a = _sigmoid(a0_ref[d:d + 1, :] + _dot3(wa, a2_ref[d]))
        keys = k * (1.0 + (a - 1.0) * ka_ref[...])
        keys_ref[d] = keys
        b_ref[d] = kk * a
        ksum = ksum + keys
    r_ref[...] = r
    v_ref[...] = v
    nkk_ref[...] = -kk
    g_ref[...] = _dot3(_sigmoid(gd), g2_ref[...])
    bonus_ref[...] = _dot2(r * (0.5 * ksum) * rk_ref[...], ones) * v


def _rwkv_prep(p_r, mu, w0, w2, a0, a2, g2, k_k, k_a, r_k, nc, ctx, seq):
    N = p_r.shape[0]
    W = RWKV_W
    tt = _tile(256, ctx, seq)
    nblk8 = N // 8
    zeros = jnp.zeros((2, 64, W), F32)
    w2p = jnp.concatenate([w2, zeros], axis=1)
    a2p = jnp.concatenate([zeros, a2], axis=1)
    row = lambda t: t.reshape(1, W)
    full = lambda a: pl.BlockSpec(a.shape, lambda i: (0,) * a.ndim)
    consts = (mu, w0, w2p, a0, a2p, g2, row(k_k), row(k_a), row(r_k))
    one = jax.ShapeDtypeStruct((N, W), F32)
    two = jax.ShapeDtypeStruct((2, N, W), F32)
    s1 = pl.BlockSpec((tt, W), lambda i: (i, 0))
    s2 = pl.BlockSpec((2, tt, W), lambda i: (0, i, 0))
    return pl.pallas_call(
        functools.partial(_prep_kernel, tt=tt, nc=nc, ctx=ctx, seq=seq),
        grid=(N // tt,),
        in_specs=[pl.BlockSpec((tt, R_COLS), lambda i: (i, 0)),
                  pl.BlockSpec((8, R_COLS), lambda i: (jnp.maximum(i * (tt // 8) - 1, 0), 0)),
                  pl.BlockSpec((8, R_COLS), lambda i: (jnp.minimum((i + 1) * (tt // 8), nblk8 - 1), 0))]
                 + [full(a) for a in consts],
        out_specs=[s1, s1, s1, s1, s1, s2, s2, s2],
        out_shape=[one, one, one, one, one, two, two, two],
        compiler_params=_params("parallel"),
        name="rwkv_prep",
    )(p_r, p_r, p_r, *consts)


def _pair_blocks(x):
    left = lax.broadcasted_iota(jnp.int32, x.shape, 1) < HEAD
    zero = jnp.zeros_like(x)
    return jnp.concatenate([jnp.where(left, x, zero), jnp.where(left, zero, x)], axis=0)


def _pair_diag(x):
    left = lax.broadcasted_iota(jnp.int32, (HEAD, LANES), 1) < HEAD
    return jnp.where(left, x[:HEAD], x[HEAD:])


def _scan_pair_kernel(*refs, C, direction, nslot):
    assert C == HEAD
    y_ref, st_ref = refs[6 * nslot], refs[6 * nslot + 1]

    @pl.when(pl.program_id(1) == 0)
    def _():
        st_ref[...] = jnp.zeros_like(st_ref)

    npair = RWKV_HEADS // 2
    row = lax.broadcasted_iota(jnp.int32, (C, C), 0)
    col = lax.broadcasted_iota(jnp.int32, (C, C), 1)
    before = row - col if direction == 0 else col - row
    linc = jnp.where(before >= 0, 1.0, 0.0).astype(BF16)
    row2 = lax.broadcasted_iota(jnp.int32, (C, LANES), 0)
    col2 = lax.broadcasted_iota(jnp.int32, (C, LANES), 1) % C
    before2 = row2 - col2 if direction == 0 else col2 - row2
    strict = before2 > 0
    incl = before2 >= 0
    eye2 = row2 == col2
    eye2b = jnp.where(eye2, 1.0, 0.0).astype(BF16)
    nt = (((1,), (1,)), ((), ()))
    tn = (((0,), (0,)), ((), ()))
    nlev = int(math.log2(C))

    at, rt, bt, kt, bh, kh, vb, g_tot = ([] for _ in range(8))
    for s in range(nslot):
        r_ref, v_ref, a_ref, lw_ref, k_ref, b_ref = refs[6 * s:6 * s + 6]
        lw = lw_ref[0]
        lh, ll = _split(lw)
        lam = _dot(linc, lh) + _dot(linc, ll)
        tot = jnp.sum(lw, axis=0, keepdims=True)
        g_rem = jnp.exp(tot - lam)
        g_inv = jnp.exp(-lam)
        gt = jnp.exp(tot)
        k = k_ref[0]
        b = b_ref[0]
        a_s = a_ref[...] * jnp.exp(lam - lw)
        r_s = r_ref[...] * jnp.exp(lam)
        b_t = (b * g_inv).astype(BF16)
        k_t = (k * g_inv).astype(BF16)
        b_h = (b * g_rem).astype(BF16)
        k_h = (k * g_rem).astype(BF16)
        v_b = v_ref[...].astype(BF16)
        for p in range(npair):
            sl = slice(p * LANES, (p + 1) * LANES)
            for dst, src in ((at, a_s), (rt, r_s), (bt, b_t), (kt, k_t), (bh, b_h), (kh, k_h), (vb, v_b), (g_tot, gt)):
                dst.append(src[:, sl])
    nch = nslot * npair

    ar = [jnp.concatenate([at[c].astype(BF16), rt[c].astype(BF16)], axis=0) for c in range(nch)]
    L = LANES
    xbk = [lax.dot_general(ar[c], jnp.concatenate([_pair_blocks(bt[c]), _pair_blocks(kt[c])], axis=0), nt,
                           preferred_element_type=F32) for c in range(nch)]
    s_pow = [jnp.where(strict, x[:C, :L], 0.0) for x in xbk]
    m_rb = [jnp.where(incl, x[C:, :L], 0.0).astype(BF16) for x in xbk]
    q = [jnp.where(strict, x[:C, L:], 0.0) for x in xbk]
    m_rk = [jnp.where(incl, x[C:, L:], 0.0) for x in xbk]
    ah = list(at)
    for lev in range(nlev):
        for c in range(nch):
            sb = s_pow[c].astype(BF16)
            rhs = [_pair_blocks(ah[c].astype(BF16)), _pair_blocks(q[c].astype(BF16))]
            if lev < nlev - 1:
                rhs.append(_pair_blocks(sb))
            prod = _dot(sb, jnp.concatenate(rhs, axis=1))
            ah[c] = ah[c] + prod[:, :L]
            q[c] = q[c] + prod[:, L:2 * L]
            if lev < nlev - 1:
                s_pow[c] = prod[:, 2 * L:]
    ahb = [x.astype(BF16) for x in ah]
    qb = [x.astype(BF16) for x in q]
    raq = [_dot(m_rb[c], jnp.concatenate([_pair_blocks(ahb[c]), _pair_blocks(qb[c])], axis=1))
           for c in range(nch)]
    baq = [lax.dot_general(bh[c], jnp.concatenate([ahb[c], qb[c]], axis=1), tn, preferred_element_type=F32)
           for c in range(nch)]
    k_tr = [_pair_diag(lax.dot_general(kh[c], eye2b, tn, preferred_element_type=F32)) for c in range(nch)]
    for c in range(nch):
        s, p = divmod(c, npair)
        r_hat = rt[c] + raq[c][:, :L]
        g_mat = jnp.where(eye2, g_tot[c], 0.0) + _pair_diag(baq[c][:, :L])
        v2 = _pair_blocks(vb[c])
        y_in = _dot((m_rk[c] + raq[c][:, L:]).astype(BF16), v2)
        h_mat = _dot((_pair_diag(baq[c][:, L:]) + k_tr[c]).astype(BF16), v2)
        sh, sl_ = _split(st_ref[c])
        sh, sl_ = _pair_blocks(sh), _pair_blocks(sl_)
        gh, gl = _split(g_mat)
        gs = _dot(gh, jnp.concatenate([sh, sl_], axis=1))
        st_ref[c] = gs[:, :L] + (gs[:, L:] + _dot(gl, sh)) + h_mat
        y_ref[s, :, p * LANES:(p + 1) * LANES] = _dot(r_hat.astype(BF16), sh) + y_in


def _wkv_scan_pairs(r, v, nkk, lw, keys, b, nb, ctx, seq):
    N, W = r.shape
    C = SCAN_CHUNK
    ncc, nlc = ctx // C, seq // C
    nslot = math.gcd(SCAN_BATCHES_PER_STEP, nb)
    hb = nb // nslot

    def blk(d, j, bi, nbat):
        jc = j if d == 0 else ncc - 1 - j
        jl = j - ncc if d == 0 else nlc - 1 - (j - ncc)
        return jnp.where(j < ncc, bi * ncc + jc, nbat * ncc + bi * nlc + jl)

    def one(d):
        s1 = lambda s: pl.BlockSpec((C, W), lambda i, j: (blk(d, j, s * hb + i, nb), 0))
        s2 = lambda s: pl.BlockSpec((1, C, W), lambda i, j: (d, blk(d, j, s * hb + i, nb), 0))
        in_specs = [spec for s in range(nslot) for spec in (s1(s), s1(s), s1(s), s2(s), s2(s), s2(s))]
        return pl.pallas_call(
            functools.partial(_scan_pair_kernel, C=C, direction=d, nslot=nslot),
            grid=(hb, ncc + nlc),
            in_specs=in_specs,
            out_specs=pl.BlockSpec((nslot, C, W), lambda i, j: (0, blk(d, j, i, hb), 0)),
            out_shape=jax.ShapeDtypeStruct((nslot, N // nslot, W), F32),
            scratch_shapes=[pltpu.VMEM((nslot * RWKV_HEADS // 2, HEAD, LANES), F32)],
            compiler_params=_params("parallel", "arbitrary"),
            name="wkv_scan",
        )(*([r, v, nkk, lw, keys, b] * nslot))

    return one(0), one(1)


def _slot_rows(tm, nc, nl, nslot):
    def index(i):
        start = i * tm
        in_ctx = start < nc
        rel = jnp.where(in_ctx, start, start - nc)
        per = jnp.where(in_ctx, nc // nslot, nl // nslot)
        local = jnp.where(in_ctx, 0, nc // nslot) + rel % per
        return (rel // per, local // tm, 0)
    return index


def _rope(t, cos, sin_signed):
    w = t.shape[-1]
    lane = lax.broadcasted_iota(jnp.int32, t.shape, 1)
    swapped = jnp.where(lane % 2 == 0, pltpu.roll(t, w - 1, 1), pltpu.roll(t, 1, 1))
    return t * cos + swapped * sin_signed


def _head_rms(t, gain):
    ms = _dot2(t * t, _head_ones(t.shape[-1])) * (1.0 / HEAD)
    return t * lax.rsqrt(ms + NORM_EPS) * gain


def _kv_kernel(p_ref, kn_ref, cos_ref, sin_ref, k_ref, v_ref, *, tt, nc):
    p = p_ref[...]
    kn = _head_rms(p[:, :KV_W], kn_ref[...])
    kr = _rope(kn, cos_ref[...], sin_ref[...])
    k = jnp.where(pl.program_id(0) * tt < nc, kn, kr).astype(BF16)
    v = p[:, KV_W:].astype(BF16)
    one_col = jnp.where(lax.broadcasted_iota(jnp.int32, (tt, HEAD), 1) == 0, 1.0, 0.0).astype(BF16)
    for h in range(KV_HEADS):
        k_ref[h] = k[:, h * HEAD:(h + 1) * HEAD]
        v_ref[h] = jnp.concatenate([v[:, h * HEAD:(h + 1) * HEAD], one_col], axis=1)


def _kv_prep(p_kv, k_norm, cos, sin_signed, nc, ctx, seq):
    N = p_kv.shape[0]
    tt = _tile(512, ctx, seq)
    npos = seq // tt
    pos = lambda i: ((jnp.maximum(i * tt - nc, 0) // tt) % npos, 0)
    spec = lambda w: pl.BlockSpec((KV_HEADS, tt, w), lambda i: (0, i, 0))
    return pl.pallas_call(
        functools.partial(_kv_kernel, tt=tt, nc=nc),
        grid=(N // tt,),
        in_specs=[pl.BlockSpec((tt, 2 * KV_W), lambda i: (i, 0)),
                  pl.BlockSpec((1, KV_W), lambda i: (0, 0)),
                  pl.BlockSpec((tt, KV_W), pos),
                  pl.BlockSpec((tt, KV_W), pos)],
        out_specs=[spec(HEAD), spec(2 * HEAD)],
        out_shape=[jax.ShapeDtypeStruct((KV_HEADS, N, w), BF16) for w in (HEAD, 2 * HEAD)],
        compiler_params=_params("parallel"),
        name="kv_prep",
    )(p_kv, jnp.tile(k_norm, KV_HEADS).reshape(1, KV_W), cos[:, :KV_W], sin_signed[:, :KV_W])


def _attn_kernel(*refs, nseg, rope):
    q_ref, qn_ref = refs[0], refs[1]
    pos = 2
    if rope:
        cos_ref, sin_ref = refs[2], refs[3]
        pos = 4
    kv_refs = refs[pos:pos + 2 * nseg]
    o_ref = refs[pos + 2 * nseg]
    q = _head_rms(q_ref[...], qn_ref[...])
    if rope:
        q = _rope(q, cos_ref[...], sin_ref[...])
    q = (q * (HEAD ** -0.5 * math.log2(math.e))).astype(BF16)
    outs = []
    for g in range(GROUPS):
        qh = q[:, g * HEAD:(g + 1) * HEAD]
        s = [lax.dot_general(qh, kv_refs[2 * i][0], (((1,), (1,)), ((), ())), preferred_element_type=F32)
             for i in range(nseg)]
        m = s[0].max(axis=-1, keepdims=True)
        for si in s[1:]:
            m = jnp.maximum(m, si.max(axis=-1, keepdims=True))
        acc = None
        for i in range(nseg):
            pv = _dot(jnp.exp2(s[i] - m).astype(BF16), kv_refs[2 * i + 1][0])
            acc = pv if acc is None else acc + pv
        outs.append(acc[:, :HEAD] / acc[:, HEAD:HEAD + 1])
    o_ref[...] = jnp.concatenate(outs, axis=1)


def _attention(p_q, k, v, q_norm, cos, sin_signed, nb, nc, ctx, seq, latent):
    qlen = seq if latent else ctx
    tq = _tile(512, qlen)
    nqt = qlen // tq
    qw = GROUPS * HEAD
    qoff = nc // tq if latent else 0
    qmap = lambda bi, g, t: (qoff + bi * nqt + t, g)
    in_specs = [pl.BlockSpec((tq, qw), qmap), pl.BlockSpec((1, qw), lambda bi, g, t: (0, 0))]
    args = [p_q, jnp.tile(q_norm, GROUPS).reshape(1, qw)]
    ctx_spec = lambda a: pl.BlockSpec((1, ctx, a.shape[-1]), lambda bi, g, t: (g, bi, 0))
    if latent:
        in_specs += [pl.BlockSpec((tq, qw), lambda bi, g, t: (t, 0))] * 2
        args += [cos, sin_signed]
        lat_spec = lambda a: pl.BlockSpec((1, seq, a.shape[-1]), lambda bi, g, t: (g, nc // seq + bi, 0))
        in_specs += [lat_spec(k), lat_spec(v), ctx_spec(k), ctx_spec(v)]
        args += [k, v, k, v]
    else:
        in_specs += [ctx_spec(k), ctx_spec(v)]
        args += [k, v]
    return pl.pallas_call(
        functools.partial(_attn_kernel, nseg=2 if latent else 1, rope=latent),
        grid=(nb, KV_HEADS, nqt),
        in_specs=in_specs,
        out_specs=pl.BlockSpec((tq, qw), lambda bi, g, t: (bi * nqt + t, g)),
        out_shape=jax.ShapeDtypeStruct((nb * qlen, ATT_W), F32),
        compiler_params=_params("parallel", "parallel", "parallel"),
        name="attn_lat" if latent else "attn_ctx",
    )(*args)


def _merge_kernel(x_ref, mod_ref, yf_ref, yb_ref, bonus_ref, g_ref, yc_ref, yl_ref, pg_ref, lw_ref, lb_ref, wpa_ref,
                  wpb_ref, wo_ref, o_ref, *, nct):
    D = x_ref.shape[-1]
    y_att = yl_ref[...] if nct == 0 else jnp.where(pl.program_id(0) < nct, yc_ref[...], yl_ref[...])
    ones = _head_ones(RWKV_W)
    y = yf_ref[0] + yb_ref[0]
    mean = _dot2(y, ones) * (1.0 / HEAD)
    yc = y - mean
    var = _dot2(yc * yc, ones) * (1.0 / HEAD)
    yn = yc * lax.rsqrt(var + LNX_EPS) * lw_ref[...] + lb_ref[...]
    y_rwkv = (yn + bonus_ref[...]) * g_ref[...]
    pg = pg_ref[...]
    mix = (_sigmoid(pg[:, :D]) * _dot(y_rwkv.astype(BF16), wpa_ref[...])
           + _sigmoid(pg[:, D:]) * _dot(y_att.astype(BF16), wpb_ref[...]))
    o_ref[...] = x_ref[...] + mod_ref[0][2:3] * _dot(mix.astype(BF16), wo_ref[...])


def _merge(xf, mod, y_fwd, y_bwd, bonus, g, y_ctx, y_lat, p_g, lnx_w, lnx_b, w_pa, w_pb, w_o, nc, seq, nb, row0):
    N, D = xf.shape
    W = RWKV_W
    tm = _tile(512, nc // y_fwd.shape[0], seq)
    off = row0 // tm
    n_out = N - row0
    mod_idx = _mod_index(tm, nc, seq, nb)
    rows = lambda w: pl.BlockSpec((tm, w), lambda i: (i + off, 0))
    slot_index = _slot_rows(tm, nc, N - nc, y_fwd.shape[0])
    scan_rows = pl.BlockSpec((1, tm, W), lambda i: slot_index(i + off))
    full = lambda a: pl.BlockSpec(a.shape, lambda i: (0,) * a.ndim)
    consts = (lnx_w.reshape(1, W), lnx_b.reshape(1, W), w_pa, w_pb, w_o)
    nct = (nc - row0) // tm
    att_ctx = pl.BlockSpec((tm, W), lambda i: (jnp.minimum(i, max(nct - 1, 0)), 0))
    att_lat = pl.BlockSpec((tm, W), lambda i: (jnp.maximum(i - nct, 0), 0))
    return pl.pallas_call(
        functools.partial(_merge_kernel, nct=nct),
        grid=(n_out // tm,),
        in_specs=[rows(D), pl.BlockSpec((1, 8, D), lambda i: mod_idx(i + off)),
                  scan_rows, scan_rows, rows(W), rows(W), att_ctx, att_lat, rows(2 * D)]
                 + [full(a) for a in consts],
        out_specs=pl.BlockSpec((tm, D), lambda i: (i, 0)),
        out_shape=jax.ShapeDtypeStruct((n_out, D), F32),
        compiler_params=_params("parallel"),
        name="merge",
    )(xf, mod, y_fwd, y_bwd, bonus, g, y_ctx, y_lat, p_g, *consts)


def _ffn_kernel(x_ref, mod_ref, g_ref, wg_ref, wu_ref, wd_ref, o_ref, h_ref, acc_ref):
    f = pl.program_id(1)

    @pl.when(f == 0)
    def _():
        m = mod_ref[0]
        h_ref[...] = _normmod(x_ref[...], g_ref[...], m[4:5], m[3:4]).astype(BF16)
        acc_ref[...] = jnp.zeros_like(acc_ref)

    h = h_ref[...]
    a = _dot(h, wg_ref[...])
    z = a * _sigmoid(a) * _dot(h, wu_ref[...])
    acc_ref[...] += _dot(z.astype(BF16), wd_ref[...])

    @pl.when(f == pl.num_programs(1) - 1)
    def _():
        o_ref[...] = x_ref[...] + mod_ref[0][5:6] * acc_ref[...]


def _ffn(xf, mod, gain, wg, wu, wd, nc, seq, nb):
    N, D = xf.shape
    F = wg.shape[-1]
    tm = _tile(1024, nc, seq)
    tf = _ffn_tile(F)
    return pl.pallas_call(
        _ffn_kernel,
        grid=(N // tm, F // tf),
        in_specs=[pl.BlockSpec((tm, D), lambda i, f: (i, 0)),
                  pl.BlockSpec((1, 8, D), lambda i, f: _mod_index(tm, nc, seq, nb)(i)),
                  pl.BlockSpec((1, D), lambda i, f: (0, 0)),
                  pl.BlockSpec((D, tf), lambda i, f: (0, f)),
                  pl.BlockSpec((D, tf), lambda i, f: (0, f)),
                  pl.BlockSpec((tf, D), lambda i, f: (f, 0))],
        out_specs=pl.BlockSpec((tm, D), lambda i, f: (i, 0)),
        out_shape=jax.ShapeDtypeStruct((N, D), F32),
        scratch_shapes=[pltpu.VMEM((tm, D), BF16), pltpu.VMEM((tm, D), F32)],
        compiler_params=_params("parallel", "arbitrary"),
        name="ffn",
    )(xf, mod, gain.reshape(1, D), wg, wu, wd)


R_E1, R_E2, R_RANK1, R_RANK2, R_W1, R_W2 = range(6)


def _router_kernel(x_ref, mod_ref, g_ref, wr_ref, route_ref, cnt_ref, carry_ref):
    @pl.when(pl.program_id(0) == 0)
    def _():
        carry_ref[...] = jnp.zeros_like(carry_ref)

    m = mod_ref[0]
    h = _normmod(x_ref[...], g_ref[...], m[4:5], m[3:4])
    tm = h.shape[0]
    lane = lax.broadcasted_iota(jnp.int32, (tm, LANES), 1).astype(F32)
    logits = jnp.where(lane < N_EXPERTS, _dot3(h, wr_ref[...]), NEG_BIG)
    m1 = logits.max(axis=-1, keepdims=True)
    i1 = jnp.where(logits == m1, lane, float(LANES)).min(axis=-1, keepdims=True)
    rest = jnp.where(lane == i1, NEG_BIG, logits)
    m2 = rest.max(axis=-1, keepdims=True)
    i2 = jnp.where(rest == m2, lane, float(LANES)).min(axis=-1, keepdims=True)
    e2 = jnp.exp(m2 - m1)
    w1 = 1.0 / (1.0 + e2)
    w2 = e2 / (1.0 + e2)
    onehot = jnp.where(lane == i1, 1.0, jnp.where(lane == i2, 1.0, 0.0))
    earlier = (lax.broadcasted_iota(jnp.int32, (tm, tm), 0) > lax.broadcasted_iota(jnp.int32, (tm, tm), 1))
    before = _dot(jnp.where(earlier, 1.0, 0.0).astype(BF16), onehot.astype(BF16)) + carry_ref[...]
    rank1 = jnp.where(lane == i1, before, 0.0).sum(axis=-1, keepdims=True)
    rank2 = jnp.where(lane == i2, before, 0.0).sum(axis=-1, keepdims=True)
    carry_ref[...] += onehot.sum(axis=0, keepdims=True)
    cnt_ref[...] = jnp.broadcast_to(carry_ref[...], cnt_ref.shape)
    rec = jnp.zeros_like(lane)
    for idx, val in ((R_E1, i1), (R_E2, i2), (R_RANK1, rank1), (R_RANK2, rank2), (R_W1, w1), (R_W2, w2)):
        rec = jnp.where(lane == float(idx), val, rec)
    route_ref[...] = rec


def _router(xl, mod, gain, router, seq):
    N, D = xl.shape
    tm = _tile(512, seq)
    wr = jnp.zeros((D, LANES), F32).at[:, :N_EXPERTS].set(router)
    return pl.pallas_call(
        _router_kernel,
        grid=(N // tm,),
        in_specs=[pl.BlockSpec((tm, D), lambda i: (i, 0)),
                  pl.BlockSpec((1, 8, D), lambda i: ((i * tm) // seq, 0, 0)),
                  pl.BlockSpec((1, D), lambda i: (0, 0)),
                  pl.BlockSpec((D, LANES), lambda i: (0, 0))],
        out_specs=[pl.BlockSpec((tm, LANES), lambda i: (i, 0)), pl.BlockSpec((8, LANES), lambda i: (0, 0))],
        out_shape=[jax.ShapeDtypeStruct((N, LANES), F32), jax.ShapeDtypeStruct((8, LANES), F32)],
        scratch_shapes=[pltpu.VMEM((1, LANES), F32)],
        compiler_params=_params("arbitrary"),
        name="router",
    )(xl, mod, gain.reshape(1, D), wr)


def _all_rows(n, copies):
    def start(r, c):
        for copy in copies:
            copy(r).start()
        return c

    def wait(r, c):
        for copy in copies:
            copy(r).wait()
        return c

    lax.fori_loop(0, n, start, 0, unroll=8)
    lax.fori_loop(0, n, wait, 0, unroll=8)


def _dispatch_kernel(dest_ref, x_ref, mod_ref, g_ref, xs_in_ref, xs_ref, h_ref, sem):
    del xs_in_ref
    tm = x_ref.shape[0]
    n = tm * pl.num_programs(0)
    base = pl.program_id(0) * tm
    m = mod_ref[0]
    h_ref[...] = _normmod(x_ref[...], g_ref[...], m[4:5], m[3:4])
    _all_rows(tm, [lambda r, off=off: pltpu.make_async_copy(
        h_ref.at[pl.ds(r, 1)], xs_ref.at[pl.ds(dest_ref[off + base + r], 1)], sem) for off in (0, n)])


def _dispatch(dest, xl, mod, gain, xs, seq):
    N, D = xl.shape
    tm = _tile(512, seq)
    return pl.pallas_call(
        _dispatch_kernel,
        grid_spec=pltpu.PrefetchScalarGridSpec(
            num_scalar_prefetch=1,
            grid=(N // tm,),
            in_specs=[pl.BlockSpec((tm, D), lambda i, dest: (i, 0)),
                      pl.BlockSpec((1, 8, D), lambda i, dest: ((i * tm) // seq, 0, 0)),
                      pl.BlockSpec((1, D), lambda i, dest: (0, 0)),
                      pl.BlockSpec(memory_space=pl.ANY)],
            out_specs=pl.BlockSpec(memory_space=pl.ANY),
            scratch_shapes=[pltpu.VMEM((tm, D), F32), pltpu.SemaphoreType.DMA]),
        out_shape=jax.ShapeDtypeStruct(xs.shape, xs.dtype),
        input_output_aliases={4: 0},
        compiler_params=_params("arbitrary"),
        name="moe_dispatch",
    )(dest, xl, mod, gain.reshape(1, D), xs)


def _expert_kernel(tbl_ref, xs_ref, wg_ref, wu_ref, wd_ref, ys_ref, hb_ref, acc_ref, *, nblk):
    i = pl.program_id(0)
    f = pl.program_id(1)
    used = i < tbl_ref[nblk]
    last = f == pl.num_programs(1) - 1

    @pl.when(used & (f == 0))
    def _():
        hb_ref[...] = xs_ref[...].astype(BF16)
        acc_ref[...] = jnp.zeros_like(acc_ref)

    @pl.when(used)
    def _():
        h = hb_ref[...]
        a = _dot(h, wg_ref[0])
        z = a * _sigmoid(a) * _dot(h, wu_ref[0])
        acc_ref[...] += _dot(z.astype(BF16), wd_ref[0])

    @pl.when(used & last)
    def _():
        ys_ref[...] = acc_ref[...]

    @pl.when(jnp.logical_not(used) & last)
    def _():
        ys_ref[...] = jnp.zeros_like(ys_ref)


def _experts(tbl, xs, wg, wu, wd, blk):
    R, D = xs.shape
    F = wg.shape[-1]
    nblk = R // blk
    tf = _ffn_tile(F)
    return pl.pallas_call(
        functools.partial(_expert_kernel, nblk=nblk),
        grid_spec=pltpu.PrefetchScalarGridSpec(
            num_scalar_prefetch=1,
            grid=(nblk, F // tf),
            in_specs=[pl.BlockSpec((blk, D), lambda i, f, tbl: (i, 0)),
                      pl.BlockSpec((1, D, tf), lambda i, f, tbl: (tbl[i], 0, f)),
                      pl.BlockSpec((1, D, tf), lambda i, f, tbl: (tbl[i], 0, f)),
                      pl.BlockSpec((1, tf, D), lambda i, f, tbl: (tbl[i], f, 0))],
            out_specs=pl.BlockSpec((blk, D), lambda i, f, tbl: (i, 0)),
            scratch_shapes=[pltpu.VMEM((blk, D), BF16), pltpu.VMEM((blk, D), F32)]),
        out_shape=jax.ShapeDtypeStruct((R, D), F32),
        compiler_params=_params("parallel", "arbitrary"),
        name="moe_experts",
    )(tbl, xs, wg, wu, wd)


def _combine_kernel(dest_ref, x_ref, route_ref, mod_ref, fn_ref, ys_ref, o_ref, buf_ref, sem):
    tm = x_ref.shape[0]
    n = tm * pl.num_programs(0)
    base = pl.program_id(0) * tm
    _all_rows(tm, [lambda r, slot=slot: pltpu.make_async_copy(
        ys_ref.at[pl.ds(dest_ref[slot * n + base + r], 1)], buf_ref.at[pl.ds(slot * tm + r, 1)], sem)
        for slot in (0, 1)])
    mix = route_ref[:, R_W1:R_W1 + 1] * buf_ref[:tm] + route_ref[:, R_W2:R_W2 + 1] * buf_ref[tm:]
    y = x_ref[...] + mod_ref[0][5:6] * mix
    ms = jnp.mean(y * y, axis=-1, keepdims=True)
    o_ref[...] = y * lax.rsqrt(ms + NORM_EPS) * fn_ref[...]


def _combine(dest, acc, route, mod, final_norm, ys, seq):
    N, D = acc.shape
    tm = _tile(512, seq)
    return pl.pallas_call(
        _combine_kernel,
        grid_spec=pltpu.PrefetchScalarGridSpec(
            num_scalar_prefetch=1,
            grid=(N // tm,),
            in_specs=[pl.BlockSpec((tm, D), lambda i, dest: (i, 0)),
                      pl.BlockSpec((tm, LANES), lambda i, dest: (i, 0)),
                      pl.BlockSpec((1, 8, D), lambda i, dest: ((i * tm) // seq, 0, 0)),
                      pl.BlockSpec((1, D), lambda i, dest: (0, 0)),
                      pl.BlockSpec(memory_space=pl.ANY)],
            out_specs=pl.BlockSpec((tm, D), lambda i, dest: (i, 0)),
            scratch_shapes=[pltpu.VMEM((2 * tm, D), F32), pltpu.SemaphoreType.DMA]),
        out_shape=jax.ShapeDtypeStruct((N, D), F32),
        compiler_params=_params("arbitrary"),
        name="moe_combine",
    )(dest, acc, route, mod, final_norm.reshape(1, D), ys)


def _moe_final(xl, mod, gain, router, wg, wu, wd, final_norm, seq):
    N, D = xl.shape
    blk = MOE_BLOCK_ROWS
    route, counts = _router(xl, mod, gain, router, seq)
    counts = counts[0, :N_EXPERTS].astype(jnp.int32)
    padded = (counts + blk - 1) // blk * blk
    ends = jnp.cumsum(padded)
    starts = ends - padded
    nblk = -(-(2 * N) // blk) + N_EXPERTS
    blk_expert = jnp.minimum(jnp.searchsorted(ends, jnp.arange(nblk, dtype=jnp.int32) * blk, side='right'),
                             N_EXPERTS - 1).astype(jnp.int32)
    tbl = jnp.concatenate([blk_expert, (ends[-1:] // blk).astype(jnp.int32)])
    dest = jnp.concatenate([starts[route[:, e].astype(jnp.int32)] + route[:, r].astype(jnp.int32)
                            for e, r in ((R_E1, R_RANK1), (R_E2, R_RANK2))])
    xs = _dispatch(dest, xl, mod, gain, jnp.zeros((nblk * blk, D), F32), seq)
    ys = _experts(tbl, xs, wg, wu, wd, blk)
    return _combine(dest, xl, route, mod, final_norm, ys, seq)


def _rope_tables(seq):
    rows = seq // GRID_W
    row, col = jnp.meshgrid(jnp.arange(rows), jnp.arange(GRID_W), indexing='ij')
    axis = HEAD // 2
    inv = ROPE_THETA ** (-jnp.arange(0, axis, 2, dtype=F32) / axis)
    ang = jnp.concatenate([row.reshape(-1, 1).astype(F32) * inv, col.reshape(-1, 1).astype(F32) * inv], axis=-1)
    cos = jnp.repeat(jnp.cos(ang), 2, axis=-1)
    sin = jnp.repeat(jnp.sin(ang), 2, axis=-1) * jnp.tile(jnp.array([-1.0, 1.0], F32), HEAD // 2)
    return jnp.tile(cos, (1, GROUPS)), jnp.tile(sin, (1, GROUPS))


def kernel(x, c, ctx, c_ctx, ada_w, ada_b, norm1, norm2, w_in, shift_mu, rwkv_w0, rwkv_w2, rwkv_a0, rwkv_a2, rwkv_g2, rwkv_kk, rwkv_ka, rwkv_rk, lnx_w, lnx_b, q_norm, k_norm, w_pa, w_pb, w_o, ffn_wg, ffn_wu, ffn_wd, router, moe_wg, moe_wu, moe_wd, final_norm):
    B, T, D = x.shape
    CX = ctx.shape[1]
    depth = ada_w.shape[0]
    assert depth == 2 and ffn_wg.shape[0] == 1 and router.shape[0] == 1
    nc = B * CX
    bf = lambda w: w.astype(BF16)

    nrow = -(-(B + 1) // 8) * 8
    act = jnp.zeros((nrow, D), F32).at[:B].set(c).at[B].set(c_ctx)
    mod = _ada(act, ada_w, ada_b).reshape(depth, nrow, 6, D)
    mod = jnp.concatenate([mod, jnp.zeros((depth, nrow, 2, D), F32)], axis=2)

    cos, sin_signed = _rope_tables(T)
    xf = jnp.concatenate([ctx.reshape(nc, D), x.reshape(B * T, D)], axis=0)
    out = None
    for l in range(depth):
        last = l == depth - 1
        p_r, p_q, p_kv, p_g = _in_proj(xf, mod[l], norm1[l], bf(w_in[l]), nc, T, B)
        r, v, nkk, g, bonus, lw, keys, b = _rwkv_prep(
            p_r, shift_mu[l], rwkv_w0[l], rwkv_w2[l], rwkv_a0[l], rwkv_a2[l], rwkv_g2[l], rwkv_kk[l], rwkv_ka[l],
            rwkv_rk[l].reshape(-1), nc, CX, T)
        y_fwd, y_bwd = _wkv_scan_pairs(r, v, nkk, lw, keys, b, B, CX, T)
        k_att, v_att = _kv_prep(p_kv, k_norm[l], cos, sin_signed, nc, CX, T)
        y_lat = _attention(p_q, k_att, v_att, q_norm[l], cos, sin_signed, B, nc, CX, T, latent=True)
        y_ctx = y_lat if last else _attention(p_q, k_att, v_att, q_norm[l], cos, sin_signed, B, nc, CX, T,
                                              latent=False)
        xm = _merge(xf, mod[l], y_fwd, y_bwd, bonus, g, y_ctx, y_lat, p_g, lnx_w[l], lnx_b[l], bf(w_pa[l]),
                    bf(w_pb[l]), bf(w_o[l]), nc, T, B, nc if last else 0)
        if not last:
            xf = _ffn(xm, mod[l], norm2[l], bf(ffn_wg[l // 2]), bf(ffn_wu[l // 2]), bf(ffn_wd[l // 2]), nc, T, B)
        else:
            out = _moe_final(xm, mod[l], norm2[l], router[l // 2], bf(moe_wg[l // 2]), bf(moe_wu[l // 2]),
                             bf(moe_wd[l // 2]), final_norm, T)
    return out.reshape(B, T, D)
```

```python
import functools
import math

import jax
import jax.numpy as jnp
from jax import lax
from jax.experimental import pallas as pl
from jax.experimental.pallas import tpu as pltpu

F32 = jnp.float32
BF16 = jnp.bfloat16

HEAD = 64
RWKV_HEADS = 8
RWKV_W = RWKV_HEADS * HEAD
LORA_W = 128
ATT_HEADS = 8
KV_HEADS = 2
GROUPS = ATT_HEADS // KV_HEADS
ATT_W = ATT_HEADS * HEAD
KV_W = KV_HEADS * HEAD
R_COLS = 3 * RWKV_W + 64 + 64 + 128
GRID_W = 64
ROPE_THETA = 10000.0
N_EXPERTS = 8
NORM_EPS = 1e-6
LNX_EPS = 64e-5
LANES = 128
NEG_BIG = -1e30
VMEM_LIMIT_BYTES = 56 * 1024 * 1024
SCAN_CHUNK = 64
FFN_TILE = 512
SCAN_BATCHES_PER_STEP = 4
MOE_BLOCK_ROWS = 1024


def _params(*sem):
    return pltpu.CompilerParams(dimension_semantics=sem, vmem_limit_bytes=VMEM_LIMIT_BYTES)


def _tile(pref, *dims):
    t = pref
    while any(d % t for d in dims):
        t //= 2
    return t


def _ffn_tile(f):
    return _tile(FFN_TILE, f)


def _dot(a, b):
    return jnp.dot(a, b, preferred_element_type=F32)


def _split(x):
    hi = x.astype(BF16)
    lo = (x - hi.astype(F32)).astype(BF16)
    return hi, lo


def _dot3(a, b):
    ah, al = _split(a)
    bh, bl = _split(b)
    return _dot(ah, bh) + (_dot(ah, bl) + _dot(al, bh))


def _dot2(a, b_bf16):
    ah, al = _split(a)
    return _dot(ah, b_bf16) + _dot(al, b_bf16)


def _sigmoid(x):
    return 1.0 / (1.0 + jnp.exp(-x))


def _normmod(x, gain, scale, shift):
    ms = jnp.mean(x * x, axis=-1, keepdims=True)
    return x * lax.rsqrt(ms + NORM_EPS) * gain * (1.0 + scale) + shift


def _head_ones(width):
    r = lax.broadcasted_iota(jnp.int32, (width, width), 0) // HEAD
    c = lax.broadcasted_iota(jnp.int32, (width, width), 1) // HEAD
    return jnp.where(r == c, 1.0, 0.0).astype(BF16)


def _ada_kernel(a_ref, w_ref, b_ref, o_ref):
    a = a_ref[...]
    o_ref[0] = _dot3(a * _sigmoid(a), w_ref[0]) + b_ref[0]


def _ada(act, ada_w, ada_b):
    L, D, N6 = ada_w.shape
    R = act.shape[0]
    tn = _tile(1536, N6)
    return pl.pallas_call(
        _ada_kernel,
        grid=(L, N6 // tn),
        in_specs=[pl.BlockSpec((R, D), lambda l, j: (0, 0)),
                  pl.BlockSpec((1, D, tn), lambda l, j: (l, 0, j)),
                  pl.BlockSpec((1, 1, tn), lambda l, j: (l, 0, j))],
        out_specs=pl.BlockSpec((1, R, tn), lambda l, j: (l, 0, j)),
        out_shape=jax.ShapeDtypeStruct((L, R, N6), F32),
        compiler_params=_params("parallel", "parallel"),
        name="ada",
    )(act, ada_w, ada_b.reshape(L, 1, N6))


def _in_kernel(x_ref, mod_ref, g_ref, w_ref, kn_ref, cos_ref, sin_ref, pr_ref, pq_ref, k_ref, v_ref, pg_ref, *, nc):
    m = mod_ref[0]
    h = _normmod(x_ref[...], g_ref[...], m[1:2], m[0:1]).astype(BF16)
    col = 0
    for ref, n in ((pr_ref, R_COLS), (pq_ref, ATT_W), (None, 2 * KV_W), (pg_ref, pg_ref.shape[-1])):
        if ref is None:
            in_ctx = pl.program_id(0) * x_ref.shape[0] < nc
            _store_kv(_dot(h, w_ref[:, col:col + n]), in_ctx, kn_ref, cos_ref, sin_ref, k_ref, v_ref)
        else:
            for c0 in range(0, n, 256):
                ref[:, c0:c0 + 256] = _dot(h, w_ref[:, col + c0:col + c0 + 256])
        col += n


def _mod_index(tm, nc, seq, nb):
    def index(i):
        start = i * tm
        return (jnp.where(start < nc, nb, (start - nc) // seq), 0, 0)
    return index


def _in_proj(xf, mod, gain, w_bf16, k_norm, cos, sin_signed, nc, seq, nb):
    N, D = xf.shape
    tm = _tile(512, nc, seq)
    npos = seq // tm
    pos = lambda i: ((jnp.maximum(i * tm - nc, 0) // tm) % npos, 0)
    rows = lambda w: pl.BlockSpec((tm, w), lambda i: (i, 0))
    heads = lambda w: pl.BlockSpec((KV_HEADS, tm, w), lambda i: (0, i, 0))
    flat = lambda w: jax.ShapeDtypeStruct((N, w), F32)
    per_head = lambda w: jax.ShapeDtypeStruct((KV_HEADS, N, w), BF16)
    return pl.pallas_call(
        functools.partial(_in_kernel, nc=nc),
        grid=(N // tm,),
        in_specs=[rows(D),
                  pl.BlockSpec((1, 8, D), _mod_index(tm, nc, seq, nb)),
                  pl.BlockSpec((1, D), lambda i: (0, 0)),
                  pl.BlockSpec(w_bf16.shape, lambda i: (0, 0)),
                  pl.BlockSpec((1, KV_W), lambda i: (0, 0)),
                  pl.BlockSpec((tm, KV_W), pos),
                  pl.BlockSpec((tm, KV_W), pos)],
        out_specs=[rows(R_COLS), rows(ATT_W), heads(HEAD), heads(2 * HEAD), rows(2 * D)],
        out_shape=[flat(R_COLS), flat(ATT_W), per_head(HEAD), per_head(2 * HEAD), flat(2 * D)],
        compiler_params=_params("parallel"),
        name="in_proj",
    )(xf, mod, gain.reshape(1, D), w_bf16, jnp.tile(k_norm, KV_HEADS).reshape(1, KV_W), cos[:, :KV_W],
      sin_signed[:, :KV_W])


def _prep_kernel(p_ref, prev_ref, next_ref, mu_ref, w0_ref, w2_ref, a0_ref, a2_ref, g2_ref, kk_ref, ka_ref,
                 rk_ref, r_ref, v_ref, nkk_ref, g_ref, bonus_ref, lw_ref, keys_ref, b_ref, *, tt, nc, ctx, seq):
    i = pl.program_id(0)
    start = i * tt
    in_ctx = start < nc
    pos = jnp.where(in_ctx, start % ctx, (start - nc) % seq)
    seg = jnp.where(in_ctx, ctx, seq)
    first = pos == 0
    last = pos + tt == seg
    p = p_ref[...]
    rows = lax.broadcasted_iota(jnp.int32, (tt, 1), 0)
    prev_row = jnp.where(first, 0.0, prev_ref[7:8, :])
    next_row = jnp.where(last, 0.0, next_ref[0:1, :])
    prev = jnp.where(rows == 0, prev_row, pltpu.roll(p, 1, 0))
    nxt = jnp.where(rows == tt - 1, next_row, pltpu.roll(p, tt - 1, 0))
    ps = p + mu_ref[0:1, :] * (prev - p) + mu_ref[1:2, :] * (nxt - p)
    W = RWKV_W
    r = ps[:, 0:W]
    k = ps[:, W:2 * W]
    v = ps[:, 2 * W:3 * W]
    wa = ps[:, 3 * W:3 * W + LORA_W]
    gd = ps[:, 3 * W + LORA_W:]
    ones = _head_ones(W)
    kkf = k * kk_ref[...]
    nrm = jnp.sqrt(_dot2(kkf * kkf, ones))
    kk = kkf / jnp.maximum(nrm, 1e-12)
    tw = jnp.tanh(wa)
    ksum = jnp.zeros_like(k)
    for d in range(2):
        z = w0_ref[d:d + 1, :] + _dot3(tw, w2_ref[d])
        lw_ref[d] = -math.exp(-0.5) * _sigmoid(z)
        a = _sigmoid(a0_ref[d:d + 1, :] + _dot3(wa, a2_ref[d]))
        keys = k * (1.0 + (a - 1.0) * ka_ref[...])
        keys_ref[d] = keys
        b_ref[d] = kk * a
        ksum = ksum + keys
    r_ref[...] = r
    v_ref[...] = v
    nkk_ref[...] = -kk
    g_ref[...] = _dot3(_sigmoid(gd), g2_ref[...])
    bonus_ref[...] = _dot2(r * (0.5 * ksum) * rk_ref[...], ones) * v


def _rwkv_prep(p_r, mu, w0, w2, a0, a2, g2, k_k, k_a, r_k, nc, ctx, seq):
    N = p_r.shape[0]
    W = RWKV_W
    tt = _tile(256, ctx, seq)
    nblk8 = N // 8
    zeros = jnp.zeros((2, 64, W), F32)
    w2p = jnp.concatenate([w2, zeros], axis=1)
    a2p = jnp.concatenate([zeros, a2], axis=1)
    row = lambda t: t.reshape(1, W)
    full = lambda a: pl.BlockSpec(a.shape, lambda i: (0,) * a.ndim)
    consts = (mu, w0, w2p, a0, a2p, g2, row(k_k), row(k_a), row(r_k))
    one = jax.ShapeDtypeStruct((N, W), F32)
    two = jax.ShapeDtypeStruct((2, N, W), F32)
    s1 = pl.BlockSpec((tt, W), lambda i: (i, 0))
    s2 = pl.BlockSpec((2, tt, W), lambda i: (0, i, 0))
    return pl.pallas_call(
        functools.partial(_prep_kernel, tt=tt, nc=nc, ctx=ctx, seq=seq),
        grid=(N // tt,),
        in_specs=[pl.BlockSpec((tt, R_COLS), lambda i: (i, 0)),
                  pl.BlockSpec((8, R_COLS), lambda i: (jnp.maximum(i * (tt // 8) - 1, 0), 0)),
                  pl.BlockSpec((8, R_COLS), lambda i: (jnp.minimum((i + 1) * (tt // 8), nblk8 - 1), 0))]
                 + [full(a) for a in consts],
        out_specs=[s1, s1, s1, s1, s1, s2, s2, s2],
        out_shape=[one, one, one, one, one, two, two, two],
        compiler_params=_params("parallel"),
        name="rwkv_prep",
    )(p_r, p_r, p_r, *consts)


def _pair_blocks(x):
    left = lax.broadcasted_iota(jnp.int32, x.shape, 1) < HEAD
    zero = jnp.zeros_like(x)
    return jnp.concatenate([jnp.where(left, x, zero), jnp.where(left, zero, x)], axis=0)


def _pair_diag(x):
    left = lax.broadcasted_iota(jnp.int32, (HEAD, LANES), 1) < HEAD
    return jnp.where(left, x[:HEAD], x[HEAD:])


def _scan_pair_kernel(*refs, C, direction, nslot):
    assert C == HEAD
    y_ref, st_ref = refs[6 * nslot], refs[6 * nslot + 1]

    @pl.when(pl.program_id(1) == 0)
    def _():
        st_ref[...] = jnp.zeros_like(st_ref)

    npair = RWKV_HEADS // 2
    row = lax.broadcasted_iota(jnp.int32, (C, C), 0)
    col = lax.broadcasted_iota(jnp.int32, (C, C), 1)
    before = row - col if direction == 0 else col - row
    linc = jnp.where(before >= 0, 1.0, 0.0).astype(BF16)
    row2 = lax.broadcasted_iota(jnp.int32, (C, LANES), 0)
    col2 = lax.broadcasted_iota(jnp.int32, (C, LANES), 1) % C
    before2 = row2 - col2 if direction == 0 else col2 - row2
    strict = before2 > 0
    incl = before2 >= 0
    eye2 = row2 == col2
    eye2b = jnp.where(eye2, 1.0, 0.0).astype(BF16)
    nt = (((1,), (1,)), ((), ()))
    tn = (((0,), (0,)), ((), ()))
    nlev = int(math.log2(C))

    at, rt, bt, kt, bh, kh, vb, g_tot = ([] for _ in range(8))
    for s in range(nslot):
        r_ref, v_ref, a_ref, lw_ref, k_ref, b_ref = refs[6 * s:6 * s + 6]
        lw = lw_ref[0]
        lh, ll = _split(lw)
        lam = _dot(linc, lh) + _dot(linc, ll)
        tot = jnp.sum(lw, axis=0, keepdims=True)
        g_rem = jnp.exp(tot - lam)
        g_inv = jnp.exp(-lam)
        gt = jnp.exp(tot)
        k = k_ref[0]
        b = b_ref[0]
        a_s = a_ref[...] * jnp.exp(lam - lw)
        r_s = r_ref[...] * jnp.exp(lam)
        b_t = (b * g_inv).astype(BF16)
        k_t = (k * g_inv).astype(BF16)
        b_h = (b * g_rem).astype(BF16)
        k_h = (k * g_rem).astype(BF16)
        v_b = v_ref[...].astype(BF16)
        for p in range(npair):
            sl = slice(p * LANES, (p + 1) * LANES)
            for dst, src in ((at, a_s), (rt, r_s), (bt, b_t), (kt, k_t), (bh, b_h), (kh, k_h), (vb, v_b), (g_tot, gt)):
                dst.append(src[:, sl])
    nch = nslot * npair

    ar = [jnp.concatenate([at[c].astype(BF16), rt[c].astype(BF16)], axis=0) for c in range(nch)]
    L = LANES
    xbk = [lax.dot_general(ar[c], jnp.concatenate([_pair_blocks(bt[c]), _pair_blocks(kt[c])], axis=0), nt,
                           preferred_element_type=F32) for c in range(nch)]
    s_pow = [jnp.where(strict, x[:C, :L], 0.0) for x in xbk]
    m_rb = [jnp.where(incl, x[C:, :L], 0.0).astype(BF16) for x in xbk]
    q = [jnp.where(strict, x[:C, L:], 0.0) for x in xbk]
    m_rk = [jnp.where(incl, x[C:, L:], 0.0) for x in xbk]
    ah = list(at)
    for lev in range(nlev):
        for c in range(nch):
            sb = s_pow[c].astype(BF16)
            rhs = [_pair_blocks(ah[c].astype(BF16)), _pair_blocks(q[c].astype(BF16))]
            if lev < nlev - 1:
                rhs.append(_pair_blocks(sb))
            prod = _dot(sb, jnp.concatenate(rhs, axis=1))
            ah[c] = ah[c] + prod[:, :L]
            q[c] = q[c] + prod[:, L:2 * L]
            if lev < nlev - 1:
                s_pow[c] = prod[:, 2 * L:]
    ahb = [x.astype(BF16) for x in ah]
    qb = [x.astype(BF16) for x in q]
    raq = [_dot(m_rb[c], jnp.concatenate([_pair_blocks(ahb[c]), _pair_blocks(qb[c])], axis=1))
           for c in range(nch)]
    baq = [lax.dot_general(bh[c], jnp.concatenate([ahb[c], qb[c]], axis=1), tn, preferred_element_type=F32)
           for c in range(nch)]
    k_tr = [_pair_diag(lax.dot_general(kh[c], eye2b, tn, preferred_element_type=F32)) for c in range(nch)]
    for c in range(nch):
        s, p = divmod(c, npair)
        r_hat = rt[c] + raq[c][:, :L]
        g_mat = jnp.where(eye2, g_tot[c], 0.0) + _pair_diag(baq[c][:, :L])
        v2 = _pair_blocks(vb[c])
        y_in = _dot((m_rk[c] + raq[c][:, L:]).astype(BF16), v2)
        h_mat = _dot((_pair_diag(baq[c][:, L:]) + k_tr[c]).astype(BF16), v2)
        sh, sl_ = _split(st_ref[c])
        sh, sl_ = _pair_blocks(sh), _pair_blocks(sl_)
        gh, gl = _split(g_mat)
        gs = _dot(gh, jnp.concatenate([sh, sl_], axis=1))
        st_ref[c] = gs[:, :L] + (gs[:, L:] + _dot(gl, sh)) + h_mat
        y_ref[s, :, p * LANES:(p + 1) * LANES] = _dot(r_hat.astype(BF16), sh) + y_in


def _wkv_scan_pairs(r, v, nkk, lw, keys, b, nb, ctx, seq):
    N, W = r.shape
    C = SCAN_CHUNK
    ncc, nlc = ctx // C, seq // C
    nslot = math.gcd(SCAN_BATCHES_PER_STEP, nb)
    hb = nb // nslot

    def blk(d, j, bi, nbat):
        jc = j if d == 0 else ncc - 1 - j
        jl = j - ncc if d == 0 else nlc - 1 - (j - ncc)
        return jnp.where(j < ncc, bi * ncc + jc, nbat * ncc + bi * nlc + jl)

    def one(d):
        s1 = lambda s: pl.BlockSpec((C, W), lambda i, j: (blk(d, j, s * hb + i, nb), 0))
        s2 = lambda s: pl.BlockSpec((1, C, W), lambda i, j: (d, blk(d, j, s * hb + i, nb), 0))
        in_specs = [spec for s in range(nslot) for spec in (s1(s), s1(s), s1(s), s2(s), s2(s), s2(s))]
        return pl.pallas_call(
            functools.partial(_scan_pair_kernel, C=C, direction=d, nslot=nslot),
            grid=(hb, ncc + nlc),
            in_specs=in_specs,
            out_specs=pl.BlockSpec((nslot, C, W), lambda i, j: (0, blk(d, j, i, hb), 0)),
            out_shape=jax.ShapeDtypeStruct((nslot, N // nslot, W), F32),
            scratch_shapes=[pltpu.VMEM((nslot * RWKV_HEADS // 2, HEAD, LANES), F32)],
            compiler_params=_params("parallel", "arbitrary"),
            name="wkv_scan",
        )(*([r, v, nkk, lw, keys, b] * nslot))

    return one(0), one(1)


def _slot_rows(tm, nc, nl, nslot):
    def index(i):
        start = i * tm
        in_ctx = start < nc
        rel = jnp.where(in_ctx, start, start - nc)
        per = jnp.where(in_ctx, nc // nslot, nl // nslot)
        local = jnp.where(in_ctx, 0, nc // nslot) + rel % per
        return (rel // per, local // tm, 0)
    return index


def _rope(t, cos, sin_signed):
    w = t.shape[-1]
    lane = lax.broadcasted_iota(jnp.int32, t.shape, 1)
    swapped = jnp.where(lane % 2 == 0, pltpu.roll(t, w - 1, 1), pltpu.roll(t, 1, 1))
    return t * cos + swapped * sin_signed


def _head_rms(t, gain):
    ms = _dot2(t * t, _head_ones(t.shape[-1])) * (1.0 / HEAD)
    return t * lax.rsqrt(ms + NORM_EPS) * gain


def _store_kv(p, in_ctx, kn_ref, cos_ref, sin_ref, k_ref, v_ref):
    kn = _head_rms(p[:, :KV_W], kn_ref[...])
    kr = _rope(kn, cos_ref[...], sin_ref[...])
    k = jnp.where(in_ctx, kn, kr).astype(BF16)
    v = p[:, KV_W:].astype(BF16)
    one_col = jnp.where(lax.broadcasted_iota(jnp.int32, (p.shape[0], HEAD), 1) == 0, 1.0, 0.0).astype(BF16)
    for h in range(KV_HEADS):
        k_ref[h] = k[:, h * HEAD:(h + 1) * HEAD]
        v_ref[h] = jnp.concatenate([v[:, h * HEAD:(h + 1) * HEAD], one_col], axis=1)


def _attn_kernel(*refs, nseg, rope):
    q_ref, qn_ref = refs[0], refs[1]
    pos = 2
    if rope:
        cos_ref, sin_ref = refs[2], refs[3]
        pos = 4
    kv_refs = refs[pos:pos + 2 * nseg]
    o_ref = refs[pos + 2 * nseg]
    q = _head_rms(q_ref[...], qn_ref[...])
    if rope:
        q = _rope(q, cos_ref[...], sin_ref[...])
    q = (q * (HEAD ** -0.5 * math.log2(math.e))).astype(BF16)
    outs = []
    for g in range(GROUPS):
        qh = q[:, g * HEAD:(g + 1) * HEAD]
        s = [lax.dot_general(qh, kv_refs[2 * i][0], (((1,), (1,)), ((), ())), preferred_element_type=F32)
             for i in range(nseg)]
        m = s[0].max(axis=-1, keepdims=True)
        for si in s[1:]:
            m = jnp.maximum(m, si.max(axis=-1, keepdims=True))
        acc = None
        for i in range(nseg):
            pv = _dot(jnp.exp2(s[i] - m).astype(BF16), kv_refs[2 * i + 1][0])
            acc = pv if acc is None else acc + pv
        outs.append(acc[:, :HEAD] / acc[:, HEAD:HEAD + 1])
    o_ref[...] = jnp.concatenate(outs, axis=1)


def _attention(p_q, k, v, q_norm, cos, sin_signed, nb, nc, ctx, seq, latent):
    qlen = seq if latent else ctx
    tq = _tile(1024, qlen)
    nqt = qlen // tq
    qw = GROUPS * HEAD
    qoff = nc // tq if latent else 0
    qmap = lambda bi, g, t: (qoff + bi * nqt + t, g)
    in_specs = [pl.BlockSpec((tq, qw), qmap), pl.BlockSpec((1, qw), lambda bi, g, t: (0, 0))]
    args = [p_q, jnp.tile(q_norm, GROUPS).reshape(1, qw)]
    ctx_spec = lambda a: pl.BlockSpec((1, ctx, a.shape[-1]), lambda bi, g, t: (g, bi, 0))
    if latent:
        in_specs += [pl.BlockSpec((tq, qw), lambda bi, g, t: (t, 0))] * 2
        args += [cos, sin_signed]
        lat_spec = lambda a: pl.BlockSpec((1, seq, a.shape[-1]), lambda bi, g, t: (g, nc // seq + bi, 0))
        in_specs += [lat_spec(k), lat_spec(v), ctx_spec(k), ctx_spec(v)]
        args += [k, v, k, v]
    else:
        in_specs += [ctx_spec(k), ctx_spec(v)]
        args += [k, v]
    return pl.pallas_call(
        functools.partial(_attn_kernel, nseg=2 if latent else 1, rope=latent),
        grid=(nb, KV_HEADS, nqt),
        in_specs=in_specs,
        out_specs=pl.BlockSpec((tq, qw), lambda bi, g, t: (bi * nqt + t, g)),
        out_shape=jax.ShapeDtypeStruct((nb * qlen, ATT_W), F32),
        compiler_params=_params("parallel", "parallel", "parallel"),
        name="attn_lat" if latent else "attn_ctx",
    )(*args)


def _merge_kernel(x_ref, mod_ref, yf_ref, yb_ref, bonus_ref, g_ref, yc_ref, yl_ref, pg_ref, lw_ref, lb_ref, wpa_ref,
                  wpb_ref, wo_ref, o_ref, *, nct):
    D = x_ref.shape[-1]
    y_att = yl_ref[...] if nct == 0 else jnp.where(pl.program_id(0) < nct, yc_ref[...], yl_ref[...])
    ones = _head_ones(RWKV_W)
    y = yf_ref[0] + yb_ref[0]
    mean = _dot2(y, ones) * (1.0 / HEAD)
    yc = y - mean
    var = _dot2(yc * yc, ones) * (1.0 / HEAD)
    yn = yc * lax.rsqrt(var + LNX_EPS) * lw_ref[...] + lb_ref[...]
    y_rwkv = (yn + bonus_ref[...]) * g_ref[...]
    pg = pg_ref[...]
    mix = (_sigmoid(pg[:, :D]) * _dot(y_rwkv.astype(BF16), wpa_ref[...])
           + _sigmoid(pg[:, D:]) * _dot(y_att.astype(BF16), wpb_ref[...]))
    o_ref[...] = x_ref[...] + mod_ref[0][2:3] * _dot(mix.astype(BF16), wo_ref[...])


def _merge(xf, mod, y_fwd, y_bwd, bonus, g, y_ctx, y_lat, p_g, lnx_w, lnx_b, w_pa, w_pb, w_o, nc, seq, nb, row0):
    N, D = xf.shape
    W = RWKV_W
    tm = _tile(512, nc // y_fwd.shape[0], seq)
    off = row0 // tm
    n_out = N - row0
    mod_idx = _mod_index(tm, nc, seq, nb)
    rows = lambda w: pl.BlockSpec((tm, w), lambda i: (i + off, 0))
    slot_index = _slot_rows(tm, nc, N - nc, y_fwd.shape[0])
    scan_rows = pl.BlockSpec((1, tm, W), lambda i: slot_index(i + off))
    full = lambda a: pl.BlockSpec(a.shape, lambda i: (0,) * a.ndim)
    consts = (lnx_w.reshape(1, W), lnx_b.reshape(1, W), w_pa, w_pb, w_o)
    nct = (nc - row0) // tm
    att_ctx = pl.BlockSpec((tm, W), lambda i: (jnp.minimum(i, max(nct - 1, 0)), 0))
    att_lat = pl.BlockSpec((tm, W), lambda i: (jnp.maximum(i - nct, 0), 0))
    return pl.pallas_call(
        functools.partial(_merge_kernel, nct=nct),
        grid=(n_out // tm,),
        in_specs=[rows(D), pl.BlockSpec((1, 8, D), lambda i: mod_idx(i + off)),
                  scan_rows, scan_rows, rows(W), rows(W), att_ctx, att_lat, rows(2 * D)]
                 + [full(a) for a in consts],
        out_specs=pl.BlockSpec((tm, D), lambda i: (i, 0)),
        out_shape=jax.ShapeDtypeStruct((n_out, D), F32),
        compiler_params=_params("parallel"),
        name="merge",
    )(xf, mod, y_fwd, y_bwd, bonus, g, y_ctx, y_lat, p_g, *consts)


def _ffn_kernel(x_ref, mod_ref, g_ref, wg_ref, wu_ref, wd_ref, o_ref, h_ref, acc_ref):
    f = pl.program_id(1)

    @pl.when(f == 0)
    def _():
        m = mod_ref[0]
        h_ref[...] = _normmod(x_ref[...], g_ref[...], m[4:5], m[3:4]).astype(BF16)
        acc_ref[...] = jnp.zeros_like(acc_ref)

    h = h_ref[...]
    a = _dot(h, wg_ref[...])
    z = a * _sigmoid(a) * _dot(h, wu_ref[...])
    acc_ref[...] += _dot(z.astype(BF16), wd_ref[...])

    @pl.when(f == pl.num_programs(1) - 1)
    def _():
        o_ref[...] = x_ref[...] + mod_ref[0][5:6] * acc_ref[...]


def _ffn(xf, mod, gain, wg, wu, wd, nc, seq, nb):
    N, D = xf.shape
    F = wg.shape[-1]
    tm = _tile(1024, nc, seq)
    tf = _ffn_tile(F)
    return pl.pallas_call(
        _ffn_kernel,
        grid=(N // tm, F // tf),
        in_specs=[pl.BlockSpec((tm, D), lambda i, f: (i, 0)),
                  pl.BlockSpec((1, 8, D), lambda i, f: _mod_index(tm, nc, seq, nb)(i)),
                  pl.BlockSpec((1, D), lambda i, f: (0, 0)),
                  pl.BlockSpec((D, tf), lambda i, f: (0, f)),
                  pl.BlockSpec((D, tf), lambda i, f: (0, f)),
                  pl.BlockSpec((tf, D), lambda i, f: (f, 0))],
        out_specs=pl.BlockSpec((tm, D), lambda i, f: (i, 0)),
        out_shape=jax.ShapeDtypeStruct((N, D), F32),
        scratch_shapes=[pltpu.VMEM((tm, D), BF16), pltpu.VMEM((tm, D), F32)],
        compiler_params=_params("parallel", "arbitrary"),
        name="ffn",
    )(xf, mod, gain.reshape(1, D), wg, wu, wd)


R_E1, R_E2, R_RANK1, R_RANK2, R_W1, R_W2 = range(6)


def _router_kernel(x_ref, mod_ref, g_ref, wr_ref, route_ref, cnt_ref, carry_ref):
    @pl.when(pl.program_id(0) == 0)
    def _():
        carry_ref[...] = jnp.zeros_like(carry_ref)

    m = mod_ref[0]
    h = _normmod(x_ref[...], g_ref[...], m[4:5], m[3:4])
    tm = h.shape[0]
    lane = lax.broadcasted_iota(jnp.int32, (tm, LANES), 1).astype(F32)
    logits = jnp.where(lane < N_EXPERTS, _dot3(h, wr_ref[...]), NEG_BIG)
    m1 = logits.max(axis=-1, keepdims=True)
    i1 = jnp.where(logits == m1, lane, float(LANES)).min(axis=-1, keepdims=True)
    rest = jnp.where(lane == i1, NEG_BIG, logits)
    m2 = rest.max(axis=-1, keepdims=True)
    i2 = jnp.where(rest == m2, lane, float(LANES)).min(axis=-1, keepdims=True)
    e2 = jnp.exp(m2 - m1)
    w1 = 1.0 / (1.0 + e2)
    w2 = e2 / (1.0 + e2)
    onehot = jnp.where(lane == i1, 1.0, jnp.where(lane == i2, 1.0, 0.0))
    earlier = (lax.broadcasted_iota(jnp.int32, (tm, tm), 0) > lax.broadcasted_iota(jnp.int32, (tm, tm), 1))
    before = _dot(jnp.where(earlier, 1.0, 0.0).astype(BF16), onehot.astype(BF16)) + carry_ref[...]
    rank1 = jnp.where(lane == i1, before, 0.0).sum(axis=-1, keepdims=True)
    rank2 = jnp.where(lane == i2, before, 0.0).sum(axis=-1, keepdims=True)
    carry_ref[...] += onehot.sum(axis=0, keepdims=True)
    cnt_ref[...] = jnp.broadcast_to(carry_ref[...], cnt_ref.shape)
    rec = jnp.zeros_like(lane)
    for idx, val in ((R_E1, i1), (R_E2, i2), (R_RANK1, rank1), (R_RANK2, rank2), (R_W1, w1), (R_W2, w2)):
        rec = jnp.where(lane == float(idx), val, rec)
    route_ref[...] = rec


def _router(xl, mod, gain, router, seq):
    N, D = xl.shape
    tm = _tile(512, seq)
    wr = jnp.zeros((D, LANES), F32).at[:, :N_EXPERTS].set(router)
    return pl.pallas_call(
        _router_kernel,
        grid=(N // tm,),
        in_specs=[pl.BlockSpec((tm, D), lambda i: (i, 0)),
                  pl.BlockSpec((1, 8, D), lambda i: ((i * tm) // seq, 0, 0)),
                  pl.BlockSpec((1, D), lambda i: (0, 0)),
                  pl.BlockSpec((D, LANES), lambda i: (0, 0))],
        out_specs=[pl.BlockSpec((tm, LANES), lambda i: (i, 0)), pl.BlockSpec((8, LANES), lambda i: (0, 0))],
        out_shape=[jax.ShapeDtypeStruct((N, LANES), F32), jax.ShapeDtypeStruct((8, LANES), F32)],
        scratch_shapes=[pltpu.VMEM((1, LANES), F32)],
        compiler_params=_params("arbitrary"),
        name="router",
    )(xl, mod, gain.reshape(1, D), wr)


def _all_rows(n, copies):
    def start(r, c):
        for copy in copies:
            copy(r).start()
        return c

    def wait(r, c):
        for copy in copies:
            copy(r).wait()
        return c

    lax.fori_loop(0, n, start, 0, unroll=8)
    lax.fori_loop(0, n, wait, 0, unroll=8)


def _dispatch_kernel(dest_ref, x_ref, mod_ref, g_ref, xs_in_ref, xs_ref, h_ref, sem):
    del xs_in_ref
    tm = x_ref.shape[0]
    n = tm * pl.num_programs(0)
    base = pl.program_id(0) * tm
    m = mod_ref[0]
    h_ref[...] = _normmod(x_ref[...], g_ref[...], m[4:5], m[3:4])
    _all_rows(tm, [lambda r, off=off: pltpu.make_async_copy(
        h_ref.at[pl.ds(r, 1)], xs_ref.at[pl.ds(dest_ref[off + base + r], 1)], sem) for off in (0, n)])


def _dispatch(dest, xl, mod, gain, xs, seq):
    N, D = xl.shape
    tm = _tile(512, seq)
    return pl.pallas_call(
        _dispatch_kernel,
        grid_spec=pltpu.PrefetchScalarGridSpec(
            num_scalar_prefetch=1,
            grid=(N // tm,),
            in_specs=[pl.BlockSpec((tm, D), lambda i, dest: (i, 0)),
                      pl.BlockSpec((1, 8, D), lambda i, dest: ((i * tm) // seq, 0, 0)),
                      pl.BlockSpec((1, D), lambda i, dest: (0, 0)),
                      pl.BlockSpec(memory_space=pl.ANY)],
            out_specs=pl.BlockSpec(memory_space=pl.ANY),
            scratch_shapes=[pltpu.VMEM((tm, D), F32), pltpu.SemaphoreType.DMA]),
        out_shape=jax.ShapeDtypeStruct(xs.shape, xs.dtype),
        input_output_aliases={4: 0},
        compiler_params=_params("arbitrary"),
        name="moe_dispatch",
    )(dest, xl, mod, gain.reshape(1, D), xs)


def _expert_kernel(tbl_ref, xs_ref, wg_ref, wu_ref, wd_ref, ys_ref, hb_ref, acc_ref, *, nblk):
    i = pl.program_id(0)
    f = pl.program_id(1)
    used = i < tbl_ref[nblk]
    last = f == pl.num_programs(1) - 1

    @pl.when(used & (f == 0))
    def _():
        hb_ref[...] = xs_ref[...].astype(BF16)
        acc_ref[...] = jnp.zeros_like(acc_ref)

    @pl.when(used)
    def _():
        h = hb_ref[...]
        a = _dot(h, wg_ref[0])
        z = a * _sigmoid(a) * _dot(h, wu_ref[0])
        acc_ref[...] += _dot(z.astype(BF16), wd_ref[0])

    @pl.when(used & last)
    def _():
        ys_ref[...] = acc_ref[...]

    @pl.when(jnp.logical_not(used) & last)
    def _():
        ys_ref[...] = jnp.zeros_like(ys_ref)


def _experts(tbl, xs, wg, wu, wd, blk):
    R, D = xs.shape
    F = wg.shape[-1]
    nblk = R // blk
    tf = _ffn_tile(F)
    return pl.pallas_call(
        functools.partial(_expert_kernel, nblk=nblk),
        grid_spec=pltpu.PrefetchScalarGridSpec(
            num_scalar_prefetch=1,
            grid=(nblk, F // tf),
            in_specs=[pl.BlockSpec((blk, D), lambda i, f, tbl: (i, 0)),
                      pl.BlockSpec((1, D, tf), lambda i, f, tbl: (tbl[i], 0, f)),
                      pl.BlockSpec((1, D, tf), lambda i, f, tbl: (tbl[i], 0, f)),
                      pl.BlockSpec((1, tf, D), lambda i, f, tbl: (tbl[i], f, 0))],
            out_specs=pl.BlockSpec((blk, D), lambda i, f, tbl: (i, 0)),
            scratch_shapes=[pltpu.VMEM((blk, D), BF16), pltpu.VMEM((blk, D), F32)]),
        out_shape=jax.ShapeDtypeStruct((R, D), F32),
        compiler_params=_params("parallel", "arbitrary"),
        name="moe_experts",
    )(tbl, xs, wg, wu, wd)


def _combine_kernel(dest_ref, x_ref, route_ref, mod_ref, fn_ref, ys_ref, o_ref, buf_ref, sem):
    tm = x_ref.shape[0]
    n = tm * pl.num_programs(0)
    base = pl.program_id(0) * tm
    _all_rows(tm, [lambda r, slot=slot: pltpu.make_async_copy(
        ys_ref.at[pl.ds(dest_ref[slot * n + base + r], 1)], buf_ref.at[pl.ds(slot * tm + r, 1)], sem)
        for slot in (0, 1)])
    mix = route_ref[:, R_W1:R_W1 + 1] * buf_ref[:tm] + route_ref[:, R_W2:R_W2 + 1] * buf_ref[tm:]
    y = x_ref[...] + mod_ref[0][5:6] * mix
    ms = jnp.mean(y * y, axis=-1, keepdims=True)
    o_ref[...] = y * lax.rsqrt(ms + NORM_EPS) * fn_ref[...]


def _combine(dest, acc, route, mod, final_norm, ys, seq):
    N, D = acc.shape
    tm = _tile(512, seq)
    return pl.pallas_call(
        _combine_kernel,
        grid_spec=pltpu.PrefetchScalarGridSpec(
            num_scalar_prefetch=1,
            grid=(N // tm,),
            in_specs=[pl.BlockSpec((tm, D), lambda i, dest: (i, 0)),
                      pl.BlockSpec((tm, LANES), lambda i, dest: (i, 0)),
                      pl.BlockSpec((1, 8, D), lambda i, dest: ((i * tm) // seq, 0, 0)),
                      pl.BlockSpec((1, D), lambda i, dest: (0, 0)),
                      pl.BlockSpec(memory_space=pl.ANY)],
            out_specs=pl.BlockSpec((tm, D), lambda i, dest: (i, 0)),
            scratch_shapes=[pltpu.VMEM((2 * tm, D), F32), pltpu.SemaphoreType.DMA]),
        out_shape=jax.ShapeDtypeStruct((N, D), F32),
        compiler_params=_params("arbitrary"),
        name="moe_combine",
    )(dest, acc, route, mod, final_norm.reshape(1, D), ys)


def _moe_final(xl, mod, gain, router, wg, wu, wd, final_norm, seq):
    N, D = xl.shape
    blk = MOE_BLOCK_ROWS
    route, counts = _router(xl, mod, gain, router, seq)
    counts = counts[0, :N_EXPERTS].astype(jnp.int32)
    padded = (counts + blk - 1) // blk * blk
    ends = jnp.cumsum(padded)
    starts = ends - padded
    nblk = -(-(2 * N) // blk) + N_EXPERTS
    blk_expert = jnp.minimum(jnp.searchsorted(ends, jnp.arange(nblk, dtype=jnp.int32) * blk, side='right'),
                             N_EXPERTS - 1).astype(jnp.int32)
    tbl = jnp.concatenate([blk_expert, (ends[-1:] // blk).astype(jnp.int32)])
    dest = jnp.concatenate([starts[route[:, e].astype(jnp.int32)] + route[:, r].astype(jnp.int32)
                            for e, r in ((R_E1, R_RANK1), (R_E2, R_RANK2))])
    xs = _dispatch(dest, xl, mod, gain, jnp.zeros((nblk * blk, D), F32), seq)
    ys = _experts(tbl, xs, wg, wu, wd, blk)
    return _combine(dest, xl, route, mod, final_norm, ys, seq)


def _rope_tables(seq):
    rows = seq // GRID_W
    row, col = jnp.meshgrid(jnp.arange(rows), jnp.arange(GRID_W), indexing='ij')
    axis = HEAD // 2
    inv = ROPE_THETA ** (-jnp.arange(0, axis, 2, dtype=F32) / axis)
    ang = jnp.concatenate([row.reshape(-1, 1).astype(F32) * inv, col.reshape(-1, 1).astype(F32) * inv], axis=-1)
    cos = jnp.repeat(jnp.cos(ang), 2, axis=-1)
    sin = jnp.repeat(jnp.sin(ang), 2, axis=-1) * jnp.tile(jnp.array([-1.0, 1.0], F32), HEAD // 2)
    return jnp.tile(cos, (1, GROUPS)), jnp.tile(sin, (1, GROUPS))


def kernel(x, c, ctx, c_ctx, ada_w, ada_b, norm1, norm2, w_in, shift_mu, rwkv_w0, rwkv_w2, rwkv_a0, rwkv_a2, rwkv_g2, rwkv_kk, rwkv_ka, rwkv_rk, lnx_w, lnx_b, q_norm, k_norm, w_pa, w_pb, w_o, ffn_wg, ffn_wu, ffn_wd, router, moe_wg, moe_wu, moe_wd, final_norm):
    B, T, D = x.shape
    CX = ctx.shape[1]
    depth = ada_w.shape[0]
    assert depth == 2 and ffn_wg.shape[0] == 1 and router.shape[0] == 1
    nc = B * CX
    bf = lambda w: w.astype(BF16)

    nrow = -(-(B + 1) // 8) * 8
    act = jnp.zeros((nrow, D), F32).at[:B].set(c).at[B].set(c_ctx)
    mod = _ada(act, ada_w, ada_b).reshape(depth, nrow, 6, D)
    mod = jnp.concatenate([mod, jnp.zeros((depth, nrow, 2, D), F32)], axis=2)

    cos, sin_signed = _rope_tables(T)
    xf = jnp.concatenate([ctx.reshape(nc, D), x.reshape(B * T, D)], axis=0)
    out = None
    for l in range(depth):
        last = l == depth - 1
        p_r, p_q, k_att, v_att, p_g = _in_proj(xf, mod[l], norm1[l], bf(w_in[l]), k_norm[l], cos, sin_signed, nc, T, B)
        r, v, nkk, g, bonus, lw, keys, b = _rwkv_prep(
            p_r, shift_mu[l], rwkv_w0[l], rwkv_w2[l], rwkv_a0[l], rwkv_a2[l], rwkv_g2[l], rwkv_kk[l], rwkv_ka[l],
            rwkv_rk[l].reshape(-1), nc, CX, T)
        y_fwd, y_bwd = _wkv_scan_pairs(r, v, nkk, lw, keys, b, B, CX, T)
        y_lat = _attention(p_q, k_att, v_att, q_norm[l], cos, sin_signed, B, nc, CX, T, latent=True)
        y_ctx = y_lat if last else _attention(p_q, k_att, v_att, q_norm[l], cos, sin_signed, B, nc, CX, T,
                                              latent=False)
        xm = _merge(xf, mod[l], y_fwd, y_bwd, bonus, g, y_ctx, y_lat, p_g, lnx_w[l], lnx_b[l], bf(w_pa[l]),
                    bf(w_pb[l]), bf(w_o[l]), nc, T, B, nc if last else 0)
        if not last:
            xf = _ffn(xm, mod[l], norm2[l], bf(ffn_wg[l // 2]), bf(ffn_wu[l // 2]), bf(ffn_wd[l // 2]), nc, T, B)
        else:
            out = _moe_final(xm, mod[l], norm2[l], router[l // 2], bf(moe_wg[l // 2]), bf(moe_wu[l // 2]),
                             bf(moe_wd[l // 2]), final_norm, T)
    return out.reshape(B, T, D)
```

```python
import functools
import math

import jax
import jax.numpy as jnp
from jax import lax
from jax.experimental import pallas as pl
from jax.experimental.pallas import tpu as pltpu

F32 = jnp.float32
BF16 = jnp.bfloat16

HEAD = 64
RWKV_HEADS = 8
RWKV_W = RWKV_HEADS * HEAD
LORA_W = 128
ATT_HEADS = 8
KV_HEADS = 2
GROUPS = ATT_HEADS // KV_HEADS
ATT_W = ATT_HEADS * HEAD
KV_W = KV_HEADS * HEAD
R_COLS = 3 * RWKV_W + 64 + 64 + 128
GRID_W = 64
ROPE_THETA = 10000.0
N_EXPERTS = 8
NORM_EPS = 1e-6
LNX_EPS = 64e-5
LANES = 128
NEG_BIG = -1e30
VMEM_LIMIT_BYTES = 56 * 1024 * 1024
SCAN_CHUNK = 64
FFN_TILE = 512
SCAN_BATCHES_PER_STEP = 4
MOE_BLOCK_ROWS = 1024


def _params(*sem):
    return pltpu.CompilerParams(dimension_semantics=sem, vmem_limit_bytes=VMEM_LIMIT_BYTES)


def _tile(pref, *dims):
    t = pref
    while any(d % t for d in dims):
        t //= 2
    return t


def _ffn_tile(f):
    return _tile(FFN_TILE, f)


def _dot(a, b):
    return jnp.dot(a, b, preferred_element_type=F32)


def _split(x):
    hi = x.astype(BF16)
    lo = (x - hi.astype(F32)).astype(BF16)
    return hi, lo


def _dot3(a, b):
    ah, al = _split(a)
    bh, bl = _split(b)
    return _dot(ah, bh) + (_dot(ah, bl) + _dot(al, bh))


def _dot2(a, b_bf16):
    ah, al = _split(a)
    return _dot(ah, b_bf16) + _dot(al, b_bf16)


def _sigmoid(x):
    return 1.0 / (1.0 + jnp.exp(-x))


def _normmod(x, gain, scale, shift):
    ms = jnp.mean(x * x, axis=-1, keepdims=True)
    return x * lax.rsqrt(ms + NORM_EPS) * gain * (1.0 + scale) + shift


def _head_ones(width):
    r = lax.broadcasted_iota(jnp.int32, (width, width), 0) // HEAD
    c = lax.broadcasted_iota(jnp.int32, (width, width), 1) // HEAD
    return jnp.where(r == c, 1.0, 0.0).astype(BF16)


def _ada_kernel(a_ref, w_ref, b_ref, o_ref):
    a = a_ref[...]
    o_ref[0] = _dot3(a * _sigmoid(a), w_ref[0]) + b_ref[0]


def _ada(act, ada_w, ada_b):
    L, D, N6 = ada_w.shape
    R = act.shape[0]
    tn = _tile(1536, N6)
    return pl.pallas_call(
        _ada_kernel,
        grid=(L, N6 // tn),
        in_specs=[pl.BlockSpec((R, D), lambda l, j: (0, 0)),
                  pl.BlockSpec((1, D, tn), lambda l, j: (l, 0, j)),
                  pl.BlockSpec((1, 1, tn), lambda l, j: (l, 0, j))],
        out_specs=pl.BlockSpec((1, R, tn), lambda l, j: (l, 0, j)),
        out_shape=jax.ShapeDtypeStruct((L, R, N6), F32),
        compiler_params=_params("parallel", "parallel"),
        name="ada",
    )(act, ada_w, ada_b.reshape(L, 1, N6))


def _in_kernel(x_ref, mod_ref, g_ref, w_ref, kn_ref, cos_ref, sin_ref, pr_ref, pq_ref, k_ref, v_ref, pg_ref, *, nc):
    m = mod_ref[0]
    h = _normmod(x_ref[...], g_ref[...], m[1:2], m[0:1]).astype(BF16)
    col = 0
    for ref, n in ((pr_ref, R_COLS), (pq_ref, ATT_W), (None, 2 * KV_W), (pg_ref, pg_ref.shape[-1])):
        if ref is None:
            in_ctx = pl.program_id(0) * x_ref.shape[0] < nc
            _store_kv(_dot(h, w_ref[:, col:col + n]), in_ctx, kn_ref, cos_ref, sin_ref, k_ref, v_ref)
        else:
            for c0 in range(0, n, 256):
                ref[:, c0:c0 + 256] = _dot(h, w_ref[:, col + c0:col + c0 + 256])
        col += n


def _mod_index(tm, nc, seq, nb):
    def index(i):
        start = i * tm
        return (jnp.where(start < nc, nb, (start - nc) // seq), 0, 0)
    return index


def _in_proj(xf, mod, gain, w_bf16, k_norm, cos, sin_signed, nc, seq, nb):
    N, D = xf.shape
    tm = _tile(512, nc, seq)
    npos = seq // tm
    pos = lambda i: ((jnp.maximum(i * tm - nc, 0) // tm) % npos, 0)
    rows = lambda w: pl.BlockSpec((tm, w), lambda i: (i, 0))
    heads = lambda w: pl.BlockSpec((KV_HEADS, tm, w), lambda i: (0, i, 0))
    flat = lambda w: jax.ShapeDtypeStruct((N, w), F32)
    per_head = lambda w: jax.ShapeDtypeStruct((KV_HEADS, N, w), BF16)
    return pl.pallas_call(
        functools.partial(_in_kernel, nc=nc),
        grid=(N // tm,),
        in_specs=[rows(D),
                  pl.BlockSpec((1, 8, D), _mod_index(tm, nc, seq, nb)),
                  pl.BlockSpec((1, D), lambda i: (0, 0)),
                  pl.BlockSpec(w_bf16.shape, lambda i: (0, 0)),
                  pl.BlockSpec((1, KV_W), lambda i: (0, 0)),
                  pl.BlockSpec((tm, KV_W), pos),
                  pl.BlockSpec((tm, KV_W), pos)],
        out_specs=[rows(R_COLS), rows(ATT_W), heads(HEAD), heads(2 * HEAD), rows(2 * D)],
        out_shape=[flat(R_COLS), flat(ATT_W), per_head(HEAD), per_head(2 * HEAD), flat(2 * D)],
        compiler_params=_params("parallel"),
        name="in_proj",
    )(xf, mod, gain.reshape(1, D), w_bf16, jnp.tile(k_norm, KV_HEADS).reshape(1, KV_W), cos[:, :KV_W],
      sin_signed[:, :KV_W])


def _prep_kernel(p_ref, prev_ref, next_ref, mu_ref, w0_ref, w2_ref, a0_ref, a2_ref, g2_ref, kk_ref, ka_ref,
                 rk_ref, r_ref, v_ref, nkk_ref, g_ref, bonus_ref, lw_ref, keys_ref, b_ref, *, tt, nc, ctx, seq):
    i = pl.program_id(0)
    start = i * tt
    in_ctx = start < nc
    pos = jnp.where(in_ctx, start % ctx, (start - nc) % seq)
    seg = jnp.where(in_ctx, ctx, seq)
    first = pos == 0
    last = pos + tt == seg
    p = p_ref[...]
    rows = lax.broadcasted_iota(jnp.int32, (tt, 1), 0)
    prev_row = jnp.where(first, 0.0, prev_ref[7:8, :])
    next_row = jnp.where(last, 0.0, next_ref[0:1, :])
    prev = jnp.where(rows == 0, prev_row, pltpu.roll(p, 1, 0))
    nxt = jnp.where(rows == tt - 1, next_row, pltpu.roll(p, tt - 1, 0))
    ps = p + mu_ref[0:1, :] * (prev - p) + mu_ref[1:2, :] * (nxt - p)
    W = RWKV_W
    r = ps[:, 0:W]
    k = ps[:, W:2 * W]
    v = ps[:, 2 * W:3 * W]
    wa = ps[:, 3 * W:3 * W + LORA_W]
    gd = ps[:, 3 * W + LORA_W:]
    ones = _head_ones(W)
    kkf = k * kk_ref[...]
    nrm = jnp.sqrt(_dot2(kkf * kkf, ones))
    kk = kkf / jnp.maximum(nrm, 1e-12)
    tw = jnp.tanh(wa)
    ksum = jnp.zeros_like(k)
    for d in range(2):
        z = w0_ref[d:d + 1, :] + _dot3(tw, w2_ref[d])
        lw_ref[d] = -math.exp(-0.5) * _sigmoid(z)
        a = _sigmoid(a0_ref[d:d + 1, :] + _dot3(wa, a2_ref[d]))
        keys = k * (1.0 + (a - 1.0) * ka_ref[...])
        keys_ref[d] = keys
        b_ref[d] = kk * a
        ksum = ksum + keys
    r_ref[...] = r
    v_ref[...] = v
    nkk_ref[...] = -kk
    g_ref[...] = _dot3(_sigmoid(gd), g2_ref[...])
    bonus_ref[...] = _dot2(r * (0.5 * ksum) * rk_ref[...], ones) * v


def _rwkv_prep(p_r, mu, w0, w2, a0, a2, g2, k_k, k_a, r_k, nc, ctx, seq):
    N = p_r.shape[0]
    W = RWKV_W
    tt = _tile(256, ctx, seq)
    nblk8 = N // 8
    zeros = jnp.zeros((2, 64, W), F32)
    w2p = jnp.concatenate([w2, zeros], axis=1)
    a2p = jnp.concatenate([zeros, a2], axis=1)
    row = lambda t: t.reshape(1, W)
    full = lambda a: pl.BlockSpec(a.shape, lambda i: (0,) * a.ndim)
    consts = (mu, w0, w2p, a0, a2p, g2, row(k_k), row(k_a), row(r_k))
    one = jax.ShapeDtypeStruct((N, W), F32)
    two = jax.ShapeDtypeStruct((2, N, W), F32)
    s1 = pl.BlockSpec((tt, W), lambda i: (i, 0))
    s2 = pl.BlockSpec((2, tt, W), lambda i: (0, i, 0))
    return pl.pallas_call(
        functools.partial(_prep_kernel, tt=tt, nc=nc, ctx=ctx, seq=seq),
        grid=(N // tt,),
        in_specs=[pl.BlockSpec((tt, R_COLS), lambda i: (i, 0)),
                  pl.BlockSpec((8, R_COLS), lambda i: (jnp.maximum(i * (tt // 8) - 1, 0), 0)),
                  pl.BlockSpec((8, R_COLS), lambda i: (jnp.minimum((i + 1) * (tt // 8), nblk8 - 1), 0))]
                 + [full(a) for a in consts],
        out_specs=[s1, s1, s1, s1, s1, s2, s2, s2],
        out_shape=[one, one, one, one, one, two, two, two],
        compiler_params=_params("parallel"),
        name="rwkv_prep",
    )(p_r, p_r, p_r, *consts)


def _pair_blocks(x):
    left = lax.broadcasted_iota(jnp.int32, x.shape, 1) < HEAD
    zero = jnp.zeros_like(x)
    return jnp.concatenate([jnp.where(left, x, zero), jnp.where(left, zero, x)], axis=0)


def _pair_diag(x):
    left = lax.broadcasted_iota(jnp.int32, (HEAD, LANES), 1) < HEAD
    return jnp.where(left, x[:HEAD], x[HEAD:])


def _scan_pair_kernel(*refs, C, direction, nslot):
    assert C == HEAD
    y_ref, st_ref = refs[6 * nslot], refs[6 * nslot + 1]

    @pl.when(pl.program_id(1) == 0)
    def _():
        st_ref[...] = jnp.zeros_like(st_ref)

    npair = RWKV_HEADS // 2
    row = lax.broadcasted_iota(jnp.int32, (C, C), 0)
    col = lax.broadcasted_iota(jnp.int32, (C, C), 1)
    before = row - col if direction == 0 else col - row
    linc = jnp.where(before >= 0, 1.0, 0.0).astype(BF16)
    row2 = lax.broadcasted_iota(jnp.int32, (C, LANES), 0)
    col2 = lax.broadcasted_iota(jnp.int32, (C, LANES), 1) % C
    before2 = row2 - col2 if direction == 0 else col2 - row2
    strict = before2 > 0
    incl = before2 >= 0
    eye2 = row2 == col2
    eye2b = jnp.where(eye2, 1.0, 0.0).astype(BF16)
    nt = (((1,), (1,)), ((), ()))
    tn = (((0,), (0,)), ((), ()))
    nlev = int(math.log2(C))

    at, rt, bt, kt, bh, kh, vb, g_tot = ([] for _ in range(8))
    for s in range(nslot):
        r_ref, v_ref, a_ref, lw_ref, k_ref, b_ref = refs[6 * s:6 * s + 6]
        lw = lw_ref[0]
        lh, ll = _split(lw)
        lam = _dot(linc, lh) + _dot(linc, ll)
        tot = jnp.sum(lw, axis=0, keepdims=True)
        g_rem = jnp.exp(tot - lam)
        g_inv = jnp.exp(-lam)
        gt = jnp.exp(tot)
        k = k_ref[0]
        b = b_ref[0]
        a_s = a_ref[...] * jnp.exp(lam - lw)
        r_s = r_ref[...] * jnp.exp(lam)
        b_t = (b * g_inv).astype(BF16)
        k_t = (k * g_inv).astype(BF16)
        b_h = (b * g_rem).astype(BF16)
        k_h = (k * g_rem).astype(BF16)
        v_b = v_ref[...].astype(BF16)
        for p in range(npair):
            sl = slice(p * LANES, (p + 1) * LANES)
            for dst, src in ((at, a_s), (rt, r_s), (bt, b_t), (kt, k_t), (bh, b_h), (kh, k_h), (vb, v_b), (g_tot, gt)):
                dst.append(src[:, sl])
    nch = nslot * npair

    ar = [jnp.concatenate([at[c].astype(BF16), rt[c].astype(BF16)], axis=0) for c in range(nch)]
    L = LANES
    xbk = [lax.dot_general(ar[c], jnp.concatenate([_pair_blocks(bt[c]), _pair_blocks(kt[c])], axis=0), nt,
                           preferred_element_type=F32) for c in range(nch)]
    s_pow = [jnp.where(strict, x[:C, :L], 0.0) for x in xbk]
    m_rb = [jnp.where(incl, x[C:, :L], 0.0).astype(BF16) for x in xbk]
    q = [jnp.where(strict, x[:C, L:], 0.0) for x in xbk]
    m_rk = [jnp.where(incl, x[C:, L:], 0.0) for x in xbk]
    ah = list(at)
    for lev in range(nlev):
        for c in range(nch):
            sb = s_pow[c].astype(BF16)
            rhs = [_pair_blocks(ah[c].astype(BF16)), _pair_blocks(q[c].astype(BF16))]
            if lev < nlev - 1:
                rhs.append(_pair_blocks(sb))
            prod = _dot(sb, jnp.concatenate(rhs, axis=1))
            ah[c] = ah[c] + prod[:, :L]
            q[c] = q[c] + prod[:, L:2 * L]
            if lev < nlev - 1:
                s_pow[c] = prod[:, 2 * L:]
    ahb = [x.astype(BF16) for x in ah]
    qb = [x.astype(BF16) for x in q]
    raq = [_dot(m_rb[c], jnp.concatenate([_pair_blocks(ahb[c]), _pair_blocks(qb[c])], axis=1))
           for c in range(nch)]
    baq = [lax.dot_general(bh[c], jnp.concatenate([ahb[c], qb[c]], axis=1), tn, preferred_element_type=F32)
           for c in range(nch)]
    k_tr = [_pair_diag(lax.dot_general(kh[c], eye2b, tn, preferred_element_type=F32)) for c in range(nch)]
    for c in range(nch):
        s, p = divmod(c, npair)
        r_hat = rt[c] + raq[c][:, :L]
        g_mat = jnp.where(eye2, g_tot[c], 0.0) + _pair_diag(baq[c][:, :L])
        v2 = _pair_blocks(vb[c])
        y_in = _dot((m_rk[c] + raq[c][:, L:]).astype(BF16), v2)
        h_mat = _dot((_pair_diag(baq[c][:, L:]) + k_tr[c]).astype(BF16), v2)
        sh, sl_ = _split(st_ref[c])
        sh, sl_ = _pair_blocks(sh), _pair_blocks(sl_)
        gh, gl = _split(g_mat)
        gs = _dot(gh, jnp.concatenate([sh, sl_], axis=1))
        st_ref[c] = gs[:, :L] + (gs[:, L:] + _dot(gl, sh)) + h_mat
        y_ref[s, :, p * LANES:(p + 1) * LANES] = _dot(r_hat.astype(BF16), sh) + y_in


def _wkv_scan_pairs(r, v, nkk, lw, keys, b, nb, ctx, seq):
    N, W = r.shape
    C = SCAN_CHUNK
    ncc, nlc = ctx // C, seq // C
    nslot = math.gcd(SCAN_BATCHES_PER_STEP, nb)
    hb = nb // nslot

    def blk(d, j, bi, nbat):
        jc = j if d == 0 else ncc - 1 - j
        jl = j - ncc if d == 0 else nlc - 1 - (j - ncc)
        return jnp.where(j < ncc, bi * ncc + jc, nbat * ncc + bi * nlc + jl)

    def one(d):
        s1 = lambda s: pl.BlockSpec((C, W), lambda i, j: (blk(d, j, s * hb + i, nb), 0))
        s2 = lambda s: pl.BlockSpec((1, C, W), lambda i, j: (d, blk(d, j, s * hb + i, nb), 0))
        in_specs = [spec for s in range(nslot) for spec in (s1(s), s1(s), s1(s), s2(s), s2(s), s2(s))]
        return pl.pallas_call(
            functools.partial(_scan_pair_kernel, C=C, direction=d, nslot=nslot),
            grid=(hb, ncc + nlc),
            in_specs=in_specs,
            out_specs=pl.BlockSpec((nslot, C, W), lambda i, j: (0, blk(d, j, i, hb), 0)),
            out_shape=jax.ShapeDtypeStruct((nslot, N // nslot, W), F32),
            scratch_shapes=[pltpu.VMEM((nslot * RWKV_HEADS // 2, HEAD, LANES), F32)],
            compiler_params=_params("parallel", "arbitrary"),
            name="wkv_scan",
        )(*([r, v, nkk, lw, keys, b] * nslot))

    return one(0), one(1)


def _slot_rows(tm, nc, nl, nslot):
    def index(i):
        start = i * tm
        in_ctx = start < nc
        rel = jnp.where(in_ctx, start, start - nc)
        per = jnp.where(in_ctx, nc // nslot, nl // nslot)
        local = jnp.where(in_ctx, 0, nc // nslot) + rel % per
        return (rel // per, local // tm, 0)
    return index


def _rope(t, cos, sin_signed):
    w = t.shape[-1]
    lane = lax.broadcasted_iota(jnp.int32, t.shape, 1)
    swapped = jnp.where(lane % 2 == 0, pltpu.roll(t, w - 1, 1), pltpu.roll(t, 1, 1))
    return t * cos + swapped * sin_signed


def _head_rms(t, gain):
    ms = _dot2(t * t, _head_ones(t.shape[-1])) * (1.0 / HEAD)
    return t * lax.rsqrt(ms + NORM_EPS) * gain


def _store_kv(p, in_ctx, kn_ref, cos_ref, sin_ref, k_ref, v_ref):
    kn = _head_rms(p[:, :KV_W], kn_ref[...])
    kr = _rope(kn, cos_ref[...], sin_ref[...])
    k = jnp.where(in_ctx, kn, kr).astype(BF16)
    v = p[:, KV_W:].astype(BF16)
    one_col = jnp.where(lax.broadcasted_iota(jnp.int32, (p.shape[0], HEAD), 1) == 0, 1.0, 0.0).astype(BF16)
    for h in range(KV_HEADS):
        k_ref[h] = k[:, h * HEAD:(h + 1) * HEAD]
        v_ref[h] = jnp.concatenate([v[:, h * HEAD:(h + 1) * HEAD], one_col], axis=1)


def _attn_kernel(*refs, nseg, rope):
    q_ref, qn_ref = refs[0], refs[1]
    pos = 2
    if rope:
        cos_ref, sin_ref = refs[2], refs[3]
        pos = 4
    kv_refs = refs[pos:pos + 2 * nseg]
    o_ref = refs[pos + 2 * nseg]
    q = _head_rms(q_ref[...], qn_ref[...])
    if rope:
        q = _rope(q, cos_ref[...], sin_ref[...])
    q = (q * (HEAD ** -0.5 * math.log2(math.e))).astype(BF16)
    outs = []
    for g in range(GROUPS):
        qh = q[:, g * HEAD:(g + 1) * HEAD]
        s = [lax.dot_general(qh, kv_refs[2 * i][0], (((1,), (1,)), ((), ())), preferred_element_type=F32)
             for i in range(nseg)]
        m = s[0].max(axis=-1, keepdims=True)
        for si in s[1:]:
            m = jnp.maximum(m, si.max(axis=-1, keepdims=True))
        acc = None
        for i in range(nseg):
            pv = _dot(jnp.exp2(s[i] - m).astype(BF16), kv_refs[2 * i + 1][0])
            acc = pv if acc is None else acc + pv
        outs.append(acc[:, :HEAD] / acc[:, HEAD:HEAD + 1])
    o_ref[...] = jnp.concatenate(outs, axis=1)


def _attention(p_q, k, v, q_norm, cos, sin_signed, nb, nc, ctx, seq, latent):
    qlen = seq if latent else ctx
    tq = _tile(1024, qlen)
    nqt = qlen // tq
    qw = GROUPS * HEAD
    qoff = nc // tq if latent else 0
    qmap = lambda bi, g, t: (qoff + bi * nqt + t, g)
    in_specs = [pl.BlockSpec((tq, qw), qmap), pl.BlockSpec((1, qw), lambda bi, g, t: (0, 0))]
    args = [p_q, jnp.tile(q_norm, GROUPS).reshape(1, qw)]
    ctx_spec = lambda a: pl.BlockSpec((1, ctx, a.shape[-1]), lambda bi, g, t: (g, bi, 0))
    if latent:
        in_specs += [pl.BlockSpec((tq, qw), lambda bi, g, t: (t, 0))] * 2
        args += [cos, sin_signed]
        lat_spec = lambda a: pl.BlockSpec((1, seq, a.shape[-1]), lambda bi, g, t: (g, nc // seq + bi, 0))
        in_specs += [lat_spec(k), lat_spec(v), ctx_spec(k), ctx_spec(v)]
        args += [k, v, k, v]
    else:
        in_specs += [ctx_spec(k), ctx_spec(v)]
        args += [k, v]
    return pl.pallas_call(
        functools.partial(_attn_kernel, nseg=2 if latent else 1, rope=latent),
        grid=(nb, KV_HEADS, nqt),
        in_specs=in_specs,
        out_specs=pl.BlockSpec((tq, qw), lambda bi, g, t: (bi * nqt + t, g)),
        out_shape=jax.ShapeDtypeStruct((nb * qlen, ATT_W), F32),
        compiler_params=_params("parallel", "parallel", "parallel"),
        name="attn_lat" if latent else "attn_ctx",
    )(*args)


def _merge_kernel(x_ref, mod_ref, yf_ref, yb_ref, bonus_ref, g_ref, yc_ref, yl_ref, pg_ref, lw_ref, lb_ref, wpa_ref,
                  wpb_ref, wo_ref, o_ref, *, nct):
    D = x_ref.shape[-1]
    y_att = yl_ref[...] if nct == 0 else jnp.where(pl.program_id(0) < nct, yc_ref[...], yl_ref[...])
    ones = _head_ones(RWKV_W)
    y = yf_ref[0] + yb_ref[0]
    mean = _dot2(y, ones) * (1.0 / HEAD)
    yc = y - mean
    var = _dot2(yc * yc, ones) * (1.0 / HEAD)
    yn = yc * lax.rsqrt(var + LNX_EPS) * lw_ref[...] + lb_ref[...]
    y_rwkv = (yn + bonus_ref[...]) * g_ref[...]
    pg = pg_ref[...]
    mix = (_sigmoid(pg[:, :D]) * _dot(y_rwkv.astype(BF16), wpa_ref[...])
           + _sigmoid(pg[:, D:]) * _dot(y_att.astype(BF16), wpb_ref[...]))
    o_ref[...] = x_ref[...] + mod_ref[0][2:3] * _dot(mix.astype(BF16), wo_ref[...])


def _merge(xf, mod, y_fwd, y_bwd, bonus, g, y_ctx, y_lat, p_g, lnx_w, lnx_b, w_pa, w_pb, w_o, nc, seq, nb, row0):
    N, D = xf.shape
    W = RWKV_W
    tm = _tile(512, nc // y_fwd.shape[0], seq)
    off = row0 // tm
    n_out = N - row0
    mod_idx = _mod_index(tm, nc, seq, nb)
    rows = lambda w: pl.BlockSpec((tm, w), lambda i: (i + off, 0))
    slot_index = _slot_rows(tm, nc, N - nc, y_fwd.shape[0])
    scan_rows = pl.BlockSpec((1, tm, W), lambda i: slot_index(i + off))
    full = lambda a: pl.BlockSpec(a.shape, lambda i: (0,) * a.ndim)
    consts = (lnx_w.reshape(1, W), lnx_b.reshape(1, W), w_pa, w_pb, w_o)
    nct = (nc - row0) // tm
    att_ctx = pl.BlockSpec((tm, W), lambda i: (jnp.minimum(i, max(nct - 1, 0)), 0))
    att_lat = pl.BlockSpec((tm, W), lambda i: (jnp.maximum(i - nct, 0), 0))
    return pl.pallas_call(
        functools.partial(_merge_kernel, nct=nct),
        grid=(n_out // tm,),
        in_specs=[rows(D), pl.BlockSpec((1, 8, D), lambda i: mod_idx(i + off)),
                  scan_rows, scan_rows, rows(W), rows(W), att_ctx, att_lat, rows(2 * D)]
                 + [full(a) for a in consts],
        out_specs=pl.BlockSpec((tm, D), lambda i: (i, 0)),
        out_shape=jax.ShapeDtypeStruct((n_out, D), F32),
        compiler_params=_params("parallel"),
        name="merge",
    )(xf, mod, y_fwd, y_bwd, bonus, g, y_ctx, y_lat, p_g, *consts)


def _ffn_kernel(x_ref, mod_ref, g_ref, wg_ref, wu_ref, wd_ref, o_ref, h_ref, acc_ref):
    f = pl.program_id(1)

    @pl.when(f == 0)
    def _():
        m = mod_ref[0]
        h_ref[...] = _normmod(x_ref[...], g_ref[...], m[4:5], m[3:4]).astype(BF16)
        acc_ref[...] = jnp.zeros_like(acc_ref)

    h = h_ref[...]
    a = _dot(h, wg_ref[...])
    z = a * _sigmoid(a) * _dot(h, wu_ref[...])
    acc_ref[...] += _dot(z.astype(BF16), wd_ref[...])

    @pl.when(f == pl.num_programs(1) - 1)
    def _():
        o_ref[...] = x_ref[...] + mod_ref[0][5:6] * acc_ref[...]


def _ffn(xf, mod, gain, wg, wu, wd, nc, seq, nb):
    N, D = xf.shape
    F = wg.shape[-1]
    tm = _tile(1024, nc, seq)
    tf = _ffn_tile(F)
    return pl.pallas_call(
        _ffn_kernel,
        grid=(N // tm, F // tf),
        in_specs=[pl.BlockSpec((tm, D), lambda i, f: (i, 0)),
                  pl.BlockSpec((1, 8, D), lambda i, f: _mod_index(tm, nc, seq, nb)(i)),
                  pl.BlockSpec((1, D), lambda i, f: (0, 0)),
                  pl.BlockSpec((D, tf), lambda i, f: (0, f)),
                  pl.BlockSpec((D, tf), lambda i, f: (0, f)),
                  pl.BlockSpec((tf, D), lambda i, f: (f, 0))],
        out_specs=pl.BlockSpec((tm, D), lambda i, f: (i, 0)),
        out_shape=jax.ShapeDtypeStruct((N, D), F32),
        scratch_shapes=[pltpu.VMEM((tm, D), BF16), pltpu.VMEM((tm, D), F32)],
        compiler_params=_params("parallel", "arbitrary"),
        name="ffn",
    )(xf, mod, gain.reshape(1, D), wg, wu, wd)


R_E1, R_E2, R_RANK1, R_RANK2, R_W1, R_W2 = range(6)


def _router_kernel(x_ref, mod_ref, g_ref, wr_ref, route_ref, cnt_ref, carry_ref):
    @pl.when(pl.program_id(0) == 0)
    def _():
        carry_ref[...] = jnp.zeros_like(carry_ref)

    m = mod_ref[0]
    h = _normmod(x_ref[...], g_ref[...], m[4:5], m[3:4])
    tm = h.shape[0]
    lane = lax.broadcasted_iota(jnp.int32, (tm, LANES), 1).astype(F32)
    logits = jnp.where(lane < N_EXPERTS, _dot3(h, wr_ref[...]), NEG_BIG)
    m1 = logits.max(axis=-1, keepdims=True)
    i1 = jnp.where(logits == m1, lane, float(LANES)).min(axis=-1, keepdims=True)
    rest = jnp.where(lane == i1, NEG_BIG, logits)
    m2 = rest.max(axis=-1, keepdims=True)
    i2 = jnp.where(rest == m2, lane, float(LANES)).min(axis=-1, keepdims=True)
    e2 = jnp.exp(m2 - m1)
    w1 = 1.0 / (1.0 + e2)
    w2 = e2 / (1.0 + e2)
    onehot = jnp.where(lane == i1, 1.0, jnp.where(lane == i2, 1.0, 0.0))
    earlier = (lax.broadcasted_iota(jnp.int32, (tm, tm), 0) > lax.broadcasted_iota(jnp.int32, (tm, tm), 1))
    before = _dot(jnp.where(earlier, 1.0, 0.0).astype(BF16), onehot.astype(BF16)) + carry_ref[...]
    rank1 = jnp.where(lane == i1, before, 0.0).sum(axis=-1, keepdims=True)
    rank2 = jnp.where(lane == i2, before, 0.0).sum(axis=-1, keepdims=True)
    carry_ref[...] += onehot.sum(axis=0, keepdims=True)
    cnt_ref[...] = jnp.broadcast_to(carry_ref[...], cnt_ref.shape)
    rec = jnp.zeros_like(lane)
    for idx, val in ((R_E1, i1), (R_E2, i2), (R_RANK1, rank1), (R_RANK2, rank2), (R_W1, w1), (R_W2, w2)):
        rec = jnp.where(lane == float(idx), val, rec)
    route_ref[...] = rec


def _router(xl, mod, gain, router, seq):
    N, D = xl.shape
    tm = _tile(512, seq)
    wr = jnp.zeros((D, LANES), F32).at[:, :N_EXPERTS].set(router)
    return pl.pallas_call(
        _router_kernel,
        grid=(N // tm,),
        in_specs=[pl.BlockSpec((tm, D), lambda i: (i, 0)),
                  pl.BlockSpec((1, 8, D), lambda i: ((i * tm) // seq, 0, 0)),
                  pl.BlockSpec((1, D), lambda i: (0, 0)),
                  pl.BlockSpec((D, LANES), lambda i: (0, 0))],
        out_specs=[pl.BlockSpec((tm, LANES), lambda i: (i, 0)), pl.BlockSpec((8, LANES), lambda i: (0, 0))],
        out_shape=[jax.ShapeDtypeStruct((N, LANES), F32), jax.ShapeDtypeStruct((8, LANES), F32)],
        scratch_shapes=[pltpu.VMEM((1, LANES), F32)],
        compiler_params=_params("arbitrary"),
        name="router",
    )(xl, mod, gain.reshape(1, D), wr)


def _start_rows(n, copies):
    def start(r, c):
        for copy in copies:
            copy(r).start()
        return c

    lax.fori_loop(0, n, start, 0, unroll=8)


def _wait_rows(n, copies):
    def wait(r, c):
        for copy in copies:
            copy(r).wait()
        return c

    lax.fori_loop(0, n, wait, 0, unroll=8)


def _dispatch_kernel(dest_ref, x_ref, mod_ref, g_ref, xs_in_ref, xs_ref, h_ref, sem):
    del xs_in_ref
    tm = x_ref.shape[0]
    i = pl.program_id(0)
    nt = pl.num_programs(0)
    n = tm * nt

    def scatters(tile):
        half = tile % 2
        return [lambda r, off=off: pltpu.make_async_copy(
            h_ref.at[half, pl.ds(r, 1)], xs_ref.at[pl.ds(dest_ref[off + tile * tm + r], 1)], sem.at[half])
            for off in (0, n)]

    m = mod_ref[0]
    h_ref[i % 2] = _normmod(x_ref[...], g_ref[...], m[4:5], m[3:4])
    _start_rows(tm, scatters(i))

    @pl.when(i > 0)
    def _():
        _wait_rows(tm, scatters(i - 1))

    @pl.when(i == nt - 1)
    def _():
        _wait_rows(tm, scatters(i))


def _dispatch(dest, xl, mod, gain, xs, seq):
    N, D = xl.shape
    tm = _tile(512, seq)
    return pl.pallas_call(
        _dispatch_kernel,
        grid_spec=pltpu.PrefetchScalarGridSpec(
            num_scalar_prefetch=1,
            grid=(N // tm,),
            in_specs=[pl.BlockSpec((tm, D), lambda i, dest: (i, 0)),
                      pl.BlockSpec((1, 8, D), lambda i, dest: ((i * tm) // seq, 0, 0)),
                      pl.BlockSpec((1, D), lambda i, dest: (0, 0)),
                      pl.BlockSpec(memory_space=pl.ANY)],
            out_specs=pl.BlockSpec(memory_space=pl.ANY),
            scratch_shapes=[pltpu.VMEM((2, tm, D), F32), pltpu.SemaphoreType.DMA((2,))]),
        out_shape=jax.ShapeDtypeStruct(xs.shape, xs.dtype),
        input_output_aliases={4: 0},
        compiler_params=_params("arbitrary"),
        name="moe_dispatch",
    )(dest, xl, mod, gain.reshape(1, D), xs)


def _expert_kernel(tbl_ref, xs_ref, wg_ref, wu_ref, wd_ref, ys_ref, hb_ref, acc_ref, *, nblk):
    i = pl.program_id(0)
    f = pl.program_id(1)
    used = i < tbl_ref[nblk]
    last = f == pl.num_programs(1) - 1

    @pl.when(used & (f == 0))
    def _():
        hb_ref[...] = xs_ref[...].astype(BF16)
        acc_ref[...] = jnp.zeros_like(acc_ref)

    @pl.when(used)
    def _():
        h = hb_ref[...]
        a = _dot(h, wg_ref[0])
        z = a * _sigmoid(a) * _dot(h, wu_ref[0])
        acc_ref[...] += _dot(z.astype(BF16), wd_ref[0])

    @pl.when(used & last)
    def _():
        ys_ref[...] = acc_ref[...]

    @pl.when(jnp.logical_not(used) & last)
    def _():
        ys_ref[...] = jnp.zeros_like(ys_ref)


def _experts(tbl, xs, wg, wu, wd, blk):
    R, D = xs.shape
    F = wg.shape[-1]
    nblk = R // blk
    tf = _ffn_tile(F)
    return pl.pallas_call(
        functools.partial(_expert_kernel, nblk=nblk),
        grid_spec=pltpu.PrefetchScalarGridSpec(
            num_scalar_prefetch=1,
            grid=(nblk, F // tf),
            in_specs=[pl.BlockSpec((blk, D), lambda i, f, tbl: (i, 0)),
                      pl.BlockSpec((1, D, tf), lambda i, f, tbl: (tbl[i], 0, f)),
                      pl.BlockSpec((1, D, tf), lambda i, f, tbl: (tbl[i], 0, f)),
                      pl.BlockSpec((1, tf, D), lambda i, f, tbl: (tbl[i], f, 0))],
            out_specs=pl.BlockSpec((blk, D), lambda i, f, tbl: (i, 0)),
            scratch_shapes=[pltpu.VMEM((blk, D), BF16), pltpu.VMEM((blk, D), F32)]),
        out_shape=jax.ShapeDtypeStruct((R, D), F32),
        compiler_params=_params("parallel", "arbitrary"),
        name="moe_experts",
    )(tbl, xs, wg, wu, wd)


def _combine_kernel(dest_ref, x_ref, route_ref, mod_ref, fn_ref, ys_ref, o_ref, buf_ref, sem):
    tm = x_ref.shape[0]
    i = pl.program_id(0)
    nt = pl.num_programs(0)
    n = tm * nt

    def gathers(tile):
        half = tile % 2
        return [lambda r, a=a: pltpu.make_async_copy(
            ys_ref.at[pl.ds(dest_ref[a * n + tile * tm + r], 1)], buf_ref.at[half, pl.ds(a * tm + r, 1)],
            sem.at[half]) for a in (0, 1)]

    @pl.when(i == 0)
    def _():
        _start_rows(tm, gathers(i))

    @pl.when(i + 1 < nt)
    def _():
        _start_rows(tm, gathers(i + 1))

    _wait_rows(tm, gathers(i))
    buf = buf_ref[i % 2]
    mix = route_ref[:, R_W1:R_W1 + 1] * buf[:tm] + route_ref[:, R_W2:R_W2 + 1] * buf[tm:]
    y = x_ref[...] + mod_ref[0][5:6] * mix
    ms = jnp.mean(y * y, axis=-1, keepdims=True)
    o_ref[...] = y * lax.rsqrt(ms + NORM_EPS) * fn_ref[...]


def _combine(dest, acc, route, mod, final_norm, ys, seq):
    N, D = acc.shape
    tm = _tile(512, seq)
    return pl.pallas_call(
        _combine_kernel,
        grid_spec=pltpu.PrefetchScalarGridSpec(
            num_scalar_prefetch=1,
            grid=(N // tm,),
            in_specs=[pl.BlockSpec((tm, D), lambda i, dest: (i, 0)),
                      pl.BlockSpec((tm, LANES), lambda i, dest: (i, 0)),
                      pl.BlockSpec((1, 8, D), lambda i, dest: ((i * tm) // seq, 0, 0)),
                      pl.BlockSpec((1, D), lambda i, dest: (0, 0)),
                      pl.BlockSpec(memory_space=pl.ANY)],
            out_specs=pl.BlockSpec((tm, D), lambda i, dest: (i, 0)),
            scratch_shapes=[pltpu.VMEM((2, 2 * tm, D), F32), pltpu.SemaphoreType.DMA((2,))]),
        out_shape=jax.ShapeDtypeStruct((N, D), F32),
        compiler_params=_params("arbitrary"),
        name="moe_combine",
    )(dest, acc, route, mod, final_norm.reshape(1, D), ys)


def _moe_final(xl, mod, gain, router, wg, wu, wd, final_norm, seq):
    N, D = xl.shape
    blk = MOE_BLOCK_ROWS
    route, counts = _router(xl, mod, gain, router, seq)
    counts = counts[0, :N_EXPERTS].astype(jnp.int32)
    padded = (counts + blk - 1) // blk * blk
    ends = jnp.cumsum(padded)
    starts = ends - padded
    nblk = -(-(2 * N) // blk) + N_EXPERTS
    blk_expert = jnp.minimum(jnp.searchsorted(ends, jnp.arange(nblk, dtype=jnp.int32) * blk, side='right'),
                             N_EXPERTS - 1).astype(jnp.int32)
    tbl = jnp.concatenate([blk_expert, (ends[-1:] // blk).astype(jnp.int32)])
    dest = jnp.concatenate([starts[route[:, e].astype(jnp.int32)] + route[:, r].astype(jnp.int32)
                            for e, r in ((R_E1, R_RANK1), (R_E2, R_RANK2))])
    xs = _dispatch(dest, xl, mod, gain, jnp.zeros((nblk * blk, D), F32), seq)
    ys = _experts(tbl, xs, wg, wu, wd, blk)
    return _combine(dest, xl, route, mod, final_norm, ys, seq)


def _rope_tables(seq):
    rows = seq // GRID_W
    row, col = jnp.meshgrid(jnp.arange(rows), jnp.arange(GRID_W), indexing='ij')
    axis = HEAD // 2
    inv = ROPE_THETA ** (-jnp.arange(0, axis, 2, dtype=F32) / axis)
    ang = jnp.concatenate([row.reshape(-1, 1).astype(F32) * inv, col.reshape(-1, 1).astype(F32) * inv], axis=-1)
    cos = jnp.repeat(jnp.cos(ang), 2, axis=-1)
    sin = jnp.repeat(jnp.sin(ang), 2, axis=-1) * jnp.tile(jnp.array([-1.0, 1.0], F32), HEAD // 2)
    return jnp.tile(cos, (1, GROUPS)), jnp.tile(sin, (1, GROUPS))


def kernel(x, c, ctx, c_ctx, ada_w, ada_b, norm1, norm2, w_in, shift_mu, rwkv_w0, rwkv_w2, rwkv_a0, rwkv_a2, rwkv_g2, rwkv_kk, rwkv_ka, rwkv_rk, lnx_w, lnx_b, q_norm, k_norm, w_pa, w_pb, w_o, ffn_wg, ffn_wu, ffn_wd, router, moe_wg, moe_wu, moe_wd, final_norm):
    B, T, D = x.shape
    CX = ctx.shape[1]
    depth = ada_w.shape[0]
    assert depth == 2 and ffn_wg.shape[0] == 1 and router.shape[0] == 1
    nc = B * CX
    bf = lambda w: w.astype(BF16)

    nrow = -(-(B + 1) // 8) * 8
    act = jnp.zeros((nrow, D), F32).at[:B].set(c).at[B].set(c_ctx)
    mod = _ada(act, ada_w, ada_b).reshape(depth, nrow, 6, D)
    mod = jnp.concatenate([mod, jnp.zeros((depth, nrow, 2, D), F32)], axis=2)

    cos, sin_signed = _rope_tables(T)
    xf = jnp.concatenate([ctx.reshape(nc, D), x.reshape(B * T, D)], axis=0)
    out = None
    for l in range(depth):
        last = l == depth - 1
        p_r, p_q, k_att, v_att, p_g = _in_proj(xf, mod[l], norm1[l], bf(w_in[l]), k_norm[l], cos, sin_signed, nc, T, B)
        r, v, nkk, g, bonus, lw, keys, b = _rwkv_prep(
            p_r, shift_mu[l], rwkv_w0[l], rwkv_w2[l], rwkv_a0[l], rwkv_a2[l], rwkv_g2[l], rwkv_kk[l], rwkv_ka[l],
            rwkv_rk[l].reshape(-1), nc, CX, T)
        y_fwd, y_bwd = _wkv_scan_pairs(r, v, nkk, lw, keys, b, B, CX, T)
        y_lat = _attention(p_q, k_att, v_att, q_norm[l], cos, sin_signed, B, nc, CX, T, latent=True)
        y_ctx = y_lat if last else _attention(p_q, k_att, v_att, q_norm[l], cos, sin_signed, B, nc, CX, T,
                                              latent=False)
        xm = _merge(xf, mod[l], y_fwd, y_bwd, bonus, g, y_ctx, y_lat, p_g, lnx_w[l], lnx_b[l], bf(w_pa[l]),
                    bf(w_pb[l]), bf(w_o[l]), nc, T, B, nc if last else 0)
        if not last:
            xf = _ffn(xm, mod[l], norm2[l], bf(ffn_wg[l // 2]), bf(ffn_wu[l // 2]), bf(ffn_wd[l // 2]), nc, T, B)
        else:
            out = _moe_final(xm, mod[l], norm2[l], router[l // 2], bf(moe_wg[l // 2]), bf(moe_wu[l // 2]),
                             bf(moe_wd[l // 2]), final_norm, T)
    return out.reshape(B, T, D)
```

```python
import functools
import math

import jax
import jax.numpy as jnp
from jax import lax
from jax.experimental import pallas as pl
from jax.experimental.pallas import tpu as pltpu

F32 = jnp.float32
BF16 = jnp.bfloat16

HEAD = 64
RWKV_HEADS = 8
RWKV_W = RWKV_HEADS * HEAD
LORA_W = 128
ATT_HEADS = 8
KV_HEADS = 2
GROUPS = ATT_HEADS // KV_HEADS
ATT_W = ATT_HEADS * HEAD
KV_W = KV_HEADS * HEAD
R_COLS = 3 * RWKV_W + 64 + 64 + 128
GRID_W = 64
ROPE_THETA = 10000.0
N_EXPERTS = 8
NORM_EPS = 1e-6
LNX_EPS = 64e-5
LANES = 128
NEG_BIG = -1e30
VMEM_LIMIT_BYTES = 56 * 1024 * 1024
SCAN_CHUNK = 64
FFN_TILE = 512
SCAN_BATCHES_PER_STEP = 4
MOE_BLOCK_ROWS = 1024


def _params(*sem):
    return pltpu.CompilerParams(dimension_semantics=sem, vmem_limit_bytes=VMEM_LIMIT_BYTES)


def _tile(pref, *dims):
    t = pref
    while any(d % t for d in dims):
        t //= 2
    return t


def _ffn_tile(f):
    return _tile(FFN_TILE, f)


def _dot(a, b):
    return jnp.dot(a, b, preferred_element_type=F32)


def _split(x):
    hi = x.astype(BF16)
    lo = (x - hi.astype(F32)).astype(BF16)
    return hi, lo


def _dot3(a, b):
    ah, al = _split(a)
    bh, bl = _split(b)
    return _dot(ah, bh) + (_dot(ah, bl) + _dot(al, bh))


def _dot2(a, b_bf16):
    ah, al = _split(a)
    return _dot(ah, b_bf16) + _dot(al, b_bf16)


def _sigmoid(x):
    return 1.0 / (1.0 + jnp.exp(-x))


def _normmod(x, gain, scale, shift):
    ms = jnp.mean(x * x, axis=-1, keepdims=True)
    return x * lax.rsqrt(ms + NORM_EPS) * gain * (1.0 + scale) + shift


def _head_ones(width):
    r = lax.broadcasted_iota(jnp.int32, (width, width), 0) // HEAD
    c = lax.broadcasted_iota(jnp.int32, (width, width), 1) // HEAD
    return jnp.where(r == c, 1.0, 0.0).astype(BF16)


def _ada_kernel(a_ref, w_ref, b_ref, o_ref):
    a = a_ref[...]
    o_ref[0] = _dot3(a * _sigmoid(a), w_ref[0]) + b_ref[0]


def _ada(act, ada_w, ada_b):
    L, D, N6 = ada_w.shape
    R = act.shape[0]
    tn = _tile(1536, N6)
    return pl.pallas_call(
        _ada_kernel,
        grid=(L, N6 // tn),
        in_specs=[pl.BlockSpec((R, D), lambda l, j: (0, 0)),
                  pl.BlockSpec((1, D, tn), lambda l, j: (l, 0, j)),
                  pl.BlockSpec((1, 1, tn), lambda l, j: (l, 0, j))],
        out_specs=pl.BlockSpec((1, R, tn), lambda l, j: (l, 0, j)),
        out_shape=jax.ShapeDtypeStruct((L, R, N6), F32),
        compiler_params=_params("parallel", "parallel"),
        name="ada",
    )(act, ada_w, ada_b.reshape(L, 1, N6))


def _in_kernel(x_ref, mod_ref, g_ref, w_ref, kn_ref, cos_ref, sin_ref, pr_ref, pq_ref, k_ref, v_ref, pg_ref, *, nc):
    m = mod_ref[0]
    h = _normmod(x_ref[...], g_ref[...], m[1:2], m[0:1]).astype(BF16)
    col = 0
    for ref, n in ((pr_ref, R_COLS), (pq_ref, ATT_W), (None, 2 * KV_W), (pg_ref, pg_ref.shape[-1])):
        if ref is None:
            in_ctx = pl.program_id(0) * x_ref.shape[0] < nc
            _store_kv(_dot(h, w_ref[:, col:col + n]), in_ctx, kn_ref, cos_ref, sin_ref, k_ref, v_ref)
        else:
            for c0 in range(0, n, 256):
                ref[:, c0:c0 + 256] = _dot(h, w_ref[:, col + c0:col + c0 + 256])
        col += n


def _mod_index(tm, nc, seq, nb):
    def index(i):
        start = i * tm
        return (jnp.where(start < nc, nb, (start - nc) // seq), 0, 0)
    return index


def _in_proj(xf, mod, gain, w_bf16, k_norm, cos, sin_signed, nc, seq, nb):
    N, D = xf.shape
    tm = _tile(512, nc, seq)
    npos = seq // tm
    pos = lambda i: ((jnp.maximum(i * tm - nc, 0) // tm) % npos, 0)
    rows = lambda w: pl.BlockSpec((tm, w), lambda i: (i, 0))
    heads = lambda w: pl.BlockSpec((KV_HEADS, tm, w), lambda i: (0, i, 0))
    flat = lambda w: jax.ShapeDtypeStruct((N, w), F32)
    per_head = lambda w: jax.ShapeDtypeStruct((KV_HEADS, N, w), BF16)
    return pl.pallas_call(
        functools.partial(_in_kernel, nc=nc),
        grid=(N // tm,),
        in_specs=[rows(D),
                  pl.BlockSpec((1, 8, D), _mod_index(tm, nc, seq, nb)),
                  pl.BlockSpec((1, D), lambda i: (0, 0)),
                  pl.BlockSpec(w_bf16.shape, lambda i: (0, 0)),
                  pl.BlockSpec((1, KV_W), lambda i: (0, 0)),
                  pl.BlockSpec((tm, KV_W), pos),
                  pl.BlockSpec((tm, KV_W), pos)],
        out_specs=[rows(R_COLS), rows(ATT_W), heads(HEAD), heads(2 * HEAD), rows(2 * D)],
        out_shape=[flat(R_COLS), flat(ATT_W), per_head(HEAD), per_head(2 * HEAD), flat(2 * D)],
        compiler_params=_params("parallel"),
        name="in_proj",
    )(xf, mod, gain.reshape(1, D), w_bf16, jnp.tile(k_norm, KV_HEADS).reshape(1, KV_W), cos[:, :KV_W],
      sin_signed[:, :KV_W])


def _prep_kernel(p_ref, prev_ref, next_ref, mu_ref, w0_ref, w2_ref, a0_ref, a2_ref, g2_ref, kk_ref, ka_ref,
                 rk_ref, r_ref, v_ref, nkk_ref, g_ref, bonus_ref, lw_ref, keys_ref, b_ref, *, tt, nc, ctx, seq):
    i = pl.program_id(0)
    start = i * tt
    in_ctx = start < nc
    pos = jnp.where(in_ctx, start % ctx, (start - nc) % seq)
    seg = jnp.where(in_ctx, ctx, seq)
    first = pos == 0
    last = pos + tt == seg
    p = p_ref[...]
    rows = lax.broadcasted_iota(jnp.int32, (tt, 1), 0)
    prev_row = jnp.where(first, 0.0, prev_ref[7:8, :])
    next_row = jnp.where(last, 0.0, next_ref[0:1, :])
    prev = jnp.where(rows == 0, prev_row, pltpu.roll(p, 1, 0))
    nxt = jnp.where(rows == tt - 1, next_row, pltpu.roll(p, tt - 1, 0))
    ps = p + mu_ref[0:1, :] * (prev - p) + mu_ref[1:2, :] * (nxt - p)
    W = RWKV_W
    r = ps[:, 0:W]
    k = ps[:, W:2 * W]
    v = ps[:, 2 * W:3 * W]
    wa = ps[:, 3 * W:3 * W + LORA_W]
    gd = ps[:, 3 * W + LORA_W:]
    ones = _head_ones(W)
    kkf = k * kk_ref[...]
    nrm = jnp.sqrt(_dot2(kkf * kkf, ones))
    kk = kkf / jnp.maximum(nrm, 1e-12)
    tw = jnp.tanh(wa)
    ksum = jnp.zeros_like(k)
    for d in range(2):
        z = w0_ref[d:d + 1, :] + _dot3(tw, w2_ref[d])
        lw_ref[d] = -math.exp(-0.5) * _sigmoid(z)
        a = _sigmoid(a0_ref[d:d + 1, :] + _dot3(wa, a2_ref[d]))
        keys = k * (1.0 + (a - 1.0) * ka_ref[...])
        keys_ref[d] = keys
        b_ref[d] = kk * a
        ksum = ksum + keys
    r_ref[...] = r
    v_ref[...] = v
    nkk_ref[...] = -kk
    g_ref[...] = _dot3(_sigmoid(gd), g2_ref[...])
    bonus_ref[...] = _dot2(r * (0.5 * ksum) * rk_ref[...], ones) * v


def _rwkv_prep(p_r, mu, w0, w2, a0, a2, g2, k_k, k_a, r_k, nc, ctx, seq):
    N = p_r.shape[0]
    W = RWKV_W
    tt = _tile(256, ctx, seq)
    nblk8 = N // 8
    zeros = jnp.zeros((2, 64, W), F32)
    w2p = jnp.concatenate([w2, zeros], axis=1)
    a2p = jnp.concatenate([zeros, a2], axis=1)
    row = lambda t: t.reshape(1, W)
    full = lambda a: pl.BlockSpec(a.shape, lambda i: (0,) * a.ndim)
    consts = (mu, w0, w2p, a0, a2p, g2, row(k_k), row(k_a), row(r_k))
    one = jax.ShapeDtypeStruct((N, W), F32)
    two = jax.ShapeDtypeStruct((2, N, W), F32)
    s1 = pl.BlockSpec((tt, W), lambda i: (i, 0))
    s2 = pl.BlockSpec((2, tt, W), lambda i: (0, i, 0))
    return pl.pallas_call(
        functools.partial(_prep_kernel, tt=tt, nc=nc, ctx=ctx, seq=seq),
        grid=(N // tt,),
        in_specs=[pl.BlockSpec((tt, R_COLS), lambda i: (i, 0)),
                  pl.BlockSpec((8, R_COLS), lambda i: (jnp.maximum(i * (tt // 8) - 1, 0), 0)),
                  pl.BlockSpec((8, R_COLS), lambda i: (jnp.minimum((i + 1) * (tt // 8), nblk8 - 1), 0))]
                 + [full(a) for a in consts],
        out_specs=[s1, s1, s1, s1, s1, s2, s2, s2],
        out_shape=[one, one, one, one, one, two, two, two],
        compiler_params=_params("parallel"),
        name="rwkv_prep",
    )(p_r, p_r, p_r, *consts)


def _pair_blocks(x):
    left = lax.broadcasted_iota(jnp.int32, x.shape, 1) < HEAD
    zero = jnp.zeros_like(x)
    return jnp.concatenate([jnp.where(left, x, zero), jnp.where(left, zero, x)], axis=0)


def _pair_diag(x):
    left = lax.broadcasted_iota(jnp.int32, (HEAD, LANES), 1) < HEAD
    return jnp.where(left, x[:HEAD], x[HEAD:])


def _scan_pair_kernel(*refs, C, direction, nslot):
    assert C == HEAD
    y_ref, st_ref = refs[6 * nslot], refs[6 * nslot + 1]

    @pl.when(pl.program_id(1) == 0)
    def _():
        st_ref[...] = jnp.zeros_like(st_ref)

    npair = RWKV_HEADS // 2
    row = lax.broadcasted_iota(jnp.int32, (C, C), 0)
    col = lax.broadcasted_iota(jnp.int32, (C, C), 1)
    before = row - col if direction == 0 else col - row
    linc = jnp.where(before >= 0, 1.0, 0.0).astype(BF16)
    row2 = lax.broadcasted_iota(jnp.int32, (C, LANES), 0)
    col2 = lax.broadcasted_iota(jnp.int32, (C, LANES), 1) % C
    before2 = row2 - col2 if direction == 0 else col2 - row2
    strict = before2 > 0
    incl = before2 >= 0
    eye2 = row2 == col2
    eye2b = jnp.where(eye2, 1.0, 0.0).astype(BF16)
    nt = (((1,), (1,)), ((), ()))
    tn = (((0,), (0,)), ((), ()))
    nlev = int(math.log2(C))

    at, rt, bt, kt, bh, kh, vb, g_tot = ([] for _ in range(8))
    for s in range(nslot):
        r_ref, v_ref, a_ref, lw_ref, k_ref, b_ref = refs[6 * s:6 * s + 6]
        lw = lw_ref[0]
        lh, ll = _split(lw)
        lam = _dot(linc, lh) + _dot(linc, ll)
        tot = jnp.sum(lw, axis=0, keepdims=True)
        g_rem = jnp.exp(tot - lam)
        g_inv = jnp.exp(-lam)
        gt = jnp.exp(tot)
        k = k_ref[0]
        b = b_ref[0]
        a_s = a_ref[...] * jnp.exp(lam - lw)
        r_s = r_ref[...] * jnp.exp(lam)
        b_t = (b * g_inv).astype(BF16)
        k_t = (k * g_inv).astype(BF16)
        b_h = (b * g_rem).astype(BF16)
        k_h = (k * g_rem).astype(BF16)
        v_b = v_ref[...].astype(BF16)
        for p in range(npair):
            sl = slice(p * LANES, (p + 1) * LANES)
            for dst, src in ((at, a_s), (rt, r_s), (bt, b_t), (kt, k_t), (bh, b_h), (kh, k_h), (vb, v_b), (g_tot, gt)):
                dst.append(src[:, sl])
    nch = nslot * npair

    ar = [jnp.concatenate([at[c].astype(BF16), rt[c].astype(BF16)], axis=0) for c in range(nch)]
    L = LANES
    xbk = [lax.dot_general(ar[c], jnp.concatenate([_pair_blocks(bt[c]), _pair_blocks(kt[c])], axis=0), nt,
                           preferred_element_type=F32) for c in range(nch)]
    s_pow = [jnp.where(strict, x[:C, :L], 0.0) for x in xbk]
    m_rb = [jnp.where(incl, x[C:, :L], 0.0).astype(BF16) for x in xbk]
    q = [jnp.where(strict, x[:C, L:], 0.0) for x in xbk]
    m_rk = [jnp.where(incl, x[C:, L:], 0.0) for x in xbk]
    ah = list(at)
    for lev in range(nlev):
        for c in range(nch):
            sb = s_pow[c].astype(BF16)
            rhs = [_pair_blocks(ah[c].astype(BF16)), _pair_blocks(q[c].astype(BF16))]
            if lev < nlev - 1:
                rhs.append(_pair_blocks(sb))
            prod = _dot(sb, jnp.concatenate(rhs, axis=1))
            ah[c] = ah[c] + prod[:, :L]
            q[c] = q[c] + prod[:, L:2 * L]
            if lev < nlev - 1:
                s_pow[c] = prod[:, 2 * L:]
    ahb = [x.astype(BF16) for x in ah]
    qb = [x.astype(BF16) for x in q]
    raq = [_dot(m_rb[c], jnp.concatenate([_pair_blocks(ahb[c]), _pair_blocks(qb[c])], axis=1))
           for c in range(nch)]
    baq = [lax.dot_general(bh[c], jnp.concatenate([ahb[c], qb[c]], axis=1), tn, preferred_element_type=F32)
           for c in range(nch)]
    k_tr = [_pair_diag(lax.dot_general(kh[c], eye2b, tn, preferred_element_type=F32)) for c in range(nch)]
    for c in range(nch):
        s, p = divmod(c, npair)
        r_hat = rt[c] + raq[c][:, :L]
        g_mat = jnp.where(eye2, g_tot[c], 0.0) + _pair_diag(baq[c][:, :L])
        v2 = _pair_blocks(vb[c])
        y_in = _dot((m_rk[c] + raq[c][:, L:]).astype(BF16), v2)
        h_mat = _dot((_pair_diag(baq[c][:, L:]) + k_tr[c]).astype(BF16), v2)
        sh, sl_ = _split(st_ref[c])
        sh, sl_ = _pair_blocks(sh), _pair_blocks(sl_)
        gh, gl = _split(g_mat)
        gs = _dot(gh, jnp.concatenate([sh, sl_], axis=1))
        st_ref[c] = gs[:, :L] + (gs[:, L:] + _dot(gl, sh)) + h_mat
        y_ref[s, :, p * LANES:(p + 1) * LANES] = _dot(r_hat.astype(BF16), sh) + y_in


def _wkv_scan_pairs(r, v, nkk, lw, keys, b, nb, ctx, seq):
    N, W = r.shape
    C = SCAN_CHUNK
    ncc, nlc = ctx // C, seq // C
    nslot = math.gcd(SCAN_BATCHES_PER_STEP, nb)
    hb = nb // nslot

    def blk(d, j, bi, nbat):
        jc = j if d == 0 else ncc - 1 - j
        jl = j - ncc if d == 0 else nlc - 1 - (j - ncc)
        return jnp.where(j < ncc, bi * ncc + jc, nbat * ncc + bi * nlc + jl)

    def one(d):
        s1 = lambda s: pl.BlockSpec((C, W), lambda i, j: (blk(d, j, s * hb + i, nb), 0))
        s2 = lambda s: pl.BlockSpec((1, C, W), lambda i, j: (d, blk(d, j, s * hb + i, nb), 0))
        in_specs = [spec for s in range(nslot) for spec in (s1(s), s1(s), s1(s), s2(s), s2(s), s2(s))]
        return pl.pallas_call(
            functools.partial(_scan_pair_kernel, C=C, direction=d, nslot=nslot),
            grid=(hb, ncc + nlc),
            in_specs=in_specs,
            out_specs=pl.BlockSpec((nslot, C, W), lambda i, j: (0, blk(d, j, i, hb), 0)),
            out_shape=jax.ShapeDtypeStruct((nslot, N // nslot, W), F32),
            scratch_shapes=[pltpu.VMEM((nslot * RWKV_HEADS // 2, HEAD, LANES), F32)],
            compiler_params=_params("parallel", "arbitrary"),
            name="wkv_scan",
        )(*([r, v, nkk, lw, keys, b] * nslot))

    return one(0), one(1)


def _slot_rows(tm, nc, nl, nslot):
    def index(i):
        start = i * tm
        in_ctx = start < nc
        rel = jnp.where(in_ctx, start, start - nc)
        per = jnp.where(in_ctx, nc // nslot, nl // nslot)
        local = jnp.where(in_ctx, 0, nc // nslot) + rel % per
        return (rel // per, local // tm, 0)
    return index


def _rope(t, cos, sin_signed):
    w = t.shape[-1]
    lane = lax.broadcasted_iota(jnp.int32, t.shape, 1)
    swapped = jnp.where(lane % 2 == 0, pltpu.roll(t, w - 1, 1), pltpu.roll(t, 1, 1))
    return t * cos + swapped * sin_signed


def _head_rms(t, gain):
    ms = _dot2(t * t, _head_ones(t.shape[-1])) * (1.0 / HEAD)
    return t * lax.rsqrt(ms + NORM_EPS) * gain


def _store_kv(p, in_ctx, kn_ref, cos_ref, sin_ref, k_ref, v_ref):
    kn = _head_rms(p[:, :KV_W], kn_ref[...])
    kr = _rope(kn, cos_ref[...], sin_ref[...])
    k = jnp.where(in_ctx, kn, kr).astype(BF16)
    v = p[:, KV_W:].astype(BF16)
    one_col = jnp.where(lax.broadcasted_iota(jnp.int32, (p.shape[0], HEAD), 1) == 0, 1.0, 0.0).astype(BF16)
    for h in range(KV_HEADS):
        k_ref[h] = k[:, h * HEAD:(h + 1) * HEAD]
        v_ref[h] = jnp.concatenate([v[:, h * HEAD:(h + 1) * HEAD], one_col], axis=1)


def _attn_kernel(*refs, nseg, rope):
    q_ref, qn_ref = refs[0], refs[1]
    pos = 2
    if rope:
        cos_ref, sin_ref = refs[2], refs[3]
        pos = 4
    kv_refs = refs[pos:pos + 2 * nseg]
    o_ref = refs[pos + 2 * nseg]
    q = _head_rms(q_ref[...], qn_ref[...])
    if rope:
        q = _rope(q, cos_ref[...], sin_ref[...])
    q = (q * (HEAD ** -0.5 * math.log2(math.e))).astype(BF16)
    outs = []
    for g in range(GROUPS):
        qh = q[:, g * HEAD:(g + 1) * HEAD]
        s = [lax.dot_general(qh, kv_refs[2 * i][0], (((1,), (1,)), ((), ())), preferred_element_type=F32)
             for i in range(nseg)]
        m = s[0].max(axis=-1, keepdims=True)
        for si in s[1:]:
            m = jnp.maximum(m, si.max(axis=-1, keepdims=True))
        acc = None
        for i in range(nseg):
            pv = _dot(jnp.exp2(s[i] - m).astype(BF16), kv_refs[2 * i + 1][0])
            acc = pv if acc is None else acc + pv
        outs.append(acc[:, :HEAD] / acc[:, HEAD:HEAD + 1])
    o_ref[...] = jnp.concatenate(outs, axis=1)


def _attention(p_q, k, v, q_norm, cos, sin_signed, nb, nc, ctx, seq, latent):
    qlen = seq if latent else ctx
    tq = _tile(1024, qlen)
    nqt = qlen // tq
    qw = GROUPS * HEAD
    qoff = nc // tq if latent else 0
    qmap = lambda bi, g, t: (qoff + bi * nqt + t, g)
    in_specs = [pl.BlockSpec((tq, qw), qmap), pl.BlockSpec((1, qw), lambda bi, g, t: (0, 0))]
    args = [p_q, jnp.tile(q_norm, GROUPS).reshape(1, qw)]
    ctx_spec = lambda a: pl.BlockSpec((1, ctx, a.shape[-1]), lambda bi, g, t: (g, bi, 0))
    if latent:
        in_specs += [pl.BlockSpec((tq, qw), lambda bi, g, t: (t, 0))] * 2
        args += [cos, sin_signed]
        lat_spec = lambda a: pl.BlockSpec((1, seq, a.shape[-1]), lambda bi, g, t: (g, nc // seq + bi, 0))
        in_specs += [lat_spec(k), lat_spec(v), ctx_spec(k), ctx_spec(v)]
        args += [k, v, k, v]
    else:
        in_specs += [ctx_spec(k), ctx_spec(v)]
        args += [k, v]
    return pl.pallas_call(
        functools.partial(_attn_kernel, nseg=2 if latent else 1, rope=latent),
        grid=(nb, KV_HEADS, nqt),
        in_specs=in_specs,
        out_specs=pl.BlockSpec((tq, qw), lambda bi, g, t: (bi * nqt + t, g)),
        out_shape=jax.ShapeDtypeStruct((nb * qlen, ATT_W), F32),
        compiler_params=_params("parallel", "parallel", "parallel"),
        name="attn_lat" if latent else "attn_ctx",
    )(*args)


def _merge_kernel(x_ref, mod_ref, yf_ref, yb_ref, bonus_ref, g_ref, yc_ref, yl_ref, pg_ref, lw_ref, lb_ref, wpa_ref,
                  wpb_ref, wo_ref, o_ref, *, nct):
    D = x_ref.shape[-1]
    y_att = yl_ref[...] if nct == 0 else jnp.where(pl.program_id(0) < nct, yc_ref[...], yl_ref[...])
    ones = _head_ones(RWKV_W)
    y = yf_ref[0] + yb_ref[0]
    mean = _dot2(y, ones) * (1.0 / HEAD)
    yc = y - mean
    var = _dot2(yc * yc, ones) * (1.0 / HEAD)
    yn = yc * lax.rsqrt(var + LNX_EPS) * lw_ref[...] + lb_ref[...]
    y_rwkv = (yn + bonus_ref[...]) * g_ref[...]
    pg = pg_ref[...]
    mix = (_sigmoid(pg[:, :D]) * _dot(y_rwkv.astype(BF16), wpa_ref[...])
           + _sigmoid(pg[:, D:]) * _dot(y_att.astype(BF16), wpb_ref[...]))
    o_ref[...] = x_ref[...] + mod_ref[0][2:3] * _dot(mix.astype(BF16), wo_ref[...])


def _merge(xf, mod, y_fwd, y_bwd, bonus, g, y_ctx, y_lat, p_g, lnx_w, lnx_b, w_pa, w_pb, w_o, nc, seq, nb, row0):
    N, D = xf.shape
    W = RWKV_W
    tm = _tile(512, nc // y_fwd.shape[0], seq)
    off = row0 // tm
    n_out = N - row0
    mod_idx = _mod_index(tm, nc, seq, nb)
    rows = lambda w: pl.BlockSpec((tm, w), lambda i: (i + off, 0))
    slot_index = _slot_rows(tm, nc, N - nc, y_fwd.shape[0])
    scan_rows = pl.BlockSpec((1, tm, W), lambda i: slot_index(i + off))
    full = lambda a: pl.BlockSpec(a.shape, lambda i: (0,) * a.ndim)
    consts = (lnx_w.reshape(1, W), lnx_b.reshape(1, W), w_pa, w_pb, w_o)
    nct = (nc - row0) // tm
    att_ctx = pl.BlockSpec((tm, W), lambda i: (jnp.minimum(i, max(nct - 1, 0)), 0))
    att_lat = pl.BlockSpec((tm, W), lambda i: (jnp.maximum(i - nct, 0), 0))
    return pl.pallas_call(
        functools.partial(_merge_kernel, nct=nct),
        grid=(n_out // tm,),
        in_specs=[rows(D), pl.BlockSpec((1, 8, D), lambda i: mod_idx(i + off)),
                  scan_rows, scan_rows, rows(W), rows(W), att_ctx, att_lat, rows(2 * D)]
                 + [full(a) for a in consts],
        out_specs=pl.BlockSpec((tm, D), lambda i: (i, 0)),
        out_shape=jax.ShapeDtypeStruct((n_out, D), F32),
        compiler_params=_params("parallel"),
        name="merge",
    )(xf, mod, y_fwd, y_bwd, bonus, g, y_ctx, y_lat, p_g, *consts)


def _ffn_kernel(x_ref, mod_ref, g_ref, wg_ref, wu_ref, wd_ref, o_ref, h_ref, acc_ref):
    f = pl.program_id(1)

    @pl.when(f == 0)
    def _():
        m = mod_ref[0]
        h_ref[...] = _normmod(x_ref[...], g_ref[...], m[4:5], m[3:4]).astype(BF16)
        acc_ref[...] = jnp.zeros_like(acc_ref)

    h = h_ref[...]
    a = _dot(h, wg_ref[...])
    z = a * _sigmoid(a) * _dot(h, wu_ref[...])
    acc_ref[...] += _dot(z.astype(BF16), wd_ref[...])

    @pl.when(f == pl.num_programs(1) - 1)
    def _():
        o_ref[...] = x_ref[...] + mod_ref[0][5:6] * acc_ref[...]


def _ffn(xf, mod, gain, wg, wu, wd, nc, seq, nb):
    N, D = xf.shape
    F = wg.shape[-1]
    tm = _tile(1024, nc, seq)
    tf = _ffn_tile(F)
    return pl.pallas_call(
        _ffn_kernel,
        grid=(N // tm, F // tf),
        in_specs=[pl.BlockSpec((tm, D), lambda i, f: (i, 0)),
                  pl.BlockSpec((1, 8, D), lambda i, f: _mod_index(tm, nc, seq, nb)(i)),
                  pl.BlockSpec((1, D), lambda i, f: (0, 0)),
                  pl.BlockSpec((D, tf), lambda i, f: (0, f)),
                  pl.BlockSpec((D, tf), lambda i, f: (0, f)),
                  pl.BlockSpec((tf, D), lambda i, f: (f, 0))],
        out_specs=pl.BlockSpec((tm, D), lambda i, f: (i, 0)),
        out_shape=jax.ShapeDtypeStruct((N, D), F32),
        scratch_shapes=[pltpu.VMEM((tm, D), BF16), pltpu.VMEM((tm, D), F32)],
        compiler_params=_params("parallel", "arbitrary"),
        name="ffn",
    )(xf, mod, gain.reshape(1, D), wg, wu, wd)


R_E1, R_E2, R_RANK1, R_RANK2, R_W1, R_W2 = range(6)


def _router_kernel(x_ref, mod_ref, g_ref, wr_ref, route_ref, cnt_ref, carry_ref):
    @pl.when(pl.program_id(0) == 0)
    def _():
        carry_ref[...] = jnp.zeros_like(carry_ref)

    m = mod_ref[0]
    h = _normmod(x_ref[...], g_ref[...], m[4:5], m[3:4])
    tm = h.shape[0]
    lane = lax.broadcasted_iota(jnp.int32, (tm, LANES), 1).astype(F32)
    logits = jnp.where(lane < N_EXPERTS, _dot3(h, wr_ref[...]), NEG_BIG)
    m1 = logits.max(axis=-1, keepdims=True)
    i1 = jnp.where(logits == m1, lane, float(LANES)).min(axis=-1, keepdims=True)
    rest = jnp.where(lane == i1, NEG_BIG, logits)
    m2 = rest.max(axis=-1, keepdims=True)
    i2 = jnp.where(rest == m2, lane, float(LANES)).min(axis=-1, keepdims=True)
    e2 = jnp.exp(m2 - m1)
    w1 = 1.0 / (1.0 + e2)
    w2 = e2 / (1.0 + e2)
    onehot = jnp.where(lane == i1, 1.0, jnp.where(lane == i2, 1.0, 0.0))
    earlier = (lax.broadcasted_iota(jnp.int32, (tm, tm), 0) > lax.broadcasted_iota(jnp.int32, (tm, tm), 1))
    before = _dot(jnp.where(earlier, 1.0, 0.0).astype(BF16), onehot.astype(BF16)) + carry_ref[...]
    rank1 = jnp.where(lane == i1, before, 0.0).sum(axis=-1, keepdims=True)
    rank2 = jnp.where(lane == i2, before, 0.0).sum(axis=-1, keepdims=True)
    carry_ref[...] += onehot.sum(axis=0, keepdims=True)
    cnt_ref[...] = jnp.broadcast_to(carry_ref[...], cnt_ref.shape)
    rec = jnp.zeros_like(lane)
    for idx, val in ((R_E1, i1), (R_E2, i2), (R_RANK1, rank1), (R_RANK2, rank2), (R_W1, w1), (R_W2, w2)):
        rec = jnp.where(lane == float(idx), val, rec)
    route_ref[...] = rec


def _router(xl, mod, gain, router, seq):
    N, D = xl.shape
    tm = _tile(512, seq)
    wr = jnp.zeros((D, LANES), F32).at[:, :N_EXPERTS].set(router)
    return pl.pallas_call(
        _router_kernel,
        grid=(N // tm,),
        in_specs=[pl.BlockSpec((tm, D), lambda i: (i, 0)),
                  pl.BlockSpec((1, 8, D), lambda i: ((i * tm) // seq, 0, 0)),
                  pl.BlockSpec((1, D), lambda i: (0, 0)),
                  pl.BlockSpec((D, LANES), lambda i: (0, 0))],
        out_specs=[pl.BlockSpec((tm, LANES), lambda i: (i, 0)), pl.BlockSpec((8, LANES), lambda i: (0, 0))],
        out_shape=[jax.ShapeDtypeStruct((N, LANES), F32), jax.ShapeDtypeStruct((8, LANES), F32)],
        scratch_shapes=[pltpu.VMEM((1, LANES), F32)],
        compiler_params=_params("arbitrary"),
        name="router",
    )(xl, mod, gain.reshape(1, D), wr)


def _all_rows(n, copies):
    def start(r, c):
        for p, copy in enumerate(copies):
            copy(r).start(priority=p % 2)
        return c

    def wait(r, c):
        for copy in copies:
            copy(r).wait()
        return c

    lax.fori_loop(0, n, start, 0, unroll=8)
    lax.fori_loop(0, n, wait, 0, unroll=8)


def _dispatch_kernel(dest_ref, x_ref, mod_ref, g_ref, xs_in_ref, xs_ref, h_ref, sem):
    del xs_in_ref
    tm = x_ref.shape[0]
    n = tm * pl.num_programs(0)
    base = pl.program_id(0) * tm
    m = mod_ref[0]
    h_ref[...] = _normmod(x_ref[...], g_ref[...], m[4:5], m[3:4])
    _all_rows(tm, [lambda r, off=off: pltpu.make_async_copy(
        h_ref.at[pl.ds(r, 1)], xs_ref.at[pl.ds(dest_ref[off + base + r], 1)], sem) for off in (0, n)])


def _dispatch(dest, xl, mod, gain, xs, seq):
    N, D = xl.shape
    tm = _tile(512, seq)
    return pl.pallas_call(
        _dispatch_kernel,
        grid_spec=pltpu.PrefetchScalarGridSpec(
            num_scalar_prefetch=1,
            grid=(N // tm,),
            in_specs=[pl.BlockSpec((tm, D), lambda i, dest: (i, 0)),
                      pl.BlockSpec((1, 8, D), lambda i, dest: ((i * tm) // seq, 0, 0)),
                      pl.BlockSpec((1, D), lambda i, dest: (0, 0)),
                      pl.BlockSpec(memory_space=pl.ANY)],
            out_specs=pl.BlockSpec(memory_space=pl.ANY),
            scratch_shapes=[pltpu.VMEM((tm, D), F32), pltpu.SemaphoreType.DMA]),
        out_shape=jax.ShapeDtypeStruct(xs.shape, xs.dtype),
        input_output_aliases={4: 0},
        compiler_params=_params("arbitrary"),
        name="moe_dispatch",
    )(dest, xl, mod, gain.reshape(1, D), xs)


def _expert_kernel(tbl_ref, xs_ref, wg_ref, wu_ref, wd_ref, ys_ref, hb_ref, acc_ref, *, nblk):
    i = pl.program_id(0)
    f = pl.program_id(1)
    used = i < tbl_ref[nblk]
    last = f == pl.num_programs(1) - 1

    @pl.when(used & (f == 0))
    def _():
        hb_ref[...] = xs_ref[...].astype(BF16)
        acc_ref[...] = jnp.zeros_like(acc_ref)

    @pl.when(used)
    def _():
        h = hb_ref[...]
        a = _dot(h, wg_ref[0])
        z = a * _sigmoid(a) * _dot(h, wu_ref[0])
        acc_ref[...] += _dot(z.astype(BF16), wd_ref[0])

    @pl.when(used & last)
    def _():
        ys_ref[...] = acc_ref[...]

    @pl.when(jnp.logical_not(used) & last)
    def _():
        ys_ref[...] = jnp.zeros_like(ys_ref)


def _experts(tbl, xs, wg, wu, wd, blk):
    R, D = xs.shape
    F = wg.shape[-1]
    nblk = R // blk
    tf = _ffn_tile(F)
    return pl.pallas_call(
        functools.partial(_expert_kernel, nblk=nblk),
        grid_spec=pltpu.PrefetchScalarGridSpec(
            num_scalar_prefetch=1,
            grid=(nblk, F // tf),
            in_specs=[pl.BlockSpec((blk, D), lambda i, f, tbl: (i, 0)),
                      pl.BlockSpec((1, D, tf), lambda i, f, tbl: (tbl[i], 0, f)),
                      pl.BlockSpec((1, D, tf), lambda i, f, tbl: (tbl[i], 0, f)),
                      pl.BlockSpec((1, tf, D), lambda i, f, tbl: (tbl[i], f, 0))],
            out_specs=pl.BlockSpec((blk, D), lambda i, f, tbl: (i, 0)),
            scratch_shapes=[pltpu.VMEM((blk, D), BF16), pltpu.VMEM((blk, D), F32)]),
        out_shape=jax.ShapeDtypeStruct((R, D), F32),
        compiler_params=_params("parallel", "arbitrary"),
        name="moe_experts",
    )(tbl, xs, wg, wu, wd)


def _combine_kernel(dest_ref, x_ref, route_ref, mod_ref, fn_ref, ys_ref, o_ref, buf_ref, sem):
    tm = x_ref.shape[0]
    n = tm * pl.num_programs(0)
    base = pl.program_id(0) * tm
    _all_rows(tm, [lambda r, slot=slot: pltpu.make_async_copy(
        ys_ref.at[pl.ds(dest_ref[slot * n + base + r], 1)], buf_ref.at[pl.ds(slot * tm + r, 1)], sem)
        for slot in (0, 1)])
    mix = route_ref[:, R_W1:R_W1 + 1] * buf_ref[:tm] + route_ref[:, R_W2:R_W2 + 1] * buf_ref[tm:]
    y = x_ref[...] + mod_ref[0][5:6] * mix
    ms = jnp.mean(y * y, axis=-1, keepdims=True)
    o_ref[...] = y * lax.rsqrt(ms + NORM_EPS) * fn_ref[...]


def _combine(dest, acc, route, mod, final_norm, ys, seq):
    N, D = acc.shape
    tm = _tile(512, seq)
    return pl.pallas_call(
        _combine_kernel,
        grid_spec=pltpu.PrefetchScalarGridSpec(
            num_scalar_prefetch=1,
            grid=(N // tm,),
            in_specs=[pl.BlockSpec((tm, D), lambda i, dest: (i, 0)),
                      pl.BlockSpec((tm, LANES), lambda i, dest: (i, 0)),
                      pl.BlockSpec((1, 8, D), lambda i, dest: ((i * tm) // seq, 0, 0)),
                      pl.BlockSpec((1, D), lambda i, dest: (0, 0)),
                      pl.BlockSpec(memory_space=pl.ANY)],
            out_specs=pl.BlockSpec((tm, D), lambda i, dest: (i, 0)),
            scratch_shapes=[pltpu.VMEM((2 * tm, D), F32), pltpu.SemaphoreType.DMA]),
        out_shape=jax.ShapeDtypeStruct((N, D), F32),
        compiler_params=_params("arbitrary"),
        name="moe_combine",
    )(dest, acc, route, mod, final_norm.reshape(1, D), ys)


def _moe_final(xl, mod, gain, router, wg, wu, wd, final_norm, seq):
    N, D = xl.shape
    blk = MOE_BLOCK_ROWS
    route, counts = _router(xl, mod, gain, router, seq)
    counts = counts[0, :N_EXPERTS].astype(jnp.int32)
    padded = (counts + blk - 1) // blk * blk
    ends = jnp.cumsum(padded)
    starts = ends - padded
    nblk = -(-(2 * N) // blk) + N_EXPERTS
    blk_expert = jnp.minimum(jnp.searchsorted(ends, jnp.arange(nblk, dtype=jnp.int32) * blk, side='right'),
                             N_EXPERTS - 1).astype(jnp.int32)
    tbl = jnp.concatenate([blk_expert, (ends[-1:] // blk).astype(jnp.int32)])
    dest = jnp.concatenate([starts[route[:, e].astype(jnp.int32)] + route[:, r].astype(jnp.int32)
                            for e, r in ((R_E1, R_RANK1), (R_E2, R_RANK2))])
    xs = _dispatch(dest, xl, mod, gain, jnp.zeros((nblk * blk, D), F32), seq)
    ys = _experts(tbl, xs, wg, wu, wd, blk)
    return _combine(dest, xl, route, mod, final_norm, ys, seq)


def _rope_tables(seq):
    rows = seq // GRID_W
    row, col = jnp.meshgrid(jnp.arange(rows), jnp.arange(GRID_W), indexing='ij')
    axis = HEAD // 2
    inv = ROPE_THETA ** (-jnp.arange(0, axis, 2, dtype=F32) / axis)
    ang = jnp.concatenate([row.reshape(-1, 1).astype(F32) * inv, col.reshape(-1, 1).astype(F32) * inv], axis=-1)
    cos = jnp.repeat(jnp.cos(ang), 2, axis=-1)
    sin = jnp.repeat(jnp.sin(ang), 2, axis=-1) * jnp.tile(jnp.array([-1.0, 1.0], F32), HEAD // 2)
    return jnp.tile(cos, (1, GROUPS)), jnp.tile(sin, (1, GROUPS))


def kernel(x, c, ctx, c_ctx, ada_w, ada_b, norm1, norm2, w_in, shift_mu, rwkv_w0, rwkv_w2, rwkv_a0, rwkv_a2, rwkv_g2, rwkv_kk, rwkv_ka, rwkv_rk, lnx_w, lnx_b, q_norm, k_norm, w_pa, w_pb, w_o, ffn_wg, ffn_wu, ffn_wd, router, moe_wg, moe_wu, moe_wd, final_norm):
    B, T, D = x.shape
    CX = ctx.shape[1]
    depth = ada_w.shape[0]
    assert depth == 2 and ffn_wg.shape[0] == 1 and router.shape[0] == 1
    nc = B * CX
    bf = lambda w: w.astype(BF16)

    nrow = -(-(B + 1) // 8) * 8
    act = jnp.zeros((nrow, D), F32).at[:B].set(c).at[B].set(c_ctx)
    mod = _ada(act, ada_w, ada_b).reshape(depth, nrow, 6, D)
    mod = jnp.concatenate([mod, jnp.zeros((depth, nrow, 2, D), F32)], axis=2)

    cos, sin_signed = _rope_tables(T)
    xf = jnp.concatenate([ctx.reshape(nc, D), x.reshape(B * T, D)], axis=0)
    out = None
    for l in range(depth):
        last = l == depth - 1
        p_r, p_q, k_att, v_att, p_g = _in_proj(xf, mod[l], norm1[l], bf(w_in[l]), k_norm[l], cos, sin_signed, nc, T, B)
        r, v, nkk, g, bonus, lw, keys, b = _rwkv_prep(
            p_r, shift_mu[l], rwkv_w0[l], rwkv_w2[l], rwkv_a0[l], rwkv_a2[l], rwkv_g2[l], rwkv_kk[l], rwkv_ka[l],
            rwkv_rk[l].reshape(-1), nc, CX, T)
        y_fwd, y_bwd = _wkv_scan_pairs(r, v, nkk, lw, keys, b, B, CX, T)
        y_lat = _attention(p_q, k_att, v_att, q_norm[l], cos, sin_signed, B, nc, CX, T, latent=True)
        y_ctx = y_lat if last else _attention(p_q, k_att, v_att, q_norm[l], cos, sin_signed, B, nc, CX, T,
                                              latent=False)
        xm = _merge(xf, mod[l], y_fwd, y_bwd, bonus, g, y_ctx, y_lat, p_g, lnx_w[l], lnx_b[l], bf(w_pa[l]),
                    bf(w_pb[l]), bf(w_o[l]), nc, T, B, nc if last else 0)
        if not last:
            xf = _ffn(xm, mod[l], norm2[l], bf(ffn_wg[l // 2]), bf(ffn_wu[l // 2]), bf(ffn_wd[l // 2]), nc, T, B)
        else:
            out = _moe_final(xm, mod[l], norm2[l], router[l // 2], bf(moe_wg[l // 2]), bf(moe_wu[l // 2]),
                             bf(moe_wd[l // 2]), final_norm, T)
    return out.reshape(B, T, D)
```
